```python
import math
import jax, jax.numpy as jnp
from jax import lax
import numpy as np

D_MODEL = 4096
BATCH = 1
SEQ = 16384
DEPTH = 4

GRID_W = 64
CTX_LEN = 256
N_MIXERS = 3
MOD_RANK = 256
NORM_EPS = 1e-6
CONV_K = 31
SSD_D_INNER = 2 * D_MODEL
SSD_HEADDIM = 64
SSD_HEADS = SSD_D_INNER // SSD_HEADDIM
SSD_GROUPS = 8
SSD_HPG = SSD_HEADS // SSD_GROUPS
SSD_STATE = 128
SSD_CONV_K = 5
SSD_CHUNK = 128
SSD_CONV_DIM = SSD_D_INNER + 2 * SSD_GROUPS * SSD_STATE
SSD_IN_DIM = SSD_D_INNER + SSD_CONV_DIM + 2 * SSD_HEADS
SGU_CHUNK = 128
SGU_WIDTH = 2 * D_MODEL
SGU_GROUPS = 16
FFN_HIDDEN = 5 * D_MODEL // 2
FFN_CONV_K = 3

kernel_name = 'hybrid_conv_ssd_sgu_flow_trunk'


def _rmsnorm(x, g):
    xf = x.astype(jnp.float32)
    y = xf * lax.rsqrt(jnp.mean(xf * xf, axis=-1, keepdims=True) + NORM_EPS)
    return (y * g.astype(jnp.float32)).astype(x.dtype)


def _layernorm(x, g, b):
    xf = x.astype(jnp.float32)
    mu = jnp.mean(xf, axis=-1, keepdims=True)
    var = jnp.mean(jnp.square(xf - mu), axis=-1, keepdims=True)
    y = (xf - mu) * lax.rsqrt(var + NORM_EPS)
    return (y * g.astype(jnp.float32) + b.astype(jnp.float32)).astype(x.dtype)


def _dwconv1d(x, w, b):
    y = lax.conv_general_dilated(x, w[:, None, :].astype(x.dtype), (1,), 'SAME',
                                 dimension_numbers=('NWC', 'WIO', 'NWC'),
                                 feature_group_count=x.shape[-1])
    return y + b


def _dwconv2d(x, w, b):
    y = lax.conv_general_dilated(x, w[:, :, None, :].astype(x.dtype), (1, 1), 'SAME',
                                 dimension_numbers=('NHWC', 'HWIO', 'NHWC'),
                                 feature_group_count=x.shape[-1])
    return y + b


def _grid_conv(x, w, b):
    bsz, length, ch = x.shape
    rows = length // GRID_W
    return _dwconv2d(x.reshape(bsz, rows, GRID_W, ch), w, b).reshape(bsz, length, ch)


def _adaln(v, wa, wb, b):
    return (jax.nn.silu(v) @ wa) @ wb + b


def _prenorm(x, g, shift, scale):
    return _rmsnorm(x, g) * (1 + scale) + shift


def _conformer_conv(h, w1, b1, dw, dw_b, ln_g, ln_b, w2, b2, layout):
    a = h @ w1 + b1
    g = a[..., :D_MODEL] * jax.nn.sigmoid(a[..., D_MODEL:])
    if layout == 'seq':
        g = _dwconv1d(g, dw, dw_b)
    elif layout == 'rows':
        g = _grid_conv(g, dw[None], dw_b)
    else:
        g = _grid_conv(g, dw[:, None], dw_b)
    g = jax.nn.silu(_layernorm(g, ln_g, ln_b))
    return g @ w2 + b2


def _ssd_inputs(proj_rest, conv_w, conv_b, dt_bias):
    bsz, length, _ = proj_rest.shape
    xbc = jax.nn.silu(_dwconv1d(proj_rest[..., :SSD_CONV_DIM], conv_w, conv_b)).astype(jnp.float32)
    dt_raw = proj_rest[..., SSD_CONV_DIM:].astype(jnp.float32).reshape(bsz, length, 2, SSD_GROUPS, SSD_HPG)
    dt = jax.nn.softplus(dt_raw + dt_bias.astype(jnp.float32).reshape(2, SSD_GROUPS, SSD_HPG))
    gn = SSD_GROUPS * SSD_STATE
    xs = xbc[..., :SSD_D_INNER].reshape(bsz, length, SSD_GROUPS, SSD_HPG, SSD_HEADDIM)
    bm = xbc[..., SSD_D_INNER:SSD_D_INNER + gn].reshape(bsz, length, SSD_GROUPS, SSD_STATE)
    cm = xbc[..., SSD_D_INNER + gn:].reshape(bsz, length, SSD_GROUPS, SSD_STATE)
    return xs, bm, cm, dt


def _ssd_scan(xs, dt, a, bm, cm, state0):
    bsz, length = xs.shape[:2]
    n_chunks = length // SSD_CHUNK

    def chunks(t):
        return jnp.moveaxis(t.reshape((bsz, n_chunks, SSD_CHUNK) + t.shape[2:]), 1, 0)

    lower = jnp.tril(jnp.ones((SSD_CHUNK, SSD_CHUNK), dtype=bool))[None, :, :, None, None]

    def step(state, inp):
        xq, dtq, bq, cq = inp
        cum = jnp.cumsum(dtq * a, axis=1)
        seg = jnp.where(lower, cum[:, :, None] - cum[:, None, :], -jnp.inf)
        mix = jnp.einsum('btgn,bsgn->btsg', cq, bq)[..., None] * jnp.exp(seg)
        xdt = xq * dtq[..., None]
        y = jnp.einsum('btsgr,bsgrp->btgrp', mix, xdt)
        y = y + jnp.einsum('btgn,bgrpn->btgrp', cq, state) * jnp.exp(cum)[..., None]
        last = cum[:, -1]
        w = jnp.exp(last[:, None] - cum)
        state = state * jnp.exp(last)[..., None, None] + jnp.einsum('bsgr,bsgrp,bsgn->bgrpn', w, xdt, bq)
        return state, y

    state, y = lax.scan(step, state0, (chunks(xs), chunks(dt), chunks(bm), chunks(cm)))
    return jnp.moveaxis(y, 0, 1).reshape(xs.shape), state


def _ssd_final_state(xs, dt, a, bm):
    cum = jnp.cumsum(dt * a, axis=1)
    w = jnp.exp(cum[:, -1:] - cum)
    return jnp.einsum('bsgr,bsgrp,bsgn->bgrpn', w, xs * dt[..., None], bm)


def _ssd_out(y, z, norm_g, w_out, dtype):
    bsz, length = y.shape[:2]
    gw = SSD_D_INNER // SSD_GROUPS
    y = y.reshape(bsz, length, SSD_GROUPS, gw) * jax.nn.silu(z.astype(jnp.float32)).reshape(bsz, length, SSD_GROUPS, gw)
    y = y * lax.rsqrt(jnp.mean(y * y, axis=-1, keepdims=True) + NORM_EPS)
    y = y.reshape(bsz, length, SSD_D_INNER) * norm_g.astype(jnp.float32)
    return y.astype(dtype) @ w_out


def _rev(t, flip):
    return jnp.flip(t, axis=1) if flip else t


def _ssd_mixer(hc, hl, w_in, conv_w, conv_b, a_log, dt_bias, d_skip, norm_g, w_out, ctx_out):
    a = -jnp.exp(a_log.astype(jnp.float32)).reshape(2, SSD_GROUPS, SSD_HPG)
    d = d_skip.astype(jnp.float32).reshape(SSD_GROUPS, SSD_HPG, 1)
    proj_l = hl @ w_in
    xl, bl, cl, dtl = _ssd_inputs(proj_l[..., SSD_D_INNER:], conv_w, conv_b, dt_bias)
    if ctx_out:
        proj_c = hc @ w_in
    else:
        proj_c = hc @ w_in[:, SSD_D_INNER:]
    xc, bc, cc, dtc = _ssd_inputs(proj_c[..., -(SSD_CONV_DIM + 2 * SSD_HEADS):], conv_w, conv_b, dt_bias)
    zero_state = jnp.zeros((hc.shape[0], SSD_GROUPS, SSD_HPG, SSD_HEADDIM, SSD_STATE), jnp.float32)
    yl = xl * d
    yc = xc * d if ctx_out else None
    for direction in range(2):
        flip = direction == 1
        if ctx_out:
            y_dir, state = _ssd_scan(_rev(xc, flip), _rev(dtc[:, :, direction], flip), a[direction],
                                     _rev(bc, flip), _rev(cc, flip), zero_state)
            yc = yc + _rev(y_dir, flip)
        else:
            state = _ssd_final_state(_rev(xc, flip), _rev(dtc[:, :, direction], flip), a[direction],
                                     _rev(bc, flip))
        y_dir, _ = _ssd_scan(_rev(xl, flip), _rev(dtl[:, :, direction], flip), a[direction],
                             _rev(bl, flip), _rev(cl, flip), state)
        yl = yl + _rev(y_dir, flip)
    out_l = _ssd_out(yl, proj_l[..., :SSD_D_INNER], norm_g, w_out, hl.dtype)
    out_c = _ssd_out(yc, proj_c[..., :SSD_D_INNER], norm_g, w_out, hc.dtype) if ctx_out else None
    return out_c, out_l


def _chunk_sgu(h, w1, b1, ln_g, ln_b, ws, bs, w2, b2):
    a = jax.nn.gelu(h @ w1 + b1)
    u, v = a[..., :SGU_WIDTH], a[..., SGU_WIDTH:]
    v = _layernorm(v, ln_g, ln_b)
    bsz, length, _ = v.shape
    v = v.reshape(bsz, length // SGU_CHUNK, SGU_CHUNK, SGU_GROUPS, SGU_WIDTH // SGU_GROUPS)
    v = jnp.einsum('gts,bcsge->bctge', ws, v) + bs.T[:, :, None]
    return (u * v.reshape(bsz, length, SGU_WIDTH)) @ w2 + b2


def _conv_glu(h, w_in, dw, dw_b, w_out, on_grid):
    a = h @ w_in
    gate, up = a[..., :FFN_HIDDEN], a[..., FFN_HIDDEN:]
    if on_grid:
        gate = _grid_conv(gate, dw, dw_b)
    else:
        gate = _dwconv1d(gate, dw[FFN_CONV_K // 2], dw_b)
    return (jax.nn.gelu(gate) * up) @ w_out


def setup_inputs(seed: int = 0) -> dict:
    key = jax.random.key(seed)
    keys = list(jax.random.split(key, 48))

    def nrm(shape, scale):
        return jax.random.normal(keys.pop(), shape, jnp.float32) * scale

    def gain(shape):
        return 1.0 + nrm(shape, 0.1)

    n_a, n_b, n_c = (len(range(m, DEPTH, N_MIXERS)) for m in range(N_MIXERS))
    D = D_MODEL
    H = SSD_HEADS
    a_log = jnp.log(jax.random.uniform(keys.pop(), (n_b, 2, H), jnp.float32, 1.0, 16.0))
    u = jax.random.uniform(keys.pop(), (n_b, 2, H), jnp.float32)
    dt0 = jnp.exp(u * (math.log(0.1) - math.log(0.001)) + math.log(0.001))
    dt_bias = dt0 + jnp.log(-jnp.expm1(-dt0))
    return {
        'x': nrm((BATCH, SEQ, D), 1.0),
        'c': nrm((BATCH, D), 1.0),
        'ctx': nrm((BATCH, CTX_LEN, D), 1.0),
        'c_ctx': nrm((D,), 1.0),
        'norm_mix_g': gain((DEPTH, D)),
        'norm_ffn_g': gain((DEPTH, D)),
        'mod_wa': nrm((DEPTH, D, MOD_RANK), D ** -0.5),
        'mod_wb': nrm((DEPTH, MOD_RANK, 6 * D), 0.5 * MOD_RANK ** -0.5),
        'mod_b': nrm((DEPTH, 6 * D), 0.02),
        'conv_w1': nrm((n_a, D, 2 * D), D ** -0.5),
        'conv_b1': nrm((n_a, 2 * D), 0.02),
        'conv_dw': nrm((n_a, CONV_K, D), CONV_K ** -0.5),
        'conv_dw_b': nrm((n_a, D), 0.02),
        'conv_ln_g': gain((n_a, D)),
        'conv_ln_b': nrm((n_a, D), 0.02),
        'conv_w2': nrm((n_a, D, D), D ** -0.5),
        'conv_b2': nrm((n_a, D), 0.02),
        'ssd_w_in': nrm((n_b, D, SSD_IN_DIM), D ** -0.5),
        'ssd_conv_w': nrm((n_b, SSD_CONV_K, SSD_CONV_DIM), SSD_CONV_K ** -0.5),
        'ssd_conv_b': nrm((n_b, SSD_CONV_DIM), 0.02),
        'ssd_a_log': a_log,
        'ssd_dt_bias': dt_bias,
        'ssd_d': gain((n_b, H)),
        'ssd_norm_g': gain((n_b, SSD_D_INNER)),
        'ssd_w_out': nrm((n_b, SSD_D_INNER, D), SSD_D_INNER ** -0.5),
        'sgu_w1': nrm((n_c, D, 2 * SGU_WIDTH), D ** -0.5),
        'sgu_b1': nrm((n_c, 2 * SGU_WIDTH), 0.02),
        'sgu_ln_g': gain((n_c, SGU_WIDTH)),
        'sgu_ln_b': nrm((n_c, SGU_WIDTH), 0.02),
        'sgu_ws': nrm((n_c, SGU_GROUPS, SGU_CHUNK, SGU_CHUNK), SGU_CHUNK ** -0.5),
        'sgu_bs': 1.0 + nrm((n_c, SGU_GROUPS, SGU_CHUNK), 0.02),
        'sgu_w2': nrm((n_c, SGU_WIDTH, D), SGU_WIDTH ** -0.5),
        'sgu_b2': nrm((n_c, D), 0.02),
        'ffn_w_in': nrm((DEPTH, D, 2 * FFN_HIDDEN), D ** -0.5),
        'ffn_dw': nrm((DEPTH, FFN_CONV_K, FFN_CONV_K, FFN_HIDDEN), 1.0 / FFN_CONV_K),
        'ffn_dw_b': nrm((DEPTH, FFN_HIDDEN), 0.02),
        'ffn_w_out': nrm((DEPTH, FFN_HIDDEN, D), FFN_HIDDEN ** -0.5),
        'final_g': gain((D,)),
    }


def reference(x, c, ctx, c_ctx, norm_mix_g, norm_ffn_g, mod_wa, mod_wb, mod_b,
              conv_w1, conv_b1, conv_dw, conv_dw_b, conv_ln_g, conv_ln_b, conv_w2, conv_b2,
              ssd_w_in, ssd_conv_w, ssd_conv_b, ssd_a_log, ssd_dt_bias, ssd_d, ssd_norm_g, ssd_w_out,
              sgu_w1, sgu_b1, sgu_ln_g, sgu_ln_b, sgu_ws, sgu_bs, sgu_w2, sgu_b2,
              ffn_w_in, ffn_dw, ffn_dw_b, ffn_w_out, final_g):
    ssd_layers = [i for i in range(DEPTH) if i % N_MIXERS == 1]
    last_ctx = ssd_layers[-1] if ssd_layers else -1
    xl, xc = x, ctx
    for i in range(DEPTH):
        kind, k = i % N_MIXERS, i // N_MIXERS
        ctx_live = i <= last_ctx
        ctx_full = i < last_ctx
        mod_l = _adaln(c, mod_wa[i], mod_wb[i], mod_b[i])[:, None, :]
        sh1, sc1, g1, sh2, sc2, g2 = jnp.split(mod_l, 6, axis=-1)
        hl = _prenorm(xl, norm_mix_g[i], sh1, sc1)
        hc = None
        if ctx_live:
            mod_c = _adaln(c_ctx[None], mod_wa[i], mod_wb[i], mod_b[i])[:, None, :]
            csh1, csc1, cg1, csh2, csc2, cg2 = jnp.split(mod_c, 6, axis=-1)
            hc = _prenorm(xc, norm_mix_g[i], csh1, csc1)
        yc = None
        if kind == 0:
            p = (conv_w1[k], conv_b1[k], conv_dw[k], conv_dw_b[k], conv_ln_g[k], conv_ln_b[k], conv_w2[k], conv_b2[k])
            yl = _conformer_conv(hl, *p, 'rows' if k % 2 == 0 else 'cols')
            if ctx_full:
                yc = _conformer_conv(hc, *p, 'seq')
        elif kind == 1:
            yc, yl = _ssd_mixer(hc, hl, ssd_w_in[k], ssd_conv_w[k], ssd_conv_b[k], ssd_a_log[k],
                                ssd_dt_bias[k], ssd_d[k], ssd_norm_g[k], ssd_w_out[k], ctx_full)
        else:
            p = (sgu_w1[k], sgu_b1[k], sgu_ln_g[k], sgu_ln_b[k], sgu_ws[k], sgu_bs[k], sgu_w2[k], sgu_b2[k])
            yl = _chunk_sgu(hl, *p)
            if ctx_full:
                yc = _chunk_sgu(hc, *p)
        xl = xl + g1 * yl
        hl = _prenorm(xl, norm_ffn_g[i], sh2, sc2)
        xl = xl + g2 * _conv_glu(hl, ffn_w_in[i], ffn_dw[i], ffn_dw_b[i], ffn_w_out[i], True)
        if ctx_full:
            xc = xc + cg1 * yc
            hc = _prenorm(xc, norm_ffn_g[i], csh2, csc2)
            xc = xc + cg2 * _conv_glu(hc, ffn_w_in[i], ffn_dw[i], ffn_dw_b[i], ffn_w_out[i], False)
    return _rmsnorm(xl, final_g)
```

```python
import functools

import jax
import jax.numpy as jnp
from jax import lax
from jax.experimental import pallas as pl
from jax.experimental.pallas import tpu as pltpu

F32 = jnp.float32
BF16 = jnp.bfloat16
HI = lax.Precision.HIGHEST

NORM_EPS = 1e-6
GRID_W = 64
DEPTH_MIXERS = 3
SSD_GROUPS = 8
SSD_HEADDIM = 64
SSD_STATE = 128
SSD_CHUNK = 128
SGU_CHUNK = 128
SGU_GROUPS = 16
LANE = 128
SUBLANE = 8
CONV_LANES = 256
CONV_ROWS = 64
VMEM_LIMIT = 56 * 1024 * 1024


def _cp(*sem):
    return pltpu.CompilerParams(dimension_semantics=sem, vmem_limit_bytes=VMEM_LIMIT)


def _tile(n, pref, align):
    if n <= pref:
        return n
    t = (pref // align) * align
    while t >= align:
        if n % t == 0:
            return t
        t -= align
    return n


def _row(v):
    return v.reshape(1, -1)


def _prenorm_math(x, g, sc, sh):
    ms = jnp.mean(x * x, axis=-1, keepdims=True)
    return (x * lax.rsqrt(ms + NORM_EPS)) * g * (1.0 + sc) + sh


def _silu(x):
    return x * jax.nn.sigmoid(x)


def _adaln_kernel(v_ref, wa_ref, wb_ref, b_ref, o_ref, t_ref):
    @pl.when(pl.program_id(1) == 0)
    def _():
        t_ref[...] = jnp.dot(_silu(v_ref[...]), wa_ref[0], preferred_element_type=F32, precision=HI)

    o_ref[0] = jnp.dot(t_ref[...], wb_ref[0], preferred_element_type=F32, precision=HI) + b_ref[0]


def _adaln(v8, wa, wb, b):
    depth, d, r = wa.shape
    n = wb.shape[2]
    tn = _tile(n, 6144, LANE)
    return pl.pallas_call(
        _adaln_kernel,
        grid=(depth, n // tn),
        in_specs=[pl.BlockSpec((SUBLANE, d), lambda l, j: (0, 0)),
                  pl.BlockSpec((1, d, r), lambda l, j: (l, 0, 0)),
                  pl.BlockSpec((1, r, tn), lambda l, j: (l, 0, j)),
                  pl.BlockSpec((1, 1, tn), lambda l, j: (l, 0, j))],
        out_specs=pl.BlockSpec((1, SUBLANE, tn), lambda l, j: (l, 0, j)),
        out_shape=jax.ShapeDtypeStruct((depth, SUBLANE, n), F32),
        scratch_shapes=[pltpu.VMEM((SUBLANE, r), F32)],
        compiler_params=_cp("arbitrary", "arbitrary"),
        name="adaln",
    )(v8, wa, wb, b.reshape(depth, 1, n))


def _prenorm_kernel(x_ref, g_ref, sc_ref, sh_ref, o_ref):
    o_ref[...] = _prenorm_math(x_ref[...], g_ref[...], sc_ref[...], sh_ref[...]).astype(o_ref.dtype)


def _prenorm(x, g, sc, sh):
    m, d = x.shape
    tm = _tile(m, 256, SUBLANE)
    vec = pl.BlockSpec((1, d), lambda i: (0, 0))
    return pl.pallas_call(
        _prenorm_kernel,
        grid=(m // tm,),
        in_specs=[pl.BlockSpec((tm, d), lambda i: (i, 0)), vec, vec, vec],
        out_specs=pl.BlockSpec((tm, d), lambda i: (i, 0)),
        out_shape=jax.ShapeDtypeStruct((m, d), BF16),
        compiler_params=_cp("arbitrary"),
        name="prenorm",
    )(x, _row(g), _row(sc), _row(sh))


def _ln_silu_kernel(x_ref, g_ref, b_ref, o_ref):
    x = x_ref[...]
    mu = jnp.mean(x, axis=-1, keepdims=True)
    xc = x - mu
    var = jnp.mean(xc * xc, axis=-1, keepdims=True)
    y = xc * lax.rsqrt(var + NORM_EPS) * g_ref[...] + b_ref[...]
    o_ref[...] = _silu(y).astype(o_ref.dtype)


def _ln_silu(x, g, b):
    m, d = x.shape
    tm = _tile(m, 256, SUBLANE)
    vec = pl.BlockSpec((1, d), lambda i: (0, 0))
    return pl.pallas_call(
        _ln_silu_kernel,
        grid=(m // tm,),
        in_specs=[pl.BlockSpec((tm, d), lambda i: (i, 0)), vec, vec],
        out_specs=pl.BlockSpec((tm, d), lambda i: (i, 0)),
        out_shape=jax.ShapeDtypeStruct((m, d), BF16),
        compiler_params=_cp("arbitrary"),
        name="ln_silu",
    )(x, _row(g), _row(b))


def _up_kernel(*refs, n_w, has_bias, epi):
    h_ref = refs[0]
    w_refs = refs[1:1 + n_w]
    b_refs = refs[1 + n_w:1 + 2 * n_w] if has_bias else ()
    o_ref = refs[-1]
    h = h_ref[...]
    accs = []
    for t in range(n_w):
        a = jnp.dot(h, w_refs[t][...], preferred_element_type=F32)
        if has_bias:
            a = a + b_refs[t][...]
        accs.append(a)
    o_ref[...] = epi(*accs).astype(o_ref.dtype)


def _epi_id(a):
    return a


def _epi_glu(a, b):
    return a * jax.nn.sigmoid(b)


def _epi_gelu(a):
    return jax.nn.gelu(a)


def _epi_softplus(a):
    return jnp.maximum(a, 0.0) + jnp.log1p(jnp.exp(-jnp.abs(a)))


def _up(h, w, n_out, col_offs, bias, epi, out_dtype, name, tm_pref=1024, tn_pref=512):
    m, k = h.shape
    tm = _tile(m, tm_pref, SUBLANE)
    tn = _tile(n_out, tn_pref, LANE)
    for off in col_offs:
        assert off % tn == 0
    n_w = len(col_offs)
    in_specs = [pl.BlockSpec((tm, k), lambda i, j: (i, 0))]
    args = [h]
    for off in col_offs:
        in_specs.append(pl.BlockSpec((k, tn), functools.partial(lambda i, j, o: (0, j + o), o=off // tn)))
        args.append(w)
    if bias is not None:
        for off in col_offs:
            in_specs.append(pl.BlockSpec((1, tn), functools.partial(lambda i, j, o: (0, j + o), o=off // tn)))
            args.append(_row(bias))
    return pl.pallas_call(
        functools.partial(_up_kernel, n_w=n_w, has_bias=bias is not None, epi=epi),
        grid=(m // tm, n_out // tn),
        in_specs=in_specs,
        out_specs=pl.BlockSpec((tm, tn), lambda i, j: (i, j)),
        out_shape=jax.ShapeDtypeStruct((m, n_out), out_dtype),
        compiler_params=_cp("arbitrary", "arbitrary"),
        name=name,
    )(*args)


def _dwconv_kernel(*refs, seg, pad, halo, ntaps, act):
    if halo:
        x_ref, xp_ref, xn_ref, w_ref, b_ref, o_ref, buf_ref = refs
    else:
        x_ref, w_ref, b_ref, o_ref, buf_ref = refs
    t_rows, c = x_ref.shape
    nseg = t_rows // seg
    stride = seg + 2 * pad
    if halo:
        i = pl.program_id(0)
        lo = jnp.where(i > 0, xp_ref[...], 0.0)
        hi = jnp.where(i < pl.num_programs(0) - 1, xn_ref[...], 0.0)
    else:
        lo = hi = jnp.zeros((pad, c), F32)
    for r in range(nseg):
        buf_ref[pl.ds(r * stride, pad), :] = lo
        buf_ref[pl.ds(r * stride + pad, seg), :] = x_ref[pl.ds(r * seg, seg), :]
        buf_ref[pl.ds(r * stride + pad + seg, pad), :] = hi
    half = ntaps // 2
    sub = min(seg, CONV_ROWS)
    lw = min(c, CONV_LANES)

    def body(ci, carry):
        lanes = pl.ds(pl.multiple_of(ci * lw, lw), lw)
        bias = b_ref[:, lanes]
        for r in range(nseg):
            for s in range(seg // sub):
                base = r * stride + pad + s * sub
                acc = None
                for k in range(ntaps):
                    v = buf_ref[pl.ds(base + k - half, sub), lanes] * w_ref[pl.ds(k, 1), lanes]
                    acc = v if acc is None else acc + v
                acc = acc + bias
                if act == "silu":
                    acc = _silu(acc)
                o_ref[pl.ds(r * seg + s * sub, sub), lanes] = acc
        return carry

    lax.fori_loop(0, c // lw, body, 0)


def _dwconv(x, w, b, *, seg, tile_rows, tile_lanes, pad, halo, act, name):
    rows, lanes = x.shape
    ntaps, c = w.shape
    tl = min(tile_lanes, c)
    assert lanes % tl == 0 and c % tl == 0 and rows % tile_rows == 0
    wrap = c // tl
    grid = (rows // tile_rows, lanes // tl)
    xspec = pl.BlockSpec((tile_rows, tl), lambda i, j: (i, j))
    wspec = pl.BlockSpec((ntaps, tl), lambda i, j: (0, j % wrap))
    bspec = pl.BlockSpec((1, tl), lambda i, j: (0, j % wrap))
    if halo:
        ratio = tile_rows // pad
        nhb = rows // pad
        pspec = pl.BlockSpec((pad, tl), lambda i, j: (jnp.maximum(i * ratio - 1, 0), j))
        nspec = pl.BlockSpec((pad, tl), lambda i, j: (jnp.minimum((i + 1) * ratio, nhb - 1), j))
        in_specs = [xspec, pspec, nspec, wspec, bspec]
        args = (x, x, x, w, _row(b))
        kseg = tile_rows
    else:
        in_specs = [xspec, wspec, bspec]
        args = (x, w, _row(b))
        kseg = seg
    nseg = tile_rows // kseg
    return pl.pallas_call(
        functools.partial(_dwconv_kernel, seg=kseg, pad=pad, halo=halo, ntaps=ntaps, act=act),
        grid=grid,
        in_specs=in_specs,
        out_specs=pl.BlockSpec((tile_rows, tl), lambda i, j: (i, j)),
        out_shape=jax.ShapeDtypeStruct((rows, lanes), F32),
        scratch_shapes=[pltpu.VMEM((nseg * (kseg + 2 * pad), tl), F32)],
        compiler_params=_cp("arbitrary", "arbitrary"),
        name=name,
    )(*args)


def _down_kernel(*refs, n_pro, n_scr, pro_fn, has_bias):
    pro_refs = refs[:n_pro]
    pos = n_pro
    w_ref = refs[pos]
    pos += 1
    if has_bias:
        bias_ref = refs[pos]
        pos += 1
    x_ref, gate_ref, ng_ref, sc_ref, sh_ref, xnew_ref, h_ref, acc_ref = refs[pos:pos + 8]
    scr = refs[pos + 8:pos + 8 + n_scr]
    k = pl.program_id(1)

    @pl.when(k == 0)
    def _():
        acc_ref[...] = jnp.zeros_like(acc_ref)

    a = pro_fn(pro_refs, scr)
    acc_ref[...] += jnp.dot(a, w_ref[...], preferred_element_type=F32)

    @pl.when(k == pl.num_programs(1) - 1)
    def _():
        y = acc_ref[...]
        if has_bias:
            y = y + bias_ref[...]
        xn = x_ref[...] + gate_ref[...] * y
        xnew_ref[...] = xn
        h_ref[...] = _prenorm_math(xn, ng_ref[...], sc_ref[...], sh_ref[...]).astype(h_ref.dtype)


def _down(pro_args, pro_specs, pro_fn, pro_scratch, w, bias, x, gate, ng, sc, sh, tm, tk, out_dtype, name):
    m, d = x.shape
    kdim = w.shape[0]
    vec = pl.BlockSpec((1, d), lambda i, k: (0, 0))
    rowblk = pl.BlockSpec((tm, d), lambda i, k: (i, 0))
    in_specs = list(pro_specs) + [pl.BlockSpec((tk, d), lambda i, k: (k, 0))]
    args = list(pro_args) + [w]
    if bias is not None:
        in_specs.append(vec)
        args.append(_row(bias))
    in_specs += [rowblk, vec, vec, vec, vec]
    args += [x, _row(gate), _row(ng), _row(sc), _row(sh)]
    return pl.pallas_call(
        functools.partial(_down_kernel, n_pro=len(pro_args), n_scr=len(pro_scratch), pro_fn=pro_fn,
                          has_bias=bias is not None),
        grid=(m // tm, kdim // tk),
        in_specs=in_specs,
        out_specs=[rowblk, rowblk],
        out_shape=[jax.ShapeDtypeStruct((m, d), F32), jax.ShapeDtypeStruct((m, d), out_dtype)],
        scratch_shapes=[pltpu.VMEM((tm, d), F32)] + list(pro_scratch),
        compiler_params=_cp("arbitrary", "arbitrary"),
        name=name,
    )(*args)


def _pro_direct(refs, scr):
    return refs[0][...]


def _down_direct(a, w, bias, x, gate, ng, sc, sh, out_dtype, name):
    m, kdim = a.shape
    tm = _tile(m, 256, SUBLANE)
    tk = _tile(kdim, 512, LANE)
    return _down([a], [pl.BlockSpec((tm, tk), lambda i, k: (i, k))], _pro_direct, [], w, bias,
                 x, gate, ng, sc, sh, tm, tk, out_dtype, name)


def _pro_ssd(refs, scr):
    y_ref, z_ref, g_ref = refs
    yz = y_ref[...] * _silu(z_ref[...])
    ms = jnp.mean(yz * yz, axis=-1, keepdims=True)
    return (yz * lax.rsqrt(ms + NORM_EPS) * g_ref[...]).astype(BF16)


def _down_ssd(y, z, norm_g, w, x, gate, ng, sc, sh, out_dtype, name):
    m, kdim = y.shape
    tm = _tile(m, 256, SUBLANE)
    tk = kdim // SSD_GROUPS
    blk = pl.BlockSpec((tm, tk), lambda i, k: (i, k))
    return _down([y, z, _row(norm_g)], [blk, blk, pl.BlockSpec((1, tk), lambda i, k: (0, k))],
                 _pro_ssd, [], w, None, x, gate, ng, sc, sh, tm, tk, out_dtype, name)


def _pro_ffn(refs, scr, *, on_grid):
    gc_ref, gp_ref, gn_ref, up_ref, dw_ref, dwb_ref = refs
    buf_ref, a_ref = scr
    tm, tk = gc_ref.shape
    hb = gp_ref.shape[0]
    i = pl.program_id(0)
    buf_ref[pl.ds(0, hb), :] = jnp.where(i > 0, gp_ref[...], 0.0)
    buf_ref[pl.ds(hb, tm), :] = gc_ref[...]
    buf_ref[pl.ds(hb + tm, hb), :] = jnp.where(i < pl.num_programs(0) - 1, gn_ref[...], 0.0)
    sub = min(tm, CONV_ROWS)
    lw = min(tk, CONV_LANES)
    col = lax.broadcasted_iota(jnp.int32, (sub, lw), 0)
    dys = (-1, 0, 1) if on_grid else (0,)
    for lc in range(tk // lw):
        lanes = pl.ds(lc * lw, lw)
        for s in range(tm // sub):
            base = hb + s * sub

            def tapsum(dx):
                acc = None
                for dy in dys:
                    v = (buf_ref[pl.ds(base + GRID_W * dy + dx, sub), lanes]
                         * dw_ref[pl.ds((dy + 1) * 3 + dx + 1, 1), lanes])
                    acc = v if acc is None else acc + v
                return acc

            left, right = tapsum(-1), tapsum(1)
            if on_grid:
                left = jnp.where(col >= 1, left, 0.0)
                right = jnp.where(col <= GRID_W - 2, right, 0.0)
            conv = tapsum(0) + left + right + dwb_ref[:, lanes]
            a_ref[pl.ds(s * sub, sub), lanes] = (
                jax.nn.gelu(conv) * up_ref[pl.ds(s * sub, sub), lanes]).astype(BF16)
    return a_ref[...]


def _down_ffn(a, dw, dw_b, w, x, gate, ng, sc, sh, on_grid, out_dtype, name):
    m = a.shape[0]
    f = w.shape[0]
    tm = _tile(m, 256, 2 * GRID_W)
    tk = _tile(f, 512, LANE)
    hb = 2 * GRID_W
    ratio = tm // hb
    nhb = m // hb
    koff = f // tk
    specs = [pl.BlockSpec((tm, tk), lambda i, k: (i, k)),
             pl.BlockSpec((hb, tk), lambda i, k: (jnp.maximum(i * ratio - 1, 0), k)),
             pl.BlockSpec((hb, tk), lambda i, k: (jnp.minimum((i + 1) * ratio, nhb - 1), k)),
             pl.BlockSpec((tm, tk), lambda i, k: (i, k + koff)),
             pl.BlockSpec((9, tk), lambda i, k: (0, k)),
             pl.BlockSpec((1, tk), lambda i, k: (0, k))]
    scratch = [pltpu.VMEM((tm + 2 * hb, tk), F32), pltpu.VMEM((tm, tk), BF16)]
    return _down([a, a, a, a, dw.reshape(9, f), _row(dw_b)], specs,
                 functools.partial(_pro_ffn, on_grid=on_grid), scratch, w, None,
                 x, gate, ng, sc, sh, tm, tk, out_dtype, name)


def _sgu_kernel(u_ref, v_ref, g_ref, b_ref, ws_ref, bs_ref, o_ref):
    v = v_ref[...]
    mu = jnp.mean(v, axis=-1, keepdims=True)
    vc = v - mu
    var = jnp.mean(vc * vc, axis=-1, keepdims=True)
    vn = (vc * lax.rsqrt(var + NORM_EPS) * g_ref[...] + b_ref[...]).astype(BF16)
    groups = ws_ref.shape[0]
    gw = v.shape[1] // groups
    for g in range(groups):
        mixed = jnp.dot(ws_ref[g], vn[:, g * gw:(g + 1) * gw], preferred_element_type=F32) + bs_ref[g]
        o_ref[:, g * gw:(g + 1) * gw] = (u_ref[:, g * gw:(g + 1) * gw] * mixed).astype(o_ref.dtype)


def _sgu_mix(a, ln_g, ln_b, ws, bs):
    m = a.shape[0]
    e = a.shape[1] // 2
    q = SGU_CHUNK
    groups = ws.shape[0]
    return pl.pallas_call(
        _sgu_kernel,
        grid=(m // q,),
        in_specs=[pl.BlockSpec((q, e), lambda c: (c, 0)),
                  pl.BlockSpec((q, e), lambda c: (c, 1)),
                  pl.BlockSpec((1, e), lambda c: (0, 0)),
                  pl.BlockSpec((1, e), lambda c: (0, 0)),
                  pl.BlockSpec((groups, q, q), lambda c: (0, 0, 0)),
                  pl.BlockSpec((groups, q, 1), lambda c: (0, 0, 0))],
        out_specs=pl.BlockSpec((q, e), lambda c: (c, 0)),
        out_shape=jax.ShapeDtypeStruct((m, e), BF16),
        compiler_params=_cp("arbitrary"),
        name="sgu_mix",
    )(a, a, _row(ln_g), _row(ln_b), ws.astype(BF16), bs.reshape(groups, q, 1))


def _ssd_kernel(*refs, reverse):
    if reverse:
        (x_ref, b_ref, c_ref, dt_ref, dtt_ref, al_ref, alt_ref, s0_ref, yprev_ref, d_ref,
         y_ref, sfin_ref, st_ref) = refs
    else:
        (x_ref, b_ref, c_ref, dt_ref, dtt_ref, al_ref, alt_ref, s0_ref,
         y_ref, sfin_ref, st_ref) = refs
    c = pl.program_id(1)

    @pl.when(c == 0)
    def _():
        st_ref[...] = s0_ref[0]

    q = x_ref.shape[0]
    hpg = dt_ref.shape[3]
    pw = 2 * SSD_HEADDIM
    dt = dt_ref[0, 0]
    dta = dt * (-jnp.exp(al_ref[0, 0]))
    dtat = dtt_ref[0, 0] * (-jnp.exp(alt_ref[0, 0]))
    row = lax.broadcasted_iota(jnp.int32, (q, q), 0)
    col = lax.broadcasted_iota(jnp.int32, (q, q), 1)
    mask = (row <= col) if reverse else (row >= col)
    mask_t = (row >= col) if reverse else (row <= col)
    cum = jnp.dot(mask.astype(F32), dta, preferred_element_type=F32, precision=HI)
    cumt = jnp.dot(dtat, mask_t.astype(F32), preferred_element_type=F32, precision=HI)
    last = jnp.sum(dta, axis=0, keepdims=True)
    ecum = jnp.exp(cum)
    wdec = jnp.exp(last - cum)
    elast = jnp.exp(last)
    bmat = b_ref[...]
    cmat = c_ref[...].astype(BF16)
    cb = lax.dot_general(cmat, bmat.astype(BF16), (((1,), (1,)), ((), ())), preferred_element_type=F32)
    bt = bmat.T.astype(BF16)
    lo = lax.broadcasted_iota(jnp.int32, (1, pw), 1) < SSD_HEADDIM

    for j in range(hpg // 2):
        h0, h1 = 2 * j, 2 * j + 1

        def lanevec(a):
            return jnp.where(lo, a[:, h0:h0 + 1], a[:, h1:h1 + 1])

        def mix(h):
            seg = cum[:, h:h + 1] - cumt[h:h + 1, :]
            return (cb * jnp.where(mask, jnp.exp(seg), 0.0)).astype(BF16)

        xs = x_ref[:, j * pw:(j + 1) * pw]
        xdt = xs * lanevec(dt)
        xdt_bf = xdt.astype(BF16)
        y = jnp.where(lo,
                      jnp.dot(mix(h0), xdt_bf, preferred_element_type=F32),
                      jnp.dot(mix(h1), xdt_bf, preferred_element_type=F32))
        st = st_ref[j]
        y = y + jnp.dot(cmat, st.astype(BF16), preferred_element_type=F32) * lanevec(ecum)
        st_ref[j] = st * lanevec(elast) + jnp.dot(bt, (xdt * lanevec(wdec)).astype(BF16),
                                                  preferred_element_type=F32)
        if reverse:
            y = y + yprev_ref[:, j * pw:(j + 1) * pw] + xs * lanevec(d_ref[0])
        y_ref[:, j * pw:(j + 1) * pw] = y

    @pl.when(c == pl.num_programs(1) - 1)
    def _():
        sfin_ref[0] = st_ref[...]


def _ssd_scan(xbc, dt_a, dt_b, al_a, al_b, s0, direction, yprev=None, d=None):
    length = xbc.shape[0]
    groups, hpg = dt_a.shape[1], dt_a.shape[3]
    q = SSD_CHUNK
    n = SSD_STATE
    gw = hpg * SSD_HEADDIM
    d_inner = groups * gw
    nc = length // q
    reverse = direction == 1
    boff = d_inner // n

    def cc(c):
        return nc - 1 - c if reverse else c

    in_specs = [pl.BlockSpec((q, gw), lambda g, c: (cc(c), g)),
                pl.BlockSpec((q, n), lambda g, c: (cc(c), boff + g)),
                pl.BlockSpec((q, n), lambda g, c: (cc(c), boff + groups + g)),
                pl.BlockSpec((1, 1, q, hpg), lambda g, c: (direction, g, cc(c), 0)),
                pl.BlockSpec((1, 1, hpg, q), lambda g, c: (direction, g, 0, cc(c))),
                pl.BlockSpec((1, 1, 1, hpg), lambda g, c: (direction, g, 0, 0)),
                pl.BlockSpec((1, 1, hpg, 1), lambda g, c: (direction, g, 0, 0)),
                pl.BlockSpec((1, hpg // 2, n, 2 * SSD_HEADDIM), lambda g, c: (g, 0, 0, 0))]
    args = [xbc, xbc, xbc, dt_a, dt_b, al_a, al_b, s0]
    if reverse:
        in_specs += [pl.BlockSpec((q, gw), lambda g, c: (cc(c), g)),
                     pl.BlockSpec((1, 1, hpg), lambda g, c: (g, 0, 0))]
        args += [yprev, d]
    return pl.pallas_call(
        functools.partial(_ssd_kernel, reverse=reverse),
        grid=(groups, nc),
        in_specs=in_specs,
        out_specs=[pl.BlockSpec((q, gw), lambda g, c: (cc(c), g)),
                   pl.BlockSpec((1, hpg // 2, n, 2 * SSD_HEADDIM), lambda g, c: (g, 0, 0, 0))],
        out_shape=[jax.ShapeDtypeStruct((length, d_inner), F32),
                   jax.ShapeDtypeStruct(s0.shape, F32)],
        scratch_shapes=[pltpu.VMEM((hpg // 2, n, 2 * SSD_HEADDIM), F32)],
        compiler_params=_cp("arbitrary", "arbitrary"),
        name="ssd_scan_bwd" if reverse else "ssd_scan_fwd",
    )(*args)


def _ssd_inputs(h, w_bf, conv_w, conv_b, dt_bias, d_inner, conv_dim, heads2, name):
    length = h.shape[0]
    xbc_raw = _up(h, w_bf, conv_dim, [d_inner], None, _epi_id, F32, name + "_xbc")
    dt = _up(h, w_bf[:, d_inner + conv_dim:], heads2, [0], dt_bias.reshape(-1), _epi_softplus, F32,
             name + "_dt", tn_pref=heads2)
    xbc = _dwconv(xbc_raw, conv_w, conv_b, seg=length, tile_rows=_tile(length, 128, SUBLANE),
                  tile_lanes=conv_dim, pad=SUBLANE, halo=True, act="silu", name=name + "_conv")
    hpg = heads2 // 2 // SSD_GROUPS
    dt4 = dt.reshape(length, 2, SSD_GROUPS, hpg)
    return xbc, jnp.transpose(dt4, (1, 2, 0, 3)), jnp.transpose(dt4, (1, 2, 3, 0))


def kernel(x, c, ctx, c_ctx, norm_mix_g, norm_ffn_g, mod_wa, mod_wb, mod_b, conv_w1, conv_b1, conv_dw, conv_dw_b, conv_ln_g, conv_ln_b, conv_w2, conv_b2, ssd_w_in, ssd_conv_w, ssd_conv_b, ssd_a_log, ssd_dt_bias, ssd_d, ssd_norm_g, ssd_w_out, sgu_w1, sgu_b1, sgu_ln_g, sgu_ln_b, sgu_ws, sgu_bs, sgu_w2, sgu_b2, ffn_w_in, ffn_dw, ffn_dw_b, ffn_w_out, final_g):
    bsz, seq, d = x.shape
    assert bsz == 1
    depth = norm_mix_g.shape[0]
    xl = x.reshape(seq, d)
    xc = ctx.reshape(ctx.shape[1], d)
    grid_rows = seq // GRID_W
    f_hidden = ffn_w_out.shape[1]

    v8 = jnp.zeros((SUBLANE, d), F32).at[0].set(c[0]).at[1].set(c_ctx)
    mod = _adaln(v8, mod_wa, mod_wb, mod_b)

    def mods(layer, row):
        return [mod[layer, row, t * d:(t + 1) * d] for t in range(6)]

    ssd_layers = [i for i in range(depth) if i % DEPTH_MIXERS == 1]
    last_ctx = ssd_layers[-1] if ssd_layers else -1
    zeros_d = jnp.zeros((d,), F32)

    def next_norm(layer, row):
        if layer + 1 < depth:
            sh1, sc1 = mods(layer + 1, row)[:2]
            return norm_mix_g[layer + 1], sc1, sh1, BF16
        return final_g, zeros_d, zeros_d, F32

    sh1, sc1 = mods(0, 0)[:2]
    hl = _prenorm(xl, norm_mix_g[0], sc1, sh1)
    hc = None
    if last_ctx >= 0:
        csh1, csc1 = mods(0, 1)[:2]
        hc = _prenorm(xc, norm_mix_g[0], csc1, csh1)

    out = None
    for i in range(depth):
        kind, k = i % DEPTH_MIXERS, i // DEPTH_MIXERS
        ctx_live = i <= last_ctx
        ctx_full = i < last_ctx
        _, _, g1, sh2, sc2, g2 = mods(i, 0)
        if ctx_live:
            _, _, cg1, csh2, csc2, cg2 = mods(i, 1)
        tag = "l%d" % i

        if kind == 0:
            w1 = conv_w1[k].astype(BF16)
            w2 = conv_w2[k].astype(BF16)

            def glu_up(h, name):
                return _up(h, w1, d, [0, d], conv_b1[k], _epi_glu, F32, name)

            gl = glu_up(hl, tag + "_conv_up")
            if k % 2 == 0:
                gl = _dwconv(gl, conv_dw[k], conv_dw_b[k], seg=GRID_W, tile_rows=2 * GRID_W, tile_lanes=d,
                             pad=2 * SUBLANE, halo=False, act=None, name=tag + "_rowconv")
            else:
                gl = _dwconv(gl.reshape(grid_rows, GRID_W * d), conv_dw[k], conv_dw_b[k], seg=grid_rows,
                             tile_rows=grid_rows, tile_lanes=1024, pad=2 * SUBLANE, halo=False, act=None,
                             name=tag + "_colconv").reshape(seq, d)
            al = _ln_silu(gl, conv_ln_g[k], conv_ln_b[k])
            xl, hl = _down_direct(al, w2, conv_b2[k], xl, g1, norm_ffn_g[i], sc2, sh2, BF16, tag + "_conv_down")
            if ctx_full:
                gc = glu_up(hc, tag + "_conv_up_ctx")
                gc = _dwconv(gc, conv_dw[k], conv_dw_b[k], seg=gc.shape[0], tile_rows=gc.shape[0],
                             tile_lanes=1024, pad=2 * SUBLANE, halo=False, act=None, name=tag + "_seqconv_ctx")
                ac = _ln_silu(gc, conv_ln_g[k], conv_ln_b[k])
                xc, hc = _down_direct(ac, w2, conv_b2[k], xc, cg1, norm_ffn_g[i], csc2, csh2, BF16,
                                      tag + "_conv_down_ctx")
        elif kind == 1:
            w_in = ssd_w_in[k].astype(BF16)
            w_out = ssd_w_out[k].astype(BF16)
            d_inner = w_out.shape[0]
            heads2 = 2 * ssd_a_log.shape[2]
            conv_dim = w_in.shape[1] - d_inner - heads2
            hpg = heads2 // 2 // SSD_GROUPS
            al_a = ssd_a_log[k].reshape(2, SSD_GROUPS, 1, hpg)
            al_b = ssd_a_log[k].reshape(2, SSD_GROUPS, hpg, 1)
            dskip = ssd_d[k].reshape(SSD_GROUPS, 1, hpg)
            state = [jnp.zeros((SSD_GROUPS, hpg // 2, SSD_STATE, 2 * SSD_HEADDIM), F32)] * 2
            xbc_c, dta_c, dtb_c = _ssd_inputs(hc, w_in, ssd_conv_w[k], ssd_conv_b[k], ssd_dt_bias[k],
                                              d_inner, conv_dim, heads2, tag + "_ssd_ctx")
            yc, state[0] = _ssd_scan(xbc_c, dta_c, dtb_c, al_a, al_b, state[0], 0)
            yc, state[1] = _ssd_scan(xbc_c, dta_c, dtb_c, al_a, al_b, state[1], 1, yc, dskip)
            xbc_l, dta_l, dtb_l = _ssd_inputs(hl, w_in, ssd_conv_w[k], ssd_conv_b[k], ssd_dt_bias[k],
                                              d_inner, conv_dim, heads2, tag + "_ssd")
            z = _up(hl, w_in, d_inner, [0], None, _epi_id, F32, tag + "_ssd_z")
            yl, _ = _ssd_scan(xbc_l, dta_l, dtb_l, al_a, al_b, state[0], 0)
            yl, _ = _ssd_scan(xbc_l, dta_l, dtb_l, al_a, al_b, state[1], 1, yl, dskip)
            xl, hl = _down_ssd(yl, z, ssd_norm_g[k], w_out, xl, g1, norm_ffn_g[i], sc2, sh2, BF16,
                               tag + "_ssd_down")
            if ctx_full:
                zc = _up(hc, w_in, d_inner, [0], None, _epi_id, F32, tag + "_ssd_z_ctx")
                xc, hc = _down_ssd(yc, zc, ssd_norm_g[k], w_out, xc, cg1, norm_ffn_g[i], csc2, csh2, BF16,
                                   tag + "_ssd_down_ctx")
        else:
            w1 = sgu_w1[k].astype(BF16)
            w2 = sgu_w2[k].astype(BF16)

            def sgu(h, name):
                a = _up(h, w1, w1.shape[1], [0], sgu_b1[k], _epi_gelu, F32, name + "_up")
                return _sgu_mix(a, sgu_ln_g[k], sgu_ln_b[k], sgu_ws[k], sgu_bs[k])

            xl, hl = _down_direct(sgu(hl, tag + "_sgu"), w2, sgu_b2[k], xl, g1, norm_ffn_g[i], sc2, sh2, BF16,
                                  tag + "_sgu_down")
            if ctx_full:
                xc, hc = _down_direct(sgu(hc, tag + "_sgu_ctx"), w2, sgu_b2[k], xc, cg1, norm_ffn_g[i],
                                      csc2, csh2, BF16, tag + "_sgu_down_ctx")

        fw_in = ffn_w_in[i].astype(BF16)
        fw_out = ffn_w_out[i].astype(BF16)
        ng, nsc, nsh, ndt = next_norm(i, 0)
        a = _up(hl, fw_in, 2 * f_hidden, [0], None, _epi_id, F32, tag + "_ffn_up")
        xl, hl = _down_ffn(a, ffn_dw[i], ffn_dw_b[i], fw_out, xl, g2, ng, nsc, nsh, True, ndt, tag + "_ffn_down")
        out = hl
        if ctx_full:
            ng, nsc, nsh, ndt = next_norm(i, 1)
            ac = _up(hc, fw_in, 2 * f_hidden, [0], None, _epi_id, F32, tag + "_ffn_up_ctx")
            xc, hc = _down_ffn(ac, ffn_dw[i], ffn_dw_b[i], fw_out, xc, cg2, ng, nsc, nsh, False, ndt,
                               tag + "_ffn_down_ctx")
    return out.reshape(bsz, seq, d)
```

```python
import functools

import jax
import jax.numpy as jnp
from jax import lax
from jax.experimental import pallas as pl
from jax.experimental.pallas import tpu as pltpu

F32 = jnp.float32
BF16 = jnp.bfloat16
HI = lax.Precision.HIGHEST

NORM_EPS = 1e-6
GRID_W = 64
DEPTH_MIXERS = 3
SSD_GROUPS = 8
SSD_HEADDIM = 64
SSD_STATE = 128
SSD_CHUNK = 128
SGU_CHUNK = 128
SGU_GROUPS = 16
LANE = 128
SUBLANE = 8
CONV_LANES = 256
CONV_ROWS = 64
ROW_CHUNK = 64
VMEM_LIMIT = 58 * 1024 * 1024


def _cp(*sem):
    return pltpu.CompilerParams(dimension_semantics=sem, vmem_limit_bytes=VMEM_LIMIT)


def _tile(n, pref, align):
    if n <= pref:
        return n
    t = (pref // align) * align
    while t >= align:
        if n % t == 0:
            return t
        t -= align
    return n


def _row(v):
    return v.reshape(1, -1)


def _prenorm_math(x, g, sc, sh):
    ms = jnp.mean(x * x, axis=-1, keepdims=True)
    return (x * lax.rsqrt(ms + NORM_EPS)) * g * (1.0 + sc) + sh


def _silu(x):
    return x * jax.nn.sigmoid(x)


def _row_loop(nrows, fn):
    step = min(nrows, ROW_CHUNK)

    def body(r, carry):
        fn(pl.ds(pl.multiple_of(r * step, step), step))
        return carry

    lax.fori_loop(0, nrows // step, body, 0)


def _adaln_kernel(v_ref, wa_ref, wb_ref, b_ref, o_ref, t_ref):
    @pl.when(pl.program_id(1) == 0)
    def _():
        t_ref[...] = jnp.dot(_silu(v_ref[...]), wa_ref[0], preferred_element_type=F32, precision=HI)

    o_ref[0] = jnp.dot(t_ref[...], wb_ref[0], preferred_element_type=F32, precision=HI) + b_ref[0]


def _adaln(v8, wa, wb, b):
    depth, d, r = wa.shape
    n = wb.shape[2]
    tn = _tile(n, 6144, LANE)
    return pl.pallas_call(
        _adaln_kernel,
        grid=(depth, n // tn),
        in_specs=[pl.BlockSpec((SUBLANE, d), lambda l, j: (0, 0)),
                  pl.BlockSpec((1, d, r), lambda l, j: (l, 0, 0)),
                  pl.BlockSpec((1, r, tn), lambda l, j: (l, 0, j)),
                  pl.BlockSpec((1, 1, tn), lambda l, j: (l, 0, j))],
        out_specs=pl.BlockSpec((1, SUBLANE, tn), lambda l, j: (l, 0, j)),
        out_shape=jax.ShapeDtypeStruct((depth, SUBLANE, n), F32),
        scratch_shapes=[pltpu.VMEM((SUBLANE, r), F32)],
        compiler_params=_cp("arbitrary", "arbitrary"),
        name="adaln",
    )(v8, wa, wb, b.reshape(depth, 1, n))


def _ln_silu_kernel(x_ref, g_ref, b_ref, o_ref):
    x = x_ref[...]
    mu = jnp.mean(x, axis=-1, keepdims=True)
    xc = x - mu
    var = jnp.mean(xc * xc, axis=-1, keepdims=True)
    y = xc * lax.rsqrt(var + NORM_EPS) * g_ref[...] + b_ref[...]
    o_ref[...] = _silu(y).astype(o_ref.dtype)


def _ln_silu(x, g, b):
    m, d = x.shape
    tm = _tile(m, 256, SUBLANE)
    vec = pl.BlockSpec((1, d), lambda i: (0, 0))
    return pl.pallas_call(
        _ln_silu_kernel,
        grid=(m // tm,),
        in_specs=[pl.BlockSpec((tm, d), lambda i: (i, 0)), vec, vec],
        out_specs=pl.BlockSpec((tm, d), lambda i: (i, 0)),
        out_shape=jax.ShapeDtypeStruct((m, d), BF16),
        compiler_params=_cp("arbitrary"),
        name="ln_silu",
    )(x, _row(g), _row(b))


def _up_kernel(*refs, n_w, has_bias, epi):
    x_ref, ng_ref, sc_ref, sh_ref = refs[:4]
    w_refs = refs[4:4 + n_w]
    b_refs = refs[4 + n_w:4 + 2 * n_w] if has_bias else ()
    o_ref, h_ref = refs[-2:]

    @pl.when(pl.program_id(1) == 0)
    def _():
        def norm_rows(rows):
            h_ref[rows, :] = _prenorm_math(x_ref[rows, :], ng_ref[...], sc_ref[...], sh_ref[...]).astype(BF16)

        _row_loop(x_ref.shape[0], norm_rows)

    h = h_ref[...]
    accs = []
    for t in range(n_w):
        a = jnp.dot(h, w_refs[t][...], preferred_element_type=F32)
        if has_bias:
            a = a + b_refs[t][...]
        accs.append(a)
    o_ref[...] = epi(*accs).astype(o_ref.dtype)


def _epi_id(a):
    return a


def _epi_glu(a, b):
    return a * jax.nn.sigmoid(b)


def _epi_gelu(a):
    return jax.nn.gelu(a)


def _epi_softplus(a):
    return jnp.maximum(a, 0.0) + jnp.log1p(jnp.exp(-jnp.abs(a)))


def _up(x, norm, w, n_out, col_offs, bias, epi, out_dtype, name, tn_pref=512):
    ng, sc, sh = norm
    m, k = x.shape
    tm = _tile(m, 1024, SUBLANE)
    tn = _tile(n_out, tn_pref, LANE)
    for off in col_offs:
        assert off % tn == 0
    n_w = len(col_offs)
    vec = pl.BlockSpec((1, k), lambda i, j: (0, 0))
    in_specs = [pl.BlockSpec((tm, k), lambda i, j: (i, 0), pipeline_mode=pl.Buffered(1)), vec, vec, vec]
    args = [x, _row(ng), _row(sc), _row(sh)]
    for off in col_offs:
        in_specs.append(pl.BlockSpec((k, tn), functools.partial(lambda i, j, o: (0, j + o), o=off // tn)))
        args.append(w)
    if bias is not None:
        for off in col_offs:
            in_specs.append(pl.BlockSpec((1, tn), functools.partial(lambda i, j, o: (0, j + o), o=off // tn)))
            args.append(_row(bias))
    return pl.pallas_call(
        functools.partial(_up_kernel, n_w=n_w, has_bias=bias is not None, epi=epi),
        grid=(m // tm, n_out // tn),
        in_specs=in_specs,
        out_specs=pl.BlockSpec((tm, tn), lambda i, j: (i, j)),
        out_shape=jax.ShapeDtypeStruct((m, n_out), out_dtype),
        scratch_shapes=[pltpu.VMEM((tm, k), BF16)],
        compiler_params=_cp("arbitrary", "arbitrary"),
        name=name,
    )(*args)


def _seqconv_kernel(x_ref, xp_ref, xn_ref, w_ref, b_ref, o_ref, buf_ref, *, ntaps, act):
    t_rows, c = x_ref.shape
    pad = xp_ref.shape[0]
    i = pl.program_id(0)
    buf_ref[pl.ds(0, pad), :] = jnp.where(i > 0, xp_ref[...], 0.0)
    buf_ref[pl.ds(pad, t_rows), :] = x_ref[...]
    buf_ref[pl.ds(pad + t_rows, pad), :] = jnp.where(i < pl.num_programs(0) - 1, xn_ref[...], 0.0)
    half = ntaps // 2
    sub = min(t_rows, CONV_ROWS)
    lw = min(c, CONV_LANES)

    def body(ci, carry):
        lanes = pl.ds(pl.multiple_of(ci * lw, lw), lw)
        bias = b_ref[:, lanes]
        for s in range(t_rows // sub):
            acc = None
            for k in range(ntaps):
                v = buf_ref[pl.ds(pad + s * sub + k - half, sub), lanes] * w_ref[pl.ds(k, 1), lanes]
                acc = v if acc is None else acc + v
            acc = acc + bias
            if act == "silu":
                acc = _silu(acc)
            o_ref[pl.ds(s * sub, sub), lanes] = acc
        return carry

    lax.fori_loop(0, c // lw, body, 0)


def _seqconv(x, w, b, *, tile_rows, act, name):
    rows, c = x.shape
    ntaps = w.shape[0]
    pad = SUBLANE
    assert ntaps // 2 <= pad and rows % tile_rows == 0
    ratio = tile_rows // pad
    nhb = rows // pad
    return pl.pallas_call(
        functools.partial(_seqconv_kernel, ntaps=ntaps, act=act),
        grid=(rows // tile_rows,),
        in_specs=[pl.BlockSpec((tile_rows, c), lambda i: (i, 0)),
                  pl.BlockSpec((pad, c), lambda i: (jnp.maximum(i * ratio - 1, 0), 0)),
                  pl.BlockSpec((pad, c), lambda i: (jnp.minimum((i + 1) * ratio, nhb - 1), 0)),
                  pl.BlockSpec((ntaps, c), lambda i: (0, 0)),
                  pl.BlockSpec((1, c), lambda i: (0, 0))],
        out_specs=pl.BlockSpec((tile_rows, c), lambda i: (i, 0)),
        out_shape=jax.ShapeDtypeStruct((rows, c), F32),
        scratch_shapes=[pltpu.VMEM((tile_rows + 2 * pad, c), F32)],
        compiler_params=_cp("arbitrary"),
        name=name,
    )(x, x, x, w, _row(b))


def _rowconv_kernel(x_ref, w_ref, b_ref, o_ref, pad_ref, ph_ref, *, ntaps, seg):
    t_rows, c = x_ref.shape
    nseg = t_rows // seg
    lead = 2 * SUBLANE
    half = ntaps // 2
    amax = (lead - half + ntaps - 1) // SUBLANE
    span = seg + SUBLANE * amax
    stride = span + SUBLANE
    lw = ph_ref.shape[2]
    sub = min(seg, CONV_ROWS)
    for r in range(nseg):
        pad_ref[pl.ds(r * stride, lead), :] = jnp.zeros((lead, c), F32)
        pad_ref[pl.ds(r * stride + lead, seg), :] = x_ref[pl.ds(r * seg, seg), :]
        pad_ref[pl.ds(r * stride + lead + seg, stride - lead - seg), :] = jnp.zeros((stride - lead - seg, c), F32)

    def body(ci, carry):
        lanes = pl.ds(pl.multiple_of(ci * lw, lw), lw)
        bias = b_ref[:, lanes]
        for r in range(nseg):
            for b in range(1, SUBLANE):
                ph_ref[b] = pad_ref[pl.ds(r * stride + b, span), lanes]
            for s in range(0, seg, sub):
                acc = None
                for k in range(ntaps):
                    off = lead - half + k
                    a, b = off // SUBLANE, off % SUBLANE
                    if b == 0:
                        src = pad_ref[pl.ds(r * stride + SUBLANE * a + s, sub), lanes]
                    else:
                        src = ph_ref[b, pl.ds(SUBLANE * a + s, sub), :]
                    v = src * w_ref[pl.ds(k, 1), lanes]
                    acc = v if acc is None else acc + v
                o_ref[pl.ds(r * seg + s, sub), lanes] = acc + bias
        return carry

    lax.fori_loop(0, c // lw, body, 0)


def _rowconv(x, w, b, *, seg, tile_rows, name):
    rows, c = x.shape
    ntaps = w.shape[0]
    lead = 2 * SUBLANE
    assert ntaps // 2 <= lead and seg % SUBLANE == 0 and tile_rows % seg == 0
    amax = (lead - ntaps // 2 + ntaps - 1) // SUBLANE
    span = seg + SUBLANE * amax
    stride = span + SUBLANE
    lw = min(c, CONV_LANES)
    return pl.pallas_call(
        functools.partial(_rowconv_kernel, ntaps=ntaps, seg=seg),
        grid=(rows // tile_rows,),
        in_specs=[pl.BlockSpec((tile_rows, c), lambda i: (i, 0)),
                  pl.BlockSpec((ntaps, c), lambda i: (0, 0)),
                  pl.BlockSpec((1, c), lambda i: (0, 0))],
        out_specs=pl.BlockSpec((tile_rows, c), lambda i: (i, 0)),
        out_shape=jax.ShapeDtypeStruct((rows, c), F32),
        scratch_shapes=[pltpu.VMEM(((tile_rows // seg) * stride, c), F32),
                        pltpu.VMEM((SUBLANE, span, lw), F32)],
        compiler_params=_cp("arbitrary"),
        name=name,
    )(x, w, _row(b))


def _colconv_kernel(x_ref, w_ref, b_ref, o_ref, buf_ref, *, ntaps):
    rows, width, lw = x_ref.shape
    half = ntaps // 2
    lead = buf_ref.shape[0] - rows - half
    zeros = jnp.zeros((width, lw), F32)
    for r in range(lead):
        buf_ref[r] = zeros
    for r in range(half):
        buf_ref[lead + rows + r] = zeros

    def copy(r, carry):
        buf_ref[lead + r] = x_ref[r]
        return carry

    lax.fori_loop(0, rows, copy, 0)
    def body(r, carry):
        acc = None
        for k in range(ntaps):
            v = buf_ref[lead - half + r + k] * w_ref[pl.ds(k, 1), :]
            acc = v if acc is None else acc + v
        o_ref[r] = acc + b_ref[...]
        return carry

    lax.fori_loop(0, rows, body, 0)


def _colconv(x3, w, b, name):
    rows, width, c = x3.shape
    ntaps = w.shape[0]
    half = ntaps // 2
    lw = min(c, LANE)
    return pl.pallas_call(
        functools.partial(_colconv_kernel, ntaps=ntaps),
        grid=(c // lw,),
        in_specs=[pl.BlockSpec((rows, width, lw), lambda j: (0, 0, j)),
                  pl.BlockSpec((ntaps, lw), lambda j: (0, j)),
                  pl.BlockSpec((1, lw), lambda j: (0, j))],
        out_specs=pl.BlockSpec((rows, width, lw), lambda j: (0, 0, j)),
        out_shape=jax.ShapeDtypeStruct((rows, width, c), F32),
        scratch_shapes=[pltpu.VMEM((rows + 2 * half, width, lw), F32)],
        compiler_params=_cp("arbitrary"),
        name=name,
    )(x3, w, _row(b))


def _down_mm_kernel(*refs, has_bias):
    if has_bias:
        a_ref, w_ref, bias_ref, x_ref, gate_ref, o_ref = refs
    else:
        a_ref, w_ref, x_ref, gate_ref, o_ref = refs
    k = pl.program_id(2)
    part = jnp.dot(a_ref[...], w_ref[...], preferred_element_type=F32)

    @pl.when(k == 0)
    def _():
        o_ref[...] = part

    @pl.when(k > 0)
    def _():
        o_ref[...] += part

    @pl.when(k == pl.num_programs(2) - 1)
    def _():
        y = o_ref[...]
        if has_bias:
            y = y + bias_ref[...]
        o_ref[...] = x_ref[...] + gate_ref[...] * y


def _down_mm(a, w, bias, x, gate, name):
    m, kdim = a.shape
    d = w.shape[1]
    tm = _tile(m, 1024, SUBLANE)
    tn = _tile(d, 1024, LANE)
    tk = _tile(kdim, 2048, LANE)
    vec = pl.BlockSpec((1, tn), lambda i, j, k: (0, j))
    in_specs = [pl.BlockSpec((tm, tk), lambda i, j, k: (i, k)),
                pl.BlockSpec((tk, tn), lambda i, j, k: (k, j))]
    args = [a, w]
    if bias is not None:
        in_specs.append(vec)
        args.append(_row(bias))
    in_specs += [pl.BlockSpec((tm, tn), lambda i, j, k: (i, j)), vec]
    args += [x, _row(gate)]
    return pl.pallas_call(
        functools.partial(_down_mm_kernel, has_bias=bias is not None),
        grid=(m // tm, d // tn, kdim // tk),
        in_specs=in_specs,
        out_specs=pl.BlockSpec((tm, tn), lambda i, j, k: (i, j)),
        out_shape=jax.ShapeDtypeStruct((m, d), F32),
        compiler_params=_cp("arbitrary", "arbitrary", "arbitrary"),
        name=name,
    )(*args)


def _ffn_down_kernel(gc_ref, gp_ref, gn_ref, up_ref, dw_ref, dwb_ref, w_ref, x_ref, gate_ref, ng_ref,
                     o_ref, buf_ref, a_ref, *, on_grid, final):
    tm, tk = gc_ref.shape
    hb = gp_ref.shape[0]
    i = pl.program_id(0)
    k = pl.program_id(1)
    buf_ref[pl.ds(0, hb), :] = jnp.where(i > 0, gp_ref[...], 0.0)
    buf_ref[pl.ds(hb, tm), :] = gc_ref[...]
    buf_ref[pl.ds(hb + tm, hb), :] = jnp.where(i < pl.num_programs(0) - 1, gn_ref[...], 0.0)
    sub = min(tm, CONV_ROWS)
    lw = min(tk, CONV_LANES)
    col = lax.broadcasted_iota(jnp.int32, (sub, lw), 0)
    dys = (-1, 0, 1) if on_grid else (0,)
    for lc in range(tk // lw):
        lanes = pl.ds(lc * lw, lw)
        for s in range(tm // sub):
            base = hb + s * sub

            def tapsum(dx):
                acc = None
                for dy in dys:
                    v = (buf_ref[pl.ds(base + GRID_W * dy + dx, sub), lanes]
                         * dw_ref[pl.ds((dy + 1) * 3 + dx + 1, 1), lanes])
                    acc = v if acc is None else acc + v
                return acc

            left, right = tapsum(-1), tapsum(1)
            if on_grid:
                left = jnp.where(col >= 1, left, 0.0)
                right = jnp.where(col <= GRID_W - 2, right, 0.0)
            conv = tapsum(0) + left + right + dwb_ref[:, lanes]
            a_ref[pl.ds(s * sub, sub), lanes] = (
                jax.nn.gelu(conv) * up_ref[pl.ds(s * sub, sub), lanes]).astype(BF16)

    d = o_ref.shape[1]
    tn = min(d, 1024)
    a = a_ref[...]
    for n0 in range(0, d, tn):
        part = jnp.dot(a, w_ref[:, n0:n0 + tn], preferred_element_type=F32)

        @pl.when(k == 0)
        def _():
            o_ref[:, n0:n0 + tn] = part

        @pl.when(k > 0)
        def _():
            o_ref[:, n0:n0 + tn] += part

    @pl.when(k == pl.num_programs(1) - 1)
    def _():
        def finish(rows):
            xn = x_ref[rows, :] + gate_ref[...] * o_ref[rows, :]
            if final:
                ms = jnp.mean(xn * xn, axis=-1, keepdims=True)
                xn = (xn * lax.rsqrt(ms + NORM_EPS)) * ng_ref[...]
            o_ref[rows, :] = xn

        _row_loop(tm, finish)


def _down_ffn(a, dw, dw_b, w, x, gate, ng, on_grid, final, name):
    m = a.shape[0]
    f, d = w.shape
    tm = _tile(m, 512, 2 * GRID_W)
    tk = _tile(f, 512, LANE)
    hb = 2 * GRID_W
    ratio = tm // hb
    nhb = m // hb
    koff = f // tk
    vec = pl.BlockSpec((1, d), lambda i, k: (0, 0))
    in_specs = [pl.BlockSpec((tm, tk), lambda i, k: (i, k)),
                pl.BlockSpec((hb, tk), lambda i, k: (jnp.maximum(i * ratio - 1, 0), k)),
                pl.BlockSpec((hb, tk), lambda i, k: (jnp.minimum((i + 1) * ratio, nhb - 1), k)),
                pl.BlockSpec((tm, tk), lambda i, k: (i, k + koff)),
                pl.BlockSpec((9, tk), lambda i, k: (0, k)),
                pl.BlockSpec((1, tk), lambda i, k: (0, k)),
                pl.BlockSpec((tk, d), lambda i, k: (k, 0)),
                pl.BlockSpec((tm, d), lambda i, k: (i, 0), pipeline_mode=pl.Buffered(1)),
                vec, vec]
    return pl.pallas_call(
        functools.partial(_ffn_down_kernel, on_grid=on_grid, final=final),
        grid=(m // tm, f // tk),
        in_specs=in_specs,
        out_specs=pl.BlockSpec((tm, d), lambda i, k: (i, 0)),
        out_shape=jax.ShapeDtypeStruct((m, d), F32),
        scratch_shapes=[pltpu.VMEM((tm + 2 * hb, tk), F32), pltpu.VMEM((tm, tk), BF16)],
        compiler_params=_cp("arbitrary", "arbitrary"),
        name=name,
    )(a, a, a, a, dw.reshape(9, f), _row(dw_b), w, x, _row(gate), _row(ng))


def _sgu_kernel(u_ref, v_ref, g_ref, b_ref, ws_ref, bs_ref, o_ref):
    v = v_ref[...]
    mu = jnp.mean(v, axis=-1, keepdims=True)
    vc = v - mu
    var = jnp.mean(vc * vc, axis=-1, keepdims=True)
    vn = (vc * lax.rsqrt(var + NORM_EPS) * g_ref[...] + b_ref[...]).astype(BF16)
    groups = ws_ref.shape[0]
    gw = v.shape[1] // groups
    for g in range(groups):
        mixed = jnp.dot(ws_ref[g], vn[:, g * gw:(g + 1) * gw], preferred_element_type=F32) + bs_ref[g]
        o_ref[:, g * gw:(g + 1) * gw] = (u_ref[:, g * gw:(g + 1) * gw] * mixed).astype(o_ref.dtype)


def _sgu_mix(a, ln_g, ln_b, ws, bs):
    m = a.shape[0]
    e = a.shape[1] // 2
    q = SGU_CHUNK
    groups = ws.shape[0]
    return pl.pallas_call(
        _sgu_kernel,
        grid=(m // q,),
        in_specs=[pl.BlockSpec((q, e), lambda c: (c, 0)),
                  pl.BlockSpec((q, e), lambda c: (c, 1)),
                  pl.BlockSpec((1, e), lambda c: (0, 0)),
                  pl.BlockSpec((1, e), lambda c: (0, 0)),
                  pl.BlockSpec((groups, q, q), lambda c: (0, 0, 0)),
                  pl.BlockSpec((groups, q, 1), lambda c: (0, 0, 0))],
        out_specs=pl.BlockSpec((q, e), lambda c: (c, 0)),
        out_shape=jax.ShapeDtypeStruct((m, e), BF16),
        compiler_params=_cp("arbitrary"),
        name="sgu_mix",
    )(a, a, _row(ln_g), _row(ln_b), ws.astype(BF16), bs.reshape(groups, q, 1))


def _ssd_kernel(*refs, reverse, mode):
    x_ref, b_ref, c_ref, dt_ref, dtt_ref, al_ref, alt_ref, s0_ref = refs[:8]
    if mode == "gated":
        yprev_ref, d_ref, z_ref, ng_ref, y_ref, sfin_ref, st_ref, yacc_ref = refs[8:]
    elif mode == "y":
        y_ref, sfin_ref, st_ref = refs[8:]
    else:
        sfin_ref, st_ref = refs[8:]
    c = pl.program_id(1)

    @pl.when(c == 0)
    def _():
        st_ref[...] = s0_ref[0]

    q = x_ref.shape[0]
    hpg = dt_ref.shape[3]
    pw = 2 * SSD_HEADDIM
    dt = dt_ref[0, 0]
    dta = dt * (-jnp.exp(al_ref[0, 0]))
    row = lax.broadcasted_iota(jnp.int32, (q, q), 0)
    col = lax.broadcasted_iota(jnp.int32, (q, q), 1)
    mask = (row <= col) if reverse else (row >= col)
    cum = jnp.dot(mask.astype(F32), dta, preferred_element_type=F32, precision=HI)
    last = jnp.sum(dta, axis=0, keepdims=True)
    wdec = jnp.exp(last - cum)
    elast = jnp.exp(last)
    bmat = b_ref[...]
    bt = bmat.T.astype(BF16)
    lo = lax.broadcasted_iota(jnp.int32, (1, pw), 1) < SSD_HEADDIM
    if mode != "state":
        dtat = dtt_ref[0, 0] * (-jnp.exp(alt_ref[0, 0]))
        mask_t = (row >= col) if reverse else (row <= col)
        cumt = jnp.dot(dtat, mask_t.astype(F32), preferred_element_type=F32, precision=HI)
        ecum = jnp.exp(cum)
        cmat = c_ref[...].astype(BF16)
        cb = lax.dot_general(cmat, bmat.astype(BF16), (((1,), (1,)), ((), ())), preferred_element_type=F32)

    for j in range(hpg // 2):
        h0, h1 = 2 * j, 2 * j + 1
        cols = slice(j * pw, (j + 1) * pw)

        def lanevec(a):
            return jnp.where(lo, a[:, h0:h0 + 1], a[:, h1:h1 + 1])

        xs = x_ref[:, cols]
        xdt = xs * lanevec(dt)
        st = st_ref[j]
        st_ref[j] = st * lanevec(elast) + jnp.dot(bt, (xdt * lanevec(wdec)).astype(BF16),
                                                  preferred_element_type=F32)
        if mode == "state":
            continue

        def mix(h):
            seg = cum[:, h:h + 1] - cumt[h:h + 1, :]
            return (cb * jnp.where(mask, jnp.exp(seg), 0.0)).astype(BF16)

        xdt_bf = xdt.astype(BF16)
        y = jnp.where(lo,
                      jnp.dot(mix(h0), xdt_bf, preferred_element_type=F32),
                      jnp.dot(mix(h1), xdt_bf, preferred_element_type=F32))
        y = y + jnp.dot(cmat, st.astype(BF16), preferred_element_type=F32) * lanevec(ecum)
        if mode == "gated":
            yacc_ref[:, cols] = y + yprev_ref[:, cols] + xs * lanevec(d_ref[0])
        else:
            y_ref[:, cols] = y

    if mode == "gated":
        yz = yacc_ref[...] * _silu(z_ref[...])
        ms = jnp.mean(yz * yz, axis=-1, keepdims=True)
        y_ref[...] = (yz * lax.rsqrt(ms + NORM_EPS) * ng_ref[...]).astype(y_ref.dtype)

    @pl.when(c == pl.num_programs(1) - 1)
    def _():
        sfin_ref[0] = st_ref[...]


def _ssd_scan(xbc, dt_a, dt_b, al_a, al_b, s0, direction, mode, extra=()):
    length = xbc.shape[0]
    groups, hpg = dt_a.shape[1], dt_a.shape[3]
    q = SSD_CHUNK
    n = SSD_STATE
    pw = 2 * SSD_HEADDIM
    gw = hpg * SSD_HEADDIM
    d_inner = groups * gw
    nc = length // q
    reverse = direction == 1
    boff = d_inner // n

    def cc(c):
        return nc - 1 - c if reverse else c

    xblk = pl.BlockSpec((q, gw), lambda g, c: (cc(c), g))
    stblk = pl.BlockSpec((1, hpg // 2, n, pw), lambda g, c: (g, 0, 0, 0))
    in_specs = [xblk,
                pl.BlockSpec((q, n), lambda g, c: (cc(c), boff + g)),
                pl.BlockSpec((q, n), lambda g, c: (cc(c), boff + groups + g)),
                pl.BlockSpec((1, 1, q, hpg), lambda g, c: (direction, g, cc(c), 0)),
                pl.BlockSpec((1, 1, hpg, q), lambda g, c: (direction, g, 0, cc(c))),
                pl.BlockSpec((1, 1, 1, hpg), lambda g, c: (direction, g, 0, 0)),
                pl.BlockSpec((1, 1, hpg, 1), lambda g, c: (direction, g, 0, 0)),
                stblk]
    args = [xbc, xbc, xbc, dt_a, dt_b, al_a, al_b, s0]
    out_specs, out_shape = [], []
    scratch = [pltpu.VMEM((hpg // 2, n, pw), F32)]
    if mode == "gated":
        yprev, dskip, z, norm_g = extra
        in_specs += [xblk, pl.BlockSpec((1, 1, hpg), lambda g, c: (g, 0, 0)), xblk,
                     pl.BlockSpec((1, gw), lambda g, c: (0, g))]
        args += [yprev, dskip, z, _row(norm_g)]
        scratch.append(pltpu.VMEM((q, gw), F32))
    if mode != "state":
        out_specs.append(xblk)
        out_shape.append(jax.ShapeDtypeStruct((length, d_inner), BF16 if mode == "gated" else F32))
    out_specs.append(stblk)
    out_shape.append(jax.ShapeDtypeStruct(s0.shape, F32))
    return pl.pallas_call(
        functools.partial(_ssd_kernel, reverse=reverse, mode=mode),
        grid=(groups, nc),
        in_specs=in_specs,
        out_specs=out_specs,
        out_shape=out_shape,
        scratch_shapes=scratch,
        compiler_params=_cp("arbitrary", "arbitrary"),
        name="ssd_scan_%s_%s" % ("bwd" if reverse else "fwd", mode),
    )(*args)


def _ssd_inputs(x, norm, w_bf, conv_w, conv_b, dt_bias, d_inner, conv_dim, heads2, name):
    length = x.shape[0]
    xbc_raw = _up(x, norm, w_bf, conv_dim, [d_inner], None, _epi_id, F32, name + "_xbc")
    dt = _up(x, norm, w_bf[:, d_inner + conv_dim:], heads2, [0], dt_bias.reshape(-1), _epi_softplus, F32,
             name + "_dt", tn_pref=heads2)
    xbc = _seqconv(xbc_raw, conv_w, conv_b, tile_rows=_tile(length, 128, SUBLANE), act="silu",
                   name=name + "_conv")
    hpg = heads2 // 2 // SSD_GROUPS
    dt4 = dt.reshape(length, 2, SSD_GROUPS, hpg)
    return xbc, jnp.transpose(dt4, (1, 2, 0, 3)), jnp.transpose(dt4, (1, 2, 3, 0))


def kernel(x, c, ctx, c_ctx, norm_mix_g, norm_ffn_g, mod_wa, mod_wb, mod_b, conv_w1, conv_b1, conv_dw, conv_dw_b, conv_ln_g, conv_ln_b, conv_w2, conv_b2, ssd_w_in, ssd_conv_w, ssd_conv_b, ssd_a_log, ssd_dt_bias, ssd_d, ssd_norm_g, ssd_w_out, sgu_w1, sgu_b1, sgu_ln_g, sgu_ln_b, sgu_ws, sgu_bs, sgu_w2, sgu_b2, ffn_w_in, ffn_dw, ffn_dw_b, ffn_w_out, final_g):
    bsz, seq, d = x.shape
    assert bsz == 1
    depth = norm_mix_g.shape[0]
    xl = x.reshape(seq, d)
    xc = ctx.reshape(ctx.shape[1], d)
    grid_rows = seq // GRID_W
    f_hidden = ffn_w_out.shape[1]

    v8 = jnp.zeros((SUBLANE, d), F32).at[0].set(c[0]).at[1].set(c_ctx)
    mod = _adaln(v8, mod_wa, mod_wb, mod_b)

    def mods(layer, row):
        return [mod[layer, row, t * d:(t + 1) * d] for t in range(6)]

    ssd_layers = [i for i in range(depth) if i % DEPTH_MIXERS == 1]
    last_ctx = ssd_layers[-1] if ssd_layers else -1

    for i in range(depth):
        kind, k = i % DEPTH_MIXERS, i // DEPTH_MIXERS
        ctx_live = i <= last_ctx
        ctx_full = i < last_ctx
        sh1, sc1, g1, sh2, sc2, g2 = mods(i, 0)
        norm1 = (norm_mix_g[i], sc1, sh1)
        norm2 = (norm_ffn_g[i], sc2, sh2)
        if ctx_live:
            csh1, csc1, cg1, csh2, csc2, cg2 = mods(i, 1)
            cnorm1 = (norm_mix_g[i], csc1, csh1)
            cnorm2 = (norm_ffn_g[i], csc2, csh2)
        tag = "l%d" % i

        if kind == 0:
            w1 = conv_w1[k].astype(BF16)
            w2 = conv_w2[k].astype(BF16)
            gl = _up(xl, norm1, w1, d, [0, d], conv_b1[k], _epi_glu, F32, tag + "_conv_up")
            if k % 2 == 0:
                gl = _rowconv(gl, conv_dw[k], conv_dw_b[k], seg=GRID_W, tile_rows=2 * GRID_W, name=tag + "_rowconv")
            else:
                gl = _colconv(gl.reshape(grid_rows, GRID_W, d), conv_dw[k], conv_dw_b[k],
                              tag + "_colconv").reshape(seq, d)
            al = _ln_silu(gl, conv_ln_g[k], conv_ln_b[k])
            xl = _down_mm(al, w2, conv_b2[k], xl, g1, tag + "_conv_down")
            if ctx_full:
                gc = _up(xc, cnorm1, w1, d, [0, d], conv_b1[k], _epi_glu, F32, tag + "_conv_up_ctx")
                gc = _rowconv(gc, conv_dw[k], conv_dw_b[k], seg=gc.shape[0], tile_rows=gc.shape[0],
                              name=tag + "_seqconv_ctx")
                ac = _ln_silu(gc, conv_ln_g[k], conv_ln_b[k])
                xc = _down_mm(ac, w2, conv_b2[k], xc, cg1, tag + "_conv_down_ctx")
        elif kind == 1:
            assert not ctx_full
            w_in = ssd_w_in[k].astype(BF16)
            w_out = ssd_w_out[k].astype(BF16)
            d_inner = w_out.shape[0]
            heads2 = 2 * ssd_a_log.shape[2]
            conv_dim = w_in.shape[1] - d_inner - heads2
            hpg = heads2 // 2 // SSD_GROUPS
            al_a = ssd_a_log[k].reshape(2, SSD_GROUPS, 1, hpg)
            al_b = ssd_a_log[k].reshape(2, SSD_GROUPS, hpg, 1)
            dskip = ssd_d[k].reshape(SSD_GROUPS, 1, hpg)
            zero_state = jnp.zeros((SSD_GROUPS, hpg // 2, SSD_STATE, 2 * SSD_HEADDIM), F32)
            xbc_c, dta_c, dtb_c = _ssd_inputs(xc, cnorm1, w_in, ssd_conv_w[k], ssd_conv_b[k], ssd_dt_bias[k],
                                              d_inner, conv_dim, heads2, tag + "_ssd_ctx")
            (state_f,) = _ssd_scan(xbc_c, dta_c, dtb_c, al_a, al_b, zero_state, 0, "state")
            (state_b,) = _ssd_scan(xbc_c, dta_c, dtb_c, al_a, al_b, zero_state, 1, "state")
            xbc_l, dta_l, dtb_l = _ssd_inputs(xl, norm1, w_in, ssd_conv_w[k], ssd_conv_b[k], ssd_dt_bias[k],
                                              d_inner, conv_dim, heads2, tag + "_ssd")
            z = _up(xl, norm1, w_in, d_inner, [0], None, _epi_id, F32, tag + "_ssd_z")
            y_f, _ = _ssd_scan(xbc_l, dta_l, dtb_l, al_a, al_b, state_f, 0, "y")
            p, _ = _ssd_scan(xbc_l, dta_l, dtb_l, al_a, al_b, state_b, 1, "gated",
                             (y_f, dskip, z, ssd_norm_g[k]))
            xl = _down_mm(p, w_out, None, xl, g1, tag + "_ssd_down")
        else:
            w1 = sgu_w1[k].astype(BF16)
            w2 = sgu_w2[k].astype(BF16)

            def sgu(xin, norm, name):
                a = _up(xin, norm, w1, w1.shape[1], [0], sgu_b1[k], _epi_gelu, F32, name + "_up")
                return _sgu_mix(a, sgu_ln_g[k], sgu_ln_b[k], sgu_ws[k], sgu_bs[k])

            xl = _down_mm(sgu(xl, norm1, tag + "_sgu"), w2, sgu_b2[k], xl, g1, tag + "_sgu_down")
            if ctx_full:
                xc = _down_mm(sgu(xc, cnorm1, tag + "_sgu_ctx"), w2, sgu_b2[k], xc, cg1, tag + "_sgu_down_ctx")

        fw_in = ffn_w_in[i].astype(BF16)
        fw_out = ffn_w_out[i].astype(BF16)
        final = i == depth - 1
        a = _up(xl, norm2, fw_in, 2 * f_hidden, [0], None, _epi_id, F32, tag + "_ffn_up")
        xl = _down_ffn(a, ffn_dw[i], ffn_dw_b[i], fw_out, xl, g2, final_g, True, final, tag + "_ffn_down")
        if ctx_full:
            ac = _up(xc, cnorm2, fw_in, 2 * f_hidden, [0], None, _epi_id, F32, tag + "_ffn_up_ctx")
            xc = _down_ffn(ac, ffn_dw[i], ffn_dw_b[i], fw_out, xc, cg2, final_g, False, False,
                           tag + "_ffn_down_ctx")
    return xl.reshape(bsz, seq, d)
```

```python
import functools

import jax
import jax.numpy as jnp
from jax import lax
from jax.experimental import pallas as pl
from jax.experimental.pallas import tpu as pltpu

F32 = jnp.float32
BF16 = jnp.bfloat16
HI = lax.Precision.HIGHEST

NORM_EPS = 1e-6
GRID_W = 64
DEPTH_MIXERS = 3
SSD_GROUPS = 8
SSD_HEADDIM = 64
SSD_STATE = 128
SSD_CHUNK = 128
SGU_CHUNK = 128
SGU_GROUPS = 16
LANE = 128
SUBLANE = 8
CONV_LANES = 256
CONV_ROWS = 64
ROW_CHUNK = 64
VMEM_LIMIT = 58 * 1024 * 1024


def _cp(*sem):
    return pltpu.CompilerParams(dimension_semantics=sem, vmem_limit_bytes=VMEM_LIMIT)


def _tile(n, pref, align):
    if n <= pref:
        return n
    t = (pref // align) * align
    while t >= align:
        if n % t == 0:
            return t
        t -= align
    return n


def _row(v):
    return v.reshape(1, -1)


def _prenorm_math(x, g, sc, sh):
    ms = jnp.mean(x * x, axis=-1, keepdims=True)
    return (x * lax.rsqrt(ms + NORM_EPS)) * g * (1.0 + sc) + sh


def _silu(x):
    return x * jax.nn.sigmoid(x)


def _row_loop(nrows, fn):
    step = min(nrows, ROW_CHUNK)

    def body(r, carry):
        fn(pl.ds(pl.multiple_of(r * step, step), step))
        return carry

    lax.fori_loop(0, nrows // step, body, 0)


def _adaln_kernel(v_ref, wa_ref, wb_ref, b_ref, o_ref, t_ref):
    @pl.when(pl.program_id(1) == 0)
    def _():
        t_ref[...] = jnp.dot(_silu(v_ref[...]), wa_ref[0], preferred_element_type=F32, precision=HI)

    o_ref[0] = jnp.dot(t_ref[...], wb_ref[0], preferred_element_type=F32, precision=HI) + b_ref[0]


def _adaln(v8, wa, wb, b):
    depth, d, r = wa.shape
    n = wb.shape[2]
    tn = _tile(n, 6144, LANE)
    return pl.pallas_call(
        _adaln_kernel,
        grid=(depth, n // tn),
        in_specs=[pl.BlockSpec((SUBLANE, d), lambda l, j: (0, 0)),
                  pl.BlockSpec((1, d, r), lambda l, j: (l, 0, 0)),
                  pl.BlockSpec((1, r, tn), lambda l, j: (l, 0, j)),
                  pl.BlockSpec((1, 1, tn), lambda l, j: (l, 0, j))],
        out_specs=pl.BlockSpec((1, SUBLANE, tn), lambda l, j: (l, 0, j)),
        out_shape=jax.ShapeDtypeStruct((depth, SUBLANE, n), F32),
        scratch_shapes=[pltpu.VMEM((SUBLANE, r), F32)],
        compiler_params=_cp("arbitrary", "arbitrary"),
        name="adaln",
    )(v8, wa, wb, b.reshape(depth, 1, n))


def _prenorm_kernel(x_ref, g_ref, sc_ref, sh_ref, o_ref):
    o_ref[...] = _prenorm_math(x_ref[...], g_ref[...], sc_ref[...], sh_ref[...]).astype(o_ref.dtype)


def _prenorm(x, norm):
    g, sc, sh = norm
    m, d = x.shape
    tm = _tile(m, 256, SUBLANE)
    vec = pl.BlockSpec((1, d), lambda i: (0, 0))
    return pl.pallas_call(
        _prenorm_kernel,
        grid=(m // tm,),
        in_specs=[pl.BlockSpec((tm, d), lambda i: (i, 0)), vec, vec, vec],
        out_specs=pl.BlockSpec((tm, d), lambda i: (i, 0)),
        out_shape=jax.ShapeDtypeStruct((m, d), BF16),
        compiler_params=_cp("arbitrary"),
        name="prenorm",
    )(x, _row(g), _row(sc), _row(sh))


def _ln_silu_kernel(x_ref, g_ref, b_ref, o_ref):
    x = x_ref[...]
    mu = jnp.mean(x, axis=-1, keepdims=True)
    xc = x - mu
    var = jnp.mean(xc * xc, axis=-1, keepdims=True)
    y = xc * lax.rsqrt(var + NORM_EPS) * g_ref[...] + b_ref[...]
    o_ref[...] = _silu(y).astype(o_ref.dtype)


def _ln_silu(x, g, b):
    m, d = x.shape
    tm = _tile(m, 256, SUBLANE)
    vec = pl.BlockSpec((1, d), lambda i: (0, 0))
    return pl.pallas_call(
        _ln_silu_kernel,
        grid=(m // tm,),
        in_specs=[pl.BlockSpec((tm, d), lambda i: (i, 0)), vec, vec],
        out_specs=pl.BlockSpec((tm, d), lambda i: (i, 0)),
        out_shape=jax.ShapeDtypeStruct((m, d), BF16),
        compiler_params=_cp("arbitrary"),
        name="ln_silu",
    )(x, _row(g), _row(b))


def _up_kernel(*refs, n_w, has_bias, epi, sub):
    h_ref = refs[0]
    w_refs = refs[1:1 + n_w]
    b_refs = refs[1 + n_w:1 + 2 * n_w] if has_bias else ()
    o_ref = refs[-1]
    h = h_ref[...]
    for n0 in range(0, o_ref.shape[1], sub):
        accs = []
        for t in range(n_w):
            a = jnp.dot(h, w_refs[t][:, n0:n0 + sub], preferred_element_type=F32)
            if has_bias:
                a = a + b_refs[t][:, n0:n0 + sub]
            accs.append(a)
        o_ref[:, n0:n0 + sub] = epi(*accs).astype(o_ref.dtype)


def _epi_id(a):
    return a


def _epi_glu(a, b):
    return a * jax.nn.sigmoid(b)


def _epi_gelu(a):
    return jax.nn.gelu(a)


def _epi_softplus(a):
    return jnp.maximum(a, 0.0) + jnp.log1p(jnp.exp(-jnp.abs(a)))


def _up(h, w, n_out, col_offs, bias, epi, out_dtype, name, tn_pref=1024):
    m, k = h.shape
    tm = _tile(m, 1024, SUBLANE)
    tn = _tile(n_out, tn_pref // len(col_offs), LANE)
    for off in col_offs:
        assert off % tn == 0
    n_w = len(col_offs)
    in_specs = [pl.BlockSpec((tm, k), lambda i, j: (i, 0))]
    args = [h]
    for off in col_offs:
        in_specs.append(pl.BlockSpec((k, tn), functools.partial(lambda i, j, o: (0, j + o), o=off // tn)))
        args.append(w)
    if bias is not None:
        for off in col_offs:
            in_specs.append(pl.BlockSpec((1, tn), functools.partial(lambda i, j, o: (0, j + o), o=off // tn)))
            args.append(_row(bias))
    return pl.pallas_call(
        functools.partial(_up_kernel, n_w=n_w, has_bias=bias is not None, epi=epi,
                          sub=256 if tn % 256 == 0 else LANE),
        grid=(m // tm, n_out // tn),
        in_specs=in_specs,
        out_specs=pl.BlockSpec((tm, tn), lambda i, j: (i, j)),
        out_shape=jax.ShapeDtypeStruct((m, n_out), out_dtype),
        compiler_params=_cp("arbitrary", "arbitrary"),
        name=name,
    )(*args)


def _seqconv_kernel(x_ref, xp_ref, xn_ref, w_ref, b_ref, o_ref, buf_ref, *, ntaps, act):
    t_rows, c = x_ref.shape
    pad = xp_ref.shape[0]
    i = pl.program_id(0)
    buf_ref[pl.ds(0, pad), :] = jnp.where(i > 0, xp_ref[...], 0.0)
    buf_ref[pl.ds(pad, t_rows), :] = x_ref[...]
    buf_ref[pl.ds(pad + t_rows, pad), :] = jnp.where(i < pl.num_programs(0) - 1, xn_ref[...], 0.0)
    half = ntaps // 2
    sub = min(t_rows, CONV_ROWS)
    lw = min(c, CONV_LANES)

    def body(ci, carry):
        lanes = pl.ds(pl.multiple_of(ci * lw, lw), lw)
        bias = b_ref[:, lanes]
        for s in range(t_rows // sub):
            acc = None
            for k in range(ntaps):
                v = buf_ref[pl.ds(pad + s * sub + k - half, sub), lanes] * w_ref[pl.ds(k, 1), lanes]
                acc = v if acc is None else acc + v
            acc = acc + bias
            if act == "silu":
                acc = _silu(acc)
            o_ref[pl.ds(s * sub, sub), lanes] = acc
        return carry

    lax.fori_loop(0, c // lw, body, 0)


def _seqconv(x, w, b, *, tile_rows, act, name):
    rows, c = x.shape
    ntaps = w.shape[0]
    pad = SUBLANE
    assert ntaps // 2 <= pad and rows % tile_rows == 0
    ratio = tile_rows // pad
    nhb = rows // pad
    return pl.pallas_call(
        functools.partial(_seqconv_kernel, ntaps=ntaps, act=act),
        grid=(rows // tile_rows,),
        in_specs=[pl.BlockSpec((tile_rows, c), lambda i: (i, 0)),
                  pl.BlockSpec((pad, c), lambda i: (jnp.maximum(i * ratio - 1, 0), 0)),
                  pl.BlockSpec((pad, c), lambda i: (jnp.minimum((i + 1) * ratio, nhb - 1), 0)),
                  pl.BlockSpec((ntaps, c), lambda i: (0, 0)),
                  pl.BlockSpec((1, c), lambda i: (0, 0))],
        out_specs=pl.BlockSpec((tile_rows, c), lambda i: (i, 0)),
        out_shape=jax.ShapeDtypeStruct((rows, c), F32),
        scratch_shapes=[pltpu.VMEM((tile_rows + 2 * pad, c), F32)],
        compiler_params=_cp("arbitrary"),
        name=name,
    )(x, x, x, w, _row(b))


def _rowconv_kernel(x_ref, w_ref, b_ref, o_ref, pad_ref, ph_ref, *, ntaps, seg):
    t_rows, c = x_ref.shape
    nseg = t_rows // seg
    lead = 2 * SUBLANE
    half = ntaps // 2
    amax = (lead - half + ntaps - 1) // SUBLANE
    span = seg + SUBLANE * amax
    stride = span + SUBLANE
    lw = ph_ref.shape[2]
    sub = min(seg, CONV_ROWS)
    for r in range(nseg):
        pad_ref[pl.ds(r * stride, lead), :] = jnp.zeros((lead, c), F32)
        pad_ref[pl.ds(r * stride + lead, seg), :] = x_ref[pl.ds(r * seg, seg), :]
        pad_ref[pl.ds(r * stride + lead + seg, stride - lead - seg), :] = jnp.zeros((stride - lead - seg, c), F32)

    def body(ci, carry):
        lanes = pl.ds(pl.multiple_of(ci * lw, lw), lw)
        bias = b_ref[:, lanes]
        for r in range(nseg):
            for b in range(1, SUBLANE):
                ph_ref[b] = pad_ref[pl.ds(r * stride + b, span), lanes]
            for s in range(0, seg, sub):
                acc = None
                for k in range(ntaps):
                    off = lead - half + k
                    a, b = off // SUBLANE, off % SUBLANE
                    if b == 0:
                        src = pad_ref[pl.ds(r * stride + SUBLANE * a + s, sub), lanes]
                    else:
                        src = ph_ref[b, pl.ds(SUBLANE * a + s, sub), :]
                    v = src * w_ref[pl.ds(k, 1), lanes]
                    acc = v if acc is None else acc + v
                o_ref[pl.ds(r * seg + s, sub), lanes] = acc + bias
        return carry

    lax.fori_loop(0, c // lw, body, 0)


def _rowconv(x, w, b, *, seg, tile_rows, name):
    rows, c = x.shape
    ntaps = w.shape[0]
    lead = 2 * SUBLANE
    assert ntaps // 2 <= lead and seg % SUBLANE == 0 and tile_rows % seg == 0
    amax = (lead - ntaps // 2 + ntaps - 1) // SUBLANE
    span = seg + SUBLANE * amax
    stride = span + SUBLANE
    lw = min(c, CONV_LANES)
    return pl.pallas_call(
        functools.partial(_rowconv_kernel, ntaps=ntaps, seg=seg),
        grid=(rows // tile_rows,),
        in_specs=[pl.BlockSpec((tile_rows, c), lambda i: (i, 0)),
                  pl.BlockSpec((ntaps, c), lambda i: (0, 0)),
                  pl.BlockSpec((1, c), lambda i: (0, 0))],
        out_specs=pl.BlockSpec((tile_rows, c), lambda i: (i, 0)),
        out_shape=jax.ShapeDtypeStruct((rows, c), F32),
        scratch_shapes=[pltpu.VMEM(((tile_rows // seg) * stride, c), F32),
                        pltpu.VMEM((SUBLANE, span, lw), F32)],
        compiler_params=_cp("arbitrary"),
        name=name,
    )(x, w, _row(b))


def _colconv_kernel(x_ref, w_ref, b_ref, o_ref, buf_ref, *, ntaps):
    rows, width, lw = x_ref.shape
    half = ntaps // 2
    lead = buf_ref.shape[0] - rows - half
    zeros = jnp.zeros((width, lw), F32)
    for r in range(lead):
        buf_ref[r] = zeros
    for r in range(half):
        buf_ref[lead + rows + r] = zeros

    def copy(r, carry):
        buf_ref[lead + r] = x_ref[r]
        return carry

    lax.fori_loop(0, rows, copy, 0)

    def body(r, carry):
        acc = None
        for k in range(ntaps):
            v = buf_ref[lead - half + r + k] * w_ref[pl.ds(k, 1), :]
            acc = v if acc is None else acc + v
        o_ref[r] = acc + b_ref[...]
        return carry

    lax.fori_loop(0, rows, body, 0)


def _colconv(x3, w, b, name):
    rows, width, c = x3.shape
    ntaps = w.shape[0]
    half = ntaps // 2
    lw = min(c, LANE)
    return pl.pallas_call(
        functools.partial(_colconv_kernel, ntaps=ntaps),
        grid=(c // lw,),
        in_specs=[pl.BlockSpec((rows, width, lw), lambda j: (0, 0, j)),
                  pl.BlockSpec((ntaps, lw), lambda j: (0, j)),
                  pl.BlockSpec((1, lw), lambda j: (0, j))],
        out_specs=pl.BlockSpec((rows, width, lw), lambda j: (0, 0, j)),
        out_shape=jax.ShapeDtypeStruct((rows, width, c), F32),
        scratch_shapes=[pltpu.VMEM((rows + 2 * half, width, lw), F32)],
        compiler_params=_cp("arbitrary"),
        name=name,
    )(x3, w, _row(b))


def _down_mm_kernel(*refs, has_bias):
    if has_bias:
        a_ref, w_ref, bias_ref, x_ref, gate_ref, o_ref = refs
    else:
        a_ref, w_ref, x_ref, gate_ref, o_ref = refs
    k = pl.program_id(2)
    part = jnp.dot(a_ref[...], w_ref[...], preferred_element_type=F32)

    @pl.when(k == 0)
    def _():
        o_ref[...] = part

    @pl.when(k > 0)
    def _():
        o_ref[...] += part

    @pl.when(k == pl.num_programs(2) - 1)
    def _():
        y = o_ref[...]
        if has_bias:
            y = y + bias_ref[...]
        o_ref[...] = x_ref[...] + gate_ref[...] * y


def _down_mm(a, w, bias, x, gate, name):
    m, kdim = a.shape
    d = w.shape[1]
    tm = _tile(m, 1024, SUBLANE)
    tn = _tile(d, 1024, LANE)
    tk = _tile(kdim, 2048, LANE)
    vec = pl.BlockSpec((1, tn), lambda i, j, k: (0, j))
    in_specs = [pl.BlockSpec((tm, tk), lambda i, j, k: (i, k)),
                pl.BlockSpec((tk, tn), lambda i, j, k: (k, j))]
    args = [a, w]
    if bias is not None:
        in_specs.append(vec)
        args.append(_row(bias))
    in_specs += [pl.BlockSpec((tm, tn), lambda i, j, k: (i, j)), vec]
    args += [x, _row(gate)]
    return pl.pallas_call(
        functools.partial(_down_mm_kernel, has_bias=bias is not None),
        grid=(m // tm, d // tn, kdim // tk),
        in_specs=in_specs,
        out_specs=pl.BlockSpec((tm, tn), lambda i, j, k: (i, j)),
        out_shape=jax.ShapeDtypeStruct((m, d), F32),
        compiler_params=_cp("arbitrary", "arbitrary", "arbitrary"),
        name=name,
    )(*args)


def _ffn_down_kernel(gc_ref, gp_ref, gn_ref, up_ref, dw_ref, dwb_ref, w_ref, x_ref, gate_ref, ng_ref, sc_ref,
                     sh_ref, *rest, on_grid, final, n_i, nk):
    if final:
        o_ref, buf_ref, a0_ref, a1_ref = rest
        h_ref = None
    else:
        o_ref, h_ref, buf_ref, a0_ref, a1_ref = rest
    tm, tk = gc_ref.shape
    hb = gp_ref.shape[0]
    d = o_ref.shape[1]
    s = pl.program_id(0)
    ip = jnp.minimum(s, n_i * nk - 1) // nk
    m = jnp.maximum(s - 1, 0)
    km = lax.rem(m, nk)

    @pl.when(s == 0)
    def _():
        a1_ref[...] = jnp.zeros_like(a1_ref)

    @pl.when(km == 0)
    def _():
        def zero(rows):
            o_ref[rows, :] = jnp.zeros((min(tm, ROW_CHUNK), d), F32)

        _row_loop(tm, zero)

    sub = min(tm, CONV_ROWS)
    lw = min(tk, CONV_LANES)
    dys = (-1, 0, 1) if on_grid else (0,)
    tn = min(d, 256)

    def step(dst_ref, src_ref):
        buf_ref[pl.ds(0, hb), :] = jnp.where(ip > 0, gp_ref[...], 0.0)
        buf_ref[pl.ds(hb, tm), :] = gc_ref[...]
        buf_ref[pl.ds(hb + tm, hb), :] = jnp.where(ip < n_i - 1, gn_ref[...], 0.0)
        col = lax.broadcasted_iota(jnp.int32, (sub, lw), 0)

        def build(lc, r):
            lanes = pl.ds(lc * lw, lw)
            base = hb + r * sub

            def tapsum(dx):
                acc = None
                for dy in dys:
                    v = (buf_ref[pl.ds(base + GRID_W * dy + dx, sub), lanes]
                         * dw_ref[pl.ds((dy + 1) * 3 + dx + 1, 1), lanes])
                    acc = v if acc is None else acc + v
                return acc

            left, right = tapsum(-1), tapsum(1)
            if on_grid:
                left = jnp.where(col >= 1, left, 0.0)
                right = jnp.where(col <= GRID_W - 2, right, 0.0)
            conv = tapsum(0) + left + right + dwb_ref[:, lanes]
            dst_ref[pl.ds(r * sub, sub), lanes] = (
                jax.nn.gelu(conv) * up_ref[pl.ds(r * sub, sub), lanes]).astype(BF16)

        blocks = [(lc, r) for lc in range(tk // lw) for r in range(tm // sub)]
        n_mm = d // tn
        per = -(-len(blocks) // n_mm)
        a = src_ref[...]
        for c in range(n_mm):
            n0 = c * tn
            o_ref[:, n0:n0 + tn] += jnp.dot(a, w_ref[:, n0:n0 + tn], preferred_element_type=F32)
            for lc, r in blocks[c * per:(c + 1) * per]:
                build(lc, r)

    parity = lax.rem(s, 2)

    @pl.when(parity == 0)
    def _():
        step(a0_ref, a1_ref)

    @pl.when(parity == 1)
    def _():
        step(a1_ref, a0_ref)

    @pl.when((km == nk - 1) & (s > 0))
    def _():
        def finish(rows):
            xn = x_ref[rows, :] + gate_ref[...] * o_ref[rows, :]
            if final:
                ms = jnp.mean(xn * xn, axis=-1, keepdims=True)
                o_ref[rows, :] = (xn * lax.rsqrt(ms + NORM_EPS)) * ng_ref[...]
            else:
                o_ref[rows, :] = xn
                h_ref[rows, :] = _prenorm_math(xn, ng_ref[...], sc_ref[...], sh_ref[...]).astype(BF16)

        _row_loop(tm, finish)


def _down_ffn(a, dw, dw_b, w, x, gate, norm, on_grid, final, name):
    ng, sc, sh = norm
    m = a.shape[0]
    f, d = w.shape
    tm = _tile(m, 512, 2 * GRID_W)
    tk = _tile(f, 512, LANE)
    hb = 2 * GRID_W
    ratio = tm // hb
    nhb = m // hb
    koff = f // tk
    n_i, nk = m // tm, f // tk
    last = n_i * nk - 1

    def pro(s):
        p = jnp.minimum(s, last)
        return p // nk, lax.rem(p, nk)

    def mat(s):
        q = jnp.maximum(s - 1, 0)
        return q // nk, lax.rem(q, nk)

    vec = pl.BlockSpec((1, d), lambda s: (0, 0))
    rowblk = pl.BlockSpec((tm, d), lambda s: (mat(s)[0], 0))
    in_specs = [pl.BlockSpec((tm, tk), lambda s: pro(s)),
                pl.BlockSpec((hb, tk), lambda s: (jnp.maximum(pro(s)[0] * ratio - 1, 0), pro(s)[1])),
                pl.BlockSpec((hb, tk), lambda s: (jnp.minimum((pro(s)[0] + 1) * ratio, nhb - 1), pro(s)[1])),
                pl.BlockSpec((tm, tk), lambda s: (pro(s)[0], pro(s)[1] + koff)),
                pl.BlockSpec((9, tk), lambda s: (0, pro(s)[1])),
                pl.BlockSpec((1, tk), lambda s: (0, pro(s)[1])),
                pl.BlockSpec((tk, d), lambda s: (mat(s)[1], 0)),
                rowblk, vec, vec, vec, vec]
    out_specs = [rowblk]
    out_shape = [jax.ShapeDtypeStruct((m, d), F32)]
    if not final:
        out_specs.append(rowblk)
        out_shape.append(jax.ShapeDtypeStruct((m, d), BF16))
    res = pl.pallas_call(
        functools.partial(_ffn_down_kernel, on_grid=on_grid, final=final, n_i=n_i, nk=nk),
        grid=(n_i * nk + 1,),
        in_specs=in_specs,
        out_specs=out_specs,
        out_shape=out_shape,
        scratch_shapes=[pltpu.VMEM((tm + 2 * hb, tk), F32), pltpu.VMEM((tm, tk), BF16),
                        pltpu.VMEM((tm, tk), BF16)],
        compiler_params=_cp("arbitrary"),
        name=name,
    )(a, a, a, a, dw.reshape(9, f), _row(dw_b), w, x, _row(gate), _row(ng), _row(sc), _row(sh))
    return res[0] if final else res


def _sgu_kernel(u_ref, v_ref, g_ref, b_ref, ws_ref, bs_ref, o_ref):
    v = v_ref[...]
    mu = jnp.mean(v, axis=-1, keepdims=True)
    vc = v - mu
    var = jnp.mean(vc * vc, axis=-1, keepdims=True)
    vn = (vc * lax.rsqrt(var + NORM_EPS) * g_ref[...] + b_ref[...]).astype(BF16)
    groups = ws_ref.shape[0]
    gw = v.shape[1] // groups
    for g in range(groups):
        mixed = jnp.dot(ws_ref[g], vn[:, g * gw:(g + 1) * gw], preferred_element_type=F32) + bs_ref[g]
        o_ref[:, g * gw:(g + 1) * gw] = (u_ref[:, g * gw:(g + 1) * gw] * mixed).astype(o_ref.dtype)


def _sgu_mix(a, ln_g, ln_b, ws, bs):
    m = a.shape[0]
    e = a.shape[1] // 2
    q = SGU_CHUNK
    groups = ws.shape[0]
    return pl.pallas_call(
        _sgu_kernel,
        grid=(m // q,),
        in_specs=[pl.BlockSpec((q, e), lambda c: (c, 0)),
                  pl.BlockSpec((q, e), lambda c: (c, 1)),
                  pl.BlockSpec((1, e), lambda c: (0, 0)),
                  pl.BlockSpec((1, e), lambda c: (0, 0)),
                  pl.BlockSpec((groups, q, q), lambda c: (0, 0, 0)),
                  pl.BlockSpec((groups, q, 1), lambda c: (0, 0, 0))],
        out_specs=pl.BlockSpec((q, e), lambda c: (c, 0)),
        out_shape=jax.ShapeDtypeStruct((m, e), BF16),
        compiler_params=_cp("arbitrary"),
        name="sgu_mix",
    )(a, a, _row(ln_g), _row(ln_b), ws.astype(BF16), bs.reshape(groups, q, 1))


def _split3(a):
    a1 = a.astype(BF16)
    r1 = a - a1.astype(F32)
    a2 = r1.astype(BF16)
    a3 = (r1 - a2.astype(F32)).astype(BF16)
    return [a1, a2, a3]


def _ssd_kernel(*refs, reverse, mode):
    x_ref, b_ref, c_ref, dt_ref, dtt_ref, al_ref, alt_ref, s0_ref = refs[:8]
    if mode == "gated":
        yprev_ref, d_ref, z_ref, ng_ref, y_ref, sfin_ref, st_ref, yacc_ref = refs[8:]
    elif mode == "y":
        y_ref, sfin_ref, st_ref = refs[8:]
    else:
        sfin_ref, st_ref = refs[8:]
    c = pl.program_id(1)

    @pl.when(c == 0)
    def _():
        st_ref[...] = s0_ref[0]

    q = x_ref.shape[0]
    hpg = dt_ref.shape[3]
    pw = 2 * SSD_HEADDIM
    row = lax.broadcasted_iota(jnp.int32, (q, q), 0)
    col = lax.broadcasted_iota(jnp.int32, (q, q), 1)
    mask = (row <= col) if reverse else (row >= col)
    mask_t = (row >= col) if reverse else (row <= col)
    dtt = dtt_ref[0, 0]
    dtat = dtt * (-jnp.exp(alt_ref[0, 0]))
    cumt = jnp.dot(jnp.concatenate(_split3(dtat), axis=1), jnp.concatenate([mask_t.astype(BF16)] * 3, axis=0),
                   preferred_element_type=F32)
    lastt = jnp.sum(dtat, axis=1, keepdims=True)
    vt = dtt * jnp.exp(lastt - cumt)
    dta = dt_ref[0, 0] * (-jnp.exp(al_ref[0, 0]))
    elast = jnp.exp(jnp.sum(dta, axis=0, keepdims=True))
    bmat = b_ref[...]
    bt = bmat.T
    lo = lax.broadcasted_iota(jnp.int32, (1, pw), 1) < SSD_HEADDIM
    if mode != "state":
        cum = jnp.dot(jnp.concatenate([mask.astype(BF16)] * 3, axis=1), jnp.concatenate(_split3(dta), axis=0),
                      preferred_element_type=F32)
        cmat = c_ref[...]
        cb = lax.dot_general(cmat.astype(BF16), bmat.astype(BF16), (((1,), (1,)), ((), ())),
                             preferred_element_type=F32)

    def halves(a):
        return jnp.concatenate([jnp.where(lo, a, 0.0), jnp.where(lo, 0.0, a)], axis=0).astype(BF16)

    for j in range(hpg // 2):
        h0, h1 = 2 * j, 2 * j + 1
        cols = slice(j * pw, (j + 1) * pw)
        xs = x_ref[:, cols]
        xcat = halves(xs)
        st = st_ref[j]
        btv = jnp.concatenate([bt * vt[h0:h0 + 1, :], bt * vt[h1:h1 + 1, :]], axis=1).astype(BF16)
        st_ref[j] = (st * jnp.where(lo, elast[:, h0:h0 + 1], elast[:, h1:h1 + 1])
                     + jnp.dot(btv, xcat, preferred_element_type=F32))
        if mode == "state":
            continue

        def lhs_parts(h):
            colb = jnp.broadcast_to(cum[:, h:h + 1], (q, q))
            e = jnp.where(mask, jnp.exp(colb - cumt[h:h + 1, :]), 0.0)
            return (cb * e * dtt[h:h + 1, :]).astype(BF16), (cmat * jnp.exp(colb)).astype(BF16)

        m0, c0 = lhs_parts(h0)
        m1, c1 = lhs_parts(h1)
        lhs = jnp.concatenate([m0, m1, c0, c1], axis=1)
        rhs = jnp.concatenate([xcat, halves(st)], axis=0)
        y = jnp.dot(lhs, rhs, preferred_element_type=F32)
        if mode == "gated":
            dlane = jnp.where(lo, d_ref[0][:, h0:h0 + 1], d_ref[0][:, h1:h1 + 1])
            yacc_ref[:, cols] = y + yprev_ref[:, cols] + xs * dlane
        else:
            y_ref[:, cols] = y

    if mode == "gated":
        yz = yacc_ref[...] * _silu(z_ref[...])
        ms = jnp.mean(yz * yz, axis=-1, keepdims=True)
        y_ref[...] = (yz * lax.rsqrt(ms + NORM_EPS) * ng_ref[...]).astype(y_ref.dtype)

    @pl.when(c == pl.num_programs(1) - 1)
    def _():
        sfin_ref[0] = st_ref[...]


def _ssd_scan(xbc, dt_a, dt_b, al_a, al_b, s0, direction, mode, extra=()):
    length = xbc.shape[0]
    groups, hpg = dt_a.shape[1], dt_a.shape[3]
    q = SSD_CHUNK
    n = SSD_STATE
    pw = 2 * SSD_HEADDIM
    gw = hpg * SSD_HEADDIM
    d_inner = groups * gw
    nc = length // q
    reverse = direction == 1
    boff = d_inner // n

    def cc(c):
        return nc - 1 - c if reverse else c

    xblk = pl.BlockSpec((q, gw), lambda g, c: (cc(c), g))
    stblk = pl.BlockSpec((1, hpg // 2, n, pw), lambda g, c: (g, 0, 0, 0))
    in_specs = [xblk,
                pl.BlockSpec((q, n), lambda g, c: (cc(c), boff + g)),
                pl.BlockSpec((q, n), lambda g, c: (cc(c), boff + groups + g)),
                pl.BlockSpec((1, 1, q, hpg), lambda g, c: (direction, g, cc(c), 0)),
                pl.BlockSpec((1, 1, hpg, q), lambda g, c: (direction, g, 0, cc(c))),
                pl.BlockSpec((1, 1, 1, hpg), lambda g, c: (direction, g, 0, 0)),
                pl.BlockSpec((1, 1, hpg, 1), lambda g, c: (direction, g, 0, 0)),
                stblk]
    args = [xbc, xbc, xbc, dt_a, dt_b, al_a, al_b, s0]
    out_specs, out_shape = [], []
    scratch = [pltpu.VMEM((hpg // 2, n, pw), F32)]
    if mode == "gated":
        yprev, dskip, z, norm_g = extra
        in_specs += [xblk, pl.BlockSpec((1, 1, hpg), lambda g, c: (g, 0, 0)), xblk,
                     pl.BlockSpec((1, gw), lambda g, c: (0, g))]
        args += [yprev, dskip, z, _row(norm_g)]
        scratch.append(pltpu.VMEM((q, gw), F32))
    if mode != "state":
        out_specs.append(xblk)
        out_shape.append(jax.ShapeDtypeStruct((length, d_inner), BF16 if mode == "gated" else F32))
    out_specs.append(stblk)
    out_shape.append(jax.ShapeDtypeStruct(s0.shape, F32))
    return pl.pallas_call(
        functools.partial(_ssd_kernel, reverse=reverse, mode=mode),
        grid=(groups, nc),
        in_specs=in_specs,
        out_specs=out_specs,
        out_shape=out_shape,
        scratch_shapes=scratch,
        compiler_params=_cp("arbitrary", "arbitrary"),
        name="ssd_scan_%s_%s" % ("bwd" if reverse else "fwd", mode),
    )(*args)


def _ssd_inputs(h, w_bf, conv_w, conv_b, dt_bias, d_inner, conv_dim, heads2, name):
    length = h.shape[0]
    xbc_raw = _up(h, w_bf, conv_dim, [d_inner], None, _epi_id, F32, name + "_xbc")
    dt = _up(h, w_bf[:, d_inner + conv_dim:], heads2, [0], dt_bias.reshape(-1), _epi_softplus, F32,
             name + "_dt", tn_pref=heads2)
    xbc = _seqconv(xbc_raw, conv_w, conv_b, tile_rows=_tile(length, 128, SUBLANE), act="silu",
                   name=name + "_conv")
    hpg = heads2 // 2 // SSD_GROUPS
    dt4 = dt.reshape(length, 2, SSD_GROUPS, hpg)
    return xbc, jnp.transpose(dt4, (1, 2, 0, 3)), jnp.transpose(dt4, (1, 2, 3, 0))


def kernel(x, c, ctx, c_ctx, norm_mix_g, norm_ffn_g, mod_wa, mod_wb, mod_b, conv_w1, conv_b1, conv_dw, conv_dw_b, conv_ln_g, conv_ln_b, conv_w2, conv_b2, ssd_w_in, ssd_conv_w, ssd_conv_b, ssd_a_log, ssd_dt_bias, ssd_d, ssd_norm_g, ssd_w_out, sgu_w1, sgu_b1, sgu_ln_g, sgu_ln_b, sgu_ws, sgu_bs, sgu_w2, sgu_b2, ffn_w_in, ffn_dw, ffn_dw_b, ffn_w_out, final_g):
    bsz, seq, d = x.shape
    assert bsz == 1
    depth = norm_mix_g.shape[0]
    xl = x.reshape(seq, d)
    xc = ctx.reshape(ctx.shape[1], d)
    grid_rows = seq // GRID_W
    f_hidden = ffn_w_out.shape[1]

    v8 = jnp.zeros((SUBLANE, d), F32).at[0].set(c[0]).at[1].set(c_ctx)
    mod = _adaln(v8, mod_wa, mod_wb, mod_b)

    def norms(layer, row):
        sh1, sc1, g1, sh2, sc2, g2 = [mod[layer, row, t * d:(t + 1) * d] for t in range(6)]
        return (norm_mix_g[layer], sc1, sh1), (norm_ffn_g[layer], sc2, sh2), g1, g2

    def next_norm(layer, row):
        if layer + 1 < depth:
            return norms(layer + 1, row)[0]
        zeros = jnp.zeros((d,), F32)
        return final_g, zeros, zeros

    ssd_layers = [i for i in range(depth) if i % DEPTH_MIXERS == 1]
    last_ctx = ssd_layers[-1] if ssd_layers else -1

    hl = _prenorm(xl, norms(0, 0)[0])
    hc = _prenorm(xc, norms(0, 1)[0]) if last_ctx >= 0 else None
    for i in range(depth):
        kind, k = i % DEPTH_MIXERS, i // DEPTH_MIXERS
        ctx_full = i < last_ctx
        _, norm2, g1, g2 = norms(i, 0)
        if i <= last_ctx:
            _, cnorm2, cg1, cg2 = norms(i, 1)
        tag = "l%d" % i

        if kind == 0:
            w1 = conv_w1[k].astype(BF16)
            w2 = conv_w2[k].astype(BF16)
            gl = _up(hl, w1, d, [0, d], conv_b1[k], _epi_glu, F32, tag + "_conv_up")
            if k % 2 == 0:
                gl = _rowconv(gl, conv_dw[k], conv_dw_b[k], seg=GRID_W, tile_rows=2 * GRID_W, name=tag + "_rowconv")
            else:
                gl = _colconv(gl.reshape(grid_rows, GRID_W, d), conv_dw[k], conv_dw_b[k],
                              tag + "_colconv").reshape(seq, d)
            al = _ln_silu(gl, conv_ln_g[k], conv_ln_b[k])
            xl = _down_mm(al, w2, conv_b2[k], xl, g1, tag + "_conv_down")
            if ctx_full:
                gc = _up(hc, w1, d, [0, d], conv_b1[k], _epi_glu, F32, tag + "_conv_up_ctx")
                gc = _rowconv(gc, conv_dw[k], conv_dw_b[k], seg=gc.shape[0], tile_rows=gc.shape[0],
                              name=tag + "_seqconv_ctx")
                ac = _ln_silu(gc, conv_ln_g[k], conv_ln_b[k])
                xc = _down_mm(ac, w2, conv_b2[k], xc, cg1, tag + "_conv_down_ctx")
        elif kind == 1:
            assert not ctx_full
            w_in = ssd_w_in[k].astype(BF16)
            w_out = ssd_w_out[k].astype(BF16)
            d_inner = w_out.shape[0]
            heads2 = 2 * ssd_a_log.shape[2]
            conv_dim = w_in.shape[1] - d_inner - heads2
            hpg = heads2 // 2 // SSD_GROUPS
            al_a = ssd_a_log[k].reshape(2, SSD_GROUPS, 1, hpg)
            al_b = ssd_a_log[k].reshape(2, SSD_GROUPS, hpg, 1)
            dskip = ssd_d[k].reshape(SSD_GROUPS, 1, hpg)
            zero_state = jnp.zeros((SSD_GROUPS, hpg // 2, SSD_STATE, 2 * SSD_HEADDIM), F32)
            xbc_c, dta_c, dtb_c = _ssd_inputs(hc, w_in, ssd_conv_w[k], ssd_conv_b[k], ssd_dt_bias[k],
                                              d_inner, conv_dim, heads2, tag + "_ssd_ctx")
            (state_f,) = _ssd_scan(xbc_c, dta_c, dtb_c, al_a, al_b, zero_state, 0, "state")
            (state_b,) = _ssd_scan(xbc_c, dta_c, dtb_c, al_a, al_b, zero_state, 1, "state")
            xbc_l, dta_l, dtb_l = _ssd_inputs(hl, w_in, ssd_conv_w[k], ssd_conv_b[k], ssd_dt_bias[k],
                                              d_inner, conv_dim, heads2, tag + "_ssd")
            z = _up(hl, w_in, d_inner, [0], None, _epi_id, F32, tag + "_ssd_z")
            y_f, _ = _ssd_scan(xbc_l, dta_l, dtb_l, al_a, al_b, state_f, 0, "y")
            p, _ = _ssd_scan(xbc_l, dta_l, dtb_l, al_a, al_b, state_b, 1, "gated",
                             (y_f, dskip, z, ssd_norm_g[k]))
            xl = _down_mm(p, w_out, None, xl, g1, tag + "_ssd_down")
        else:
            w1 = sgu_w1[k].astype(BF16)
            w2 = sgu_w2[k].astype(BF16)

            def sgu(h, name):
                a = _up(h, w1, w1.shape[1], [0], sgu_b1[k], _epi_gelu, F32, name + "_up")
                return _sgu_mix(a, sgu_ln_g[k], sgu_ln_b[k], sgu_ws[k], sgu_bs[k])

            xl = _down_mm(sgu(hl, tag + "_sgu"), w2, sgu_b2[k], xl, g1, tag + "_sgu_down")
            if ctx_full:
                xc = _down_mm(sgu(hc, tag + "_sgu_ctx"), w2, sgu_b2[k], xc, cg1, tag + "_sgu_down_ctx")

        fw_in = ffn_w_in[i].astype(BF16)
        fw_out = ffn_w_out[i].astype(BF16)
        final = i == depth - 1
        a = _up(_prenorm(xl, norm2), fw_in, 2 * f_hidden, [0], None, _epi_id, F32, tag + "_ffn_up")
        res = _down_ffn(a, ffn_dw[i], ffn_dw_b[i], fw_out, xl, g2, next_norm(i, 0), True, final, tag + "_ffn_down")
        if final:
            xl = res
        else:
            xl, hl = res
        if ctx_full:
            ac = _up(_prenorm(xc, cnorm2), fw_in, 2 * f_hidden, [0], None, _epi_id, F32, tag + "_ffn_up_ctx")
            xc, hc = _down_ffn(ac, ffn_dw[i], ffn_dw_b[i], fw_out, xc, cg2, next_norm(i, 1), False, False,
                               tag + "_ffn_down_ctx")
    return xl.reshape(bsz, seq, d)
```

```python
import functools

import jax
import jax.numpy as jnp
from jax import lax
from jax.experimental import pallas as pl
from jax.experimental.pallas import tpu as pltpu

F32 = jnp.float32
BF16 = jnp.bfloat16
HI = lax.Precision.HIGHEST

NORM_EPS = 1e-6
GRID_W = 64
DEPTH_MIXERS = 3
SSD_GROUPS = 8
SSD_HEADDIM = 64
SSD_STATE = 128
SSD_CHUNK = 128
SSD_BLOCK = 2
SGU_CHUNK = 128
SGU_GROUPS = 16
LANE = 128
SUBLANE = 8
CONV_LANES = 256
CONV_ROWS = 64
ROW_CHUNK = 64
VMEM_LIMIT = 58 * 1024 * 1024


def _cp(*sem):
    return pltpu.CompilerParams(dimension_semantics=sem, vmem_limit_bytes=VMEM_LIMIT)


def _tile(n, pref, align):
    if n <= pref:
        return n
    t = (pref // align) * align
    while t >= align:
        if n % t == 0:
            return t
        t -= align
    return n


def _row(v):
    return v.reshape(1, -1)


def _prenorm_math(x, g, sc, sh):
    ms = jnp.mean(x * x, axis=-1, keepdims=True)
    return (x * lax.rsqrt(ms + NORM_EPS)) * g * (1.0 + sc) + sh


def _silu(x):
    return x * jax.nn.sigmoid(x)


def _row_loop(nrows, fn):
    step = min(nrows, ROW_CHUNK)

    def body(r, carry):
        fn(pl.ds(pl.multiple_of(r * step, step), step))
        return carry

    lax.fori_loop(0, nrows // step, body, 0)


def _adaln_kernel(v_ref, wa_ref, wb_ref, b_ref, o_ref, t_ref):
    @pl.when(pl.program_id(1) == 0)
    def _():
        t_ref[...] = jnp.dot(_silu(v_ref[...]), wa_ref[0], preferred_element_type=F32, precision=HI)

    o_ref[0] = jnp.dot(t_ref[...], wb_ref[0], preferred_element_type=F32, precision=HI) + b_ref[0]


def _adaln(v8, wa, wb, b):
    depth, d, r = wa.shape
    n = wb.shape[2]
    tn = _tile(n, 6144, LANE)
    return pl.pallas_call(
        _adaln_kernel,
        grid=(depth, n // tn),
        in_specs=[pl.BlockSpec((SUBLANE, d), lambda l, j: (0, 0)),
                  pl.BlockSpec((1, d, r), lambda l, j: (l, 0, 0)),
                  pl.BlockSpec((1, r, tn), lambda l, j: (l, 0, j)),
                  pl.BlockSpec((1, 1, tn), lambda l, j: (l, 0, j))],
        out_specs=pl.BlockSpec((1, SUBLANE, tn), lambda l, j: (l, 0, j)),
        out_shape=jax.ShapeDtypeStruct((depth, SUBLANE, n), F32),
        scratch_shapes=[pltpu.VMEM((SUBLANE, r), F32)],
        compiler_params=_cp("arbitrary", "arbitrary"),
        name="adaln",
    )(v8, wa, wb, b.reshape(depth, 1, n))


def _prenorm_kernel(x_ref, g_ref, sc_ref, sh_ref, o_ref):
    o_ref[...] = _prenorm_math(x_ref[...], g_ref[...], sc_ref[...], sh_ref[...]).astype(o_ref.dtype)


def _prenorm(x, norm):
    g, sc, sh = norm
    m, d = x.shape
    tm = _tile(m, 256, SUBLANE)
    vec = pl.BlockSpec((1, d), lambda i: (0, 0))
    return pl.pallas_call(
        _prenorm_kernel,
        grid=(m // tm,),
        in_specs=[pl.BlockSpec((tm, d), lambda i: (i, 0)), vec, vec, vec],
        out_specs=pl.BlockSpec((tm, d), lambda i: (i, 0)),
        out_shape=jax.ShapeDtypeStruct((m, d), BF16),
        compiler_params=_cp("arbitrary"),
        name="prenorm",
    )(x, _row(g), _row(sc), _row(sh))


def _ln_silu_kernel(x_ref, g_ref, b_ref, o_ref):
    x = x_ref[...]
    mu = jnp.mean(x, axis=-1, keepdims=True)
    xc = x - mu
    var = jnp.mean(xc * xc, axis=-1, keepdims=True)
    y = xc * lax.rsqrt(var + NORM_EPS) * g_ref[...] + b_ref[...]
    o_ref[...] = _silu(y).astype(o_ref.dtype)


def _ln_silu(x, g, b):
    m, d = x.shape
    tm = _tile(m, 256, SUBLANE)
    vec = pl.BlockSpec((1, d), lambda i: (0, 0))
    return pl.pallas_call(
        _ln_silu_kernel,
        grid=(m // tm,),
        in_specs=[pl.BlockSpec((tm, d), lambda i: (i, 0)), vec, vec],
        out_specs=pl.BlockSpec((tm, d), lambda i: (i, 0)),
        out_shape=jax.ShapeDtypeStruct((m, d), BF16),
        compiler_params=_cp("arbitrary"),
        name="ln_silu",
    )(x, _row(g), _row(b))


def _up_kernel(*refs, n_w, has_bias, epi, sub):
    h_ref = refs[0]
    w_refs = refs[1:1 + n_w]
    b_refs = refs[1 + n_w:1 + 2 * n_w] if has_bias else ()
    o_ref = refs[-1]
    h = h_ref[...]
    for n0 in range(0, o_ref.shape[1], sub):
        accs = []
        for t in range(n_w):
            a = jnp.dot(h, w_refs[t][:, n0:n0 + sub], preferred_element_type=F32)
            if has_bias:
                a = a + b_refs[t][:, n0:n0 + sub]
            accs.append(a)
        o_ref[:, n0:n0 + sub] = epi(*accs).astype(o_ref.dtype)


def _epi_id(a):
    return a


def _epi_glu(a, b):
    return a * jax.nn.sigmoid(b)


def _epi_gelu(a):
    return jax.nn.gelu(a)


def _epi_softplus(a):
    return jnp.maximum(a, 0.0) + jnp.log1p(jnp.exp(-jnp.abs(a)))


def _up(h, w, n_out, col_offs, bias, epi, out_dtype, name, tn_pref=1024):
    m, k = h.shape
    tm = _tile(m, 1024, SUBLANE)
    tn = _tile(n_out, tn_pref // len(col_offs), LANE)
    for off in col_offs:
        assert off % tn == 0
    n_w = len(col_offs)
    in_specs = [pl.BlockSpec((tm, k), lambda i, j: (i, 0))]
    args = [h]
    for off in col_offs:
        in_specs.append(pl.BlockSpec((k, tn), functools.partial(lambda i, j, o: (0, j + o), o=off // tn)))
        args.append(w)
    if bias is not None:
        for off in col_offs:
            in_specs.append(pl.BlockSpec((1, tn), functools.partial(lambda i, j, o: (0, j + o), o=off // tn)))
            args.append(_row(bias))
    return pl.pallas_call(
        functools.partial(_up_kernel, n_w=n_w, has_bias=bias is not None, epi=epi,
                          sub=256 if tn % 256 == 0 else LANE),
        grid=(m // tm, n_out // tn),
        in_specs=in_specs,
        out_specs=pl.BlockSpec((tm, tn), lambda i, j: (i, j)),
        out_shape=jax.ShapeDtypeStruct((m, n_out), out_dtype),
        compiler_params=_cp("arbitrary", "arbitrary"),
        name=name,
    )(*args)


def _seqconv_kernel(x_ref, xp_ref, xn_ref, w_ref, b_ref, o_ref, buf_ref, *, ntaps, act):
    t_rows, c = x_ref.shape
    pad = xp_ref.shape[0]
    i = pl.program_id(0)
    buf_ref[pl.ds(0, pad), :] = jnp.where(i > 0, xp_ref[...].astype(F32), 0.0)
    buf_ref[pl.ds(pad, t_rows), :] = x_ref[...].astype(F32)
    buf_ref[pl.ds(pad + t_rows, pad), :] = jnp.where(i < pl.num_programs(0) - 1, xn_ref[...].astype(F32), 0.0)
    half = ntaps // 2
    sub = min(t_rows, CONV_ROWS)
    lw = min(c, CONV_LANES)

    def body(ci, carry):
        lanes = pl.ds(pl.multiple_of(ci * lw, lw), lw)
        bias = b_ref[:, lanes]
        for s in range(t_rows // sub):
            acc = None
            for k in range(ntaps):
                v = buf_ref[pl.ds(pad + s * sub + k - half, sub), lanes] * w_ref[pl.ds(k, 1), lanes]
                acc = v if acc is None else acc + v
            acc = acc + bias
            if act == "silu":
                acc = _silu(acc)
            o_ref[pl.ds(s * sub, sub), lanes] = acc
        return carry

    lax.fori_loop(0, c // lw, body, 0)


def _seqconv(x, w, b, *, tile_rows, act, name):
    rows, c = x.shape
    ntaps = w.shape[0]
    pad = SUBLANE * (4 // x.dtype.itemsize)
    assert ntaps // 2 <= pad and rows % tile_rows == 0
    ratio = tile_rows // pad
    nhb = rows // pad
    return pl.pallas_call(
        functools.partial(_seqconv_kernel, ntaps=ntaps, act=act),
        grid=(rows // tile_rows,),
        in_specs=[pl.BlockSpec((tile_rows, c), lambda i: (i, 0)),
                  pl.BlockSpec((pad, c), lambda i: (jnp.maximum(i * ratio - 1, 0), 0)),
                  pl.BlockSpec((pad, c), lambda i: (jnp.minimum((i + 1) * ratio, nhb - 1), 0)),
                  pl.BlockSpec((ntaps, c), lambda i: (0, 0)),
                  pl.BlockSpec((1, c), lambda i: (0, 0))],
        out_specs=pl.BlockSpec((tile_rows, c), lambda i: (i, 0)),
        out_shape=jax.ShapeDtypeStruct((rows, c), F32),
        scratch_shapes=[pltpu.VMEM((tile_rows + 2 * pad, c), F32)],
        compiler_params=_cp("arbitrary"),
        name=name,
    )(x, x, x, w, _row(b))


def _rowconv_kernel(x_ref, w_ref, b_ref, o_ref, pad_ref, ph_ref, *, ntaps, seg):
    t_rows, c = x_ref.shape
    nseg = t_rows // seg
    lead = 2 * SUBLANE
    half = ntaps // 2
    amax = (lead - half + ntaps - 1) // SUBLANE
    span = seg + SUBLANE * amax
    stride = span + SUBLANE
    lw = ph_ref.shape[2]
    sub = min(seg, CONV_ROWS)
    for r in range(nseg):
        pad_ref[pl.ds(r * stride, lead), :] = jnp.zeros((lead, c), F32)
        pad_ref[pl.ds(r * stride + lead, seg), :] = x_ref[pl.ds(r * seg, seg), :]
        pad_ref[pl.ds(r * stride + lead + seg, stride - lead - seg), :] = jnp.zeros((stride - lead - seg, c), F32)

    def body(ci, carry):
        lanes = pl.ds(pl.multiple_of(ci * lw, lw), lw)
        bias = b_ref[:, lanes]
        for r in range(nseg):
            for b in range(1, SUBLANE):
                ph_ref[b] = pad_ref[pl.ds(r * stride + b, span), lanes]
            for s in range(0, seg, sub):
                acc = None
                for k in range(ntaps):
                    off = lead - half + k
                    a, b = off // SUBLANE, off % SUBLANE
                    if b == 0:
                        src = pad_ref[pl.ds(r * stride + SUBLANE * a + s, sub), lanes]
                    else:
                        src = ph_ref[b, pl.ds(SUBLANE * a + s, sub), :]
                    v = src * w_ref[pl.ds(k, 1), lanes]
                    acc = v if acc is None else acc + v
                o_ref[pl.ds(r * seg + s, sub), lanes] = acc + bias
        return carry

    lax.fori_loop(0, c // lw, body, 0)


def _rowconv(x, w, b, *, seg, tile_rows, name):
    rows, c = x.shape
    ntaps = w.shape[0]
    lead = 2 * SUBLANE
    assert ntaps // 2 <= lead and seg % SUBLANE == 0 and tile_rows % seg == 0
    amax = (lead - ntaps // 2 + ntaps - 1) // SUBLANE
    span = seg + SUBLANE * amax
    stride = span + SUBLANE
    lw = min(c, CONV_LANES)
    return pl.pallas_call(
        functools.partial(_rowconv_kernel, ntaps=ntaps, seg=seg),
        grid=(rows // tile_rows,),
        in_specs=[pl.BlockSpec((tile_rows, c), lambda i: (i, 0)),
                  pl.BlockSpec((ntaps, c), lambda i: (0, 0)),
                  pl.BlockSpec((1, c), lambda i: (0, 0))],
        out_specs=pl.BlockSpec((tile_rows, c), lambda i: (i, 0)),
        out_shape=jax.ShapeDtypeStruct((rows, c), F32),
        scratch_shapes=[pltpu.VMEM(((tile_rows // seg) * stride, c), F32),
                        pltpu.VMEM((SUBLANE, span, lw), F32)],
        compiler_params=_cp("arbitrary"),
        name=name,
    )(x, w, _row(b))


def _colconv_kernel(x_ref, w_ref, b_ref, o_ref, buf_ref, *, ntaps):
    rows, width, lw = x_ref.shape
    half = ntaps // 2
    lead = buf_ref.shape[0] - rows - half
    zeros = jnp.zeros((width, lw), F32)
    for r in range(lead):
        buf_ref[r] = zeros
    for r in range(half):
        buf_ref[lead + rows + r] = zeros

    def copy(r, carry):
        buf_ref[lead + r] = x_ref[r]
        return carry

    lax.fori_loop(0, rows, copy, 0)

    def body(r, carry):
        acc = None
        for k in range(ntaps):
            v = buf_ref[lead - half + r + k] * w_ref[pl.ds(k, 1), :]
            acc = v if acc is None else acc + v
        o_ref[r] = acc + b_ref[...]
        return carry

    lax.fori_loop(0, rows, body, 0)


def _colconv(x3, w, b, name):
    rows, width, c = x3.shape
    ntaps = w.shape[0]
    half = ntaps // 2
    lw = min(c, LANE)
    return pl.pallas_call(
        functools.partial(_colconv_kernel, ntaps=ntaps),
        grid=(c // lw,),
        in_specs=[pl.BlockSpec((rows, width, lw), lambda j: (0, 0, j)),
                  pl.BlockSpec((ntaps, lw), lambda j: (0, j)),
                  pl.BlockSpec((1, lw), lambda j: (0, j))],
        out_specs=pl.BlockSpec((rows, width, lw), lambda j: (0, 0, j)),
        out_shape=jax.ShapeDtypeStruct((rows, width, c), F32),
        scratch_shapes=[pltpu.VMEM((rows + 2 * half, width, lw), F32)],
        compiler_params=_cp("arbitrary"),
        name=name,
    )(x3, w, _row(b))


def _down_mm_kernel(*refs, has_bias):
    if has_bias:
        a_ref, w_ref, bias_ref, x_ref, gate_ref, o_ref = refs
    else:
        a_ref, w_ref, x_ref, gate_ref, o_ref = refs
    k = pl.program_id(2)
    part = jnp.dot(a_ref[...], w_ref[...], preferred_element_type=F32)

    @pl.when(k == 0)
    def _():
        o_ref[...] = part

    @pl.when(k > 0)
    def _():
        o_ref[...] += part

    @pl.when(k == pl.num_programs(2) - 1)
    def _():
        y = o_ref[...]
        if has_bias:
            y = y + bias_ref[...]
        o_ref[...] = x_ref[...] + gate_ref[...] * y


def _down_mm(a, w, bias, x, gate, name):
    m, kdim = a.shape
    d = w.shape[1]
    tm = _tile(m, 1024, SUBLANE)
    tn = _tile(d, 1024, LANE)
    tk = _tile(kdim, 2048, LANE)
    vec = pl.BlockSpec((1, tn), lambda i, j, k: (0, j))
    in_specs = [pl.BlockSpec((tm, tk), lambda i, j, k: (i, k)),
                pl.BlockSpec((tk, tn), lambda i, j, k: (k, j))]
    args = [a, w]
    if bias is not None:
        in_specs.append(vec)
        args.append(_row(bias))
    in_specs += [pl.BlockSpec((tm, tn), lambda i, j, k: (i, j)), vec]
    args += [x, _row(gate)]
    return pl.pallas_call(
        functools.partial(_down_mm_kernel, has_bias=bias is not None),
        grid=(m // tm, d // tn, kdim // tk),
        in_specs=in_specs,
        out_specs=pl.BlockSpec((tm, tn), lambda i, j, k: (i, j)),
        out_shape=jax.ShapeDtypeStruct((m, d), F32),
        compiler_params=_cp("arbitrary", "arbitrary", "arbitrary"),
        name=name,
    )(*args)


def _ffn_down_kernel(gc_ref, gp_ref, gn_ref, up_ref, dw_ref, dwb_ref, w_ref, x_ref, gate_ref, ng_ref, sc_ref,
                     sh_ref, *rest, on_grid, final, n_i, nk):
    if final:
        o_ref, buf_ref, a0_ref, a1_ref = rest
        h_ref = None
    else:
        o_ref, h_ref, buf_ref, a0_ref, a1_ref = rest
    tm, tk = gc_ref.shape
    hb = gp_ref.shape[0]
    d = o_ref.shape[1]
    s = pl.program_id(0)
    ip = jnp.minimum(s, n_i * nk - 1) // nk
    m = jnp.maximum(s - 1, 0)
    km = lax.rem(m, nk)

    @pl.when(s == 0)
    def _():
        a1_ref[...] = jnp.zeros_like(a1_ref)

    @pl.when(km == 0)
    def _():
        def zero(rows):
            o_ref[rows, :] = jnp.zeros((min(tm, ROW_CHUNK), d), F32)

        _row_loop(tm, zero)

    sub = min(tm, CONV_ROWS)
    lw = min(tk, CONV_LANES)
    dys = (-1, 0, 1) if on_grid else (0,)
    tn = min(d, 256)

    def step(dst_ref, src_ref):
        buf_ref[pl.ds(0, hb), :] = jnp.where(ip > 0, gp_ref[...].astype(F32), 0.0)
        buf_ref[pl.ds(hb, tm), :] = gc_ref[...].astype(F32)
        buf_ref[pl.ds(hb + tm, hb), :] = jnp.where(ip < n_i - 1, gn_ref[...].astype(F32), 0.0)
        col = lax.broadcasted_iota(jnp.int32, (sub, lw), 0)

        def build(lc, r):
            lanes = pl.ds(lc * lw, lw)
            base = hb + r * sub

            def tapsum(dx):
                acc = None
                for dy in dys:
                    v = (buf_ref[pl.ds(base + GRID_W * dy + dx, sub), lanes]
                         * dw_ref[pl.ds((dy + 1) * 3 + dx + 1, 1), lanes])
                    acc = v if acc is None else acc + v
                return acc

            left, right = tapsum(-1), tapsum(1)
            if on_grid:
                left = jnp.where(col >= 1, left, 0.0)
                right = jnp.where(col <= GRID_W - 2, right, 0.0)
            conv = tapsum(0) + left + right + dwb_ref[:, lanes]
            dst_ref[pl.ds(r * sub, sub), lanes] = (
                jax.nn.gelu(conv) * up_ref[pl.ds(r * sub, sub), lanes].astype(F32)).astype(BF16)

        blocks = [(lc, r) for lc in range(tk // lw) for r in range(tm // sub)]
        n_mm = d // tn
        per = -(-len(blocks) // n_mm)
        a = src_ref[...]
        for c in range(n_mm):
            n0 = c * tn
            o_ref[:, n0:n0 + tn] += jnp.dot(a, w_ref[:, n0:n0 + tn], preferred_element_type=F32)
            for lc, r in blocks[c * per:(c + 1) * per]:
                build(lc, r)

    parity = lax.rem(s, 2)

    @pl.when(parity == 0)
    def _():
        step(a0_ref, a1_ref)

    @pl.when(parity == 1)
    def _():
        step(a1_ref, a0_ref)

    @pl.when((km == nk - 1) & (s > 0))
    def _():
        def finish(rows):
            xn = x_ref[rows, :] + gate_ref[...] * o_ref[rows, :]
            if final:
                ms = jnp.mean(xn * xn, axis=-1, keepdims=True)
                o_ref[rows, :] = (xn * lax.rsqrt(ms + NORM_EPS)) * ng_ref[...]
            else:
                o_ref[rows, :] = xn
                h_ref[rows, :] = _prenorm_math(xn, ng_ref[...], sc_ref[...], sh_ref[...]).astype(BF16)

        _row_loop(tm, finish)


def _down_ffn(a, dw, dw_b, w, x, gate, norm, on_grid, final, name):
    ng, sc, sh = norm
    m = a.shape[0]
    f, d = w.shape
    tm = _tile(m, 512, 2 * GRID_W)
    tk = _tile(f, 512, LANE)
    hb = 2 * GRID_W
    ratio = tm // hb
    nhb = m // hb
    koff = f // tk
    n_i, nk = m // tm, f // tk
    last = n_i * nk - 1

    def pro(s):
        p = jnp.minimum(s, last)
        return p // nk, lax.rem(p, nk)

    def mat(s):
        q = jnp.maximum(s - 1, 0)
        return q // nk, lax.rem(q, nk)

    vec = pl.BlockSpec((1, d), lambda s: (0, 0))
    rowblk = pl.BlockSpec((tm, d), lambda s: (mat(s)[0], 0))
    in_specs = [pl.BlockSpec((tm, tk), lambda s: pro(s)),
                pl.BlockSpec((hb, tk), lambda s: (jnp.maximum(pro(s)[0] * ratio - 1, 0), pro(s)[1])),
                pl.BlockSpec((hb, tk), lambda s: (jnp.minimum((pro(s)[0] + 1) * ratio, nhb - 1), pro(s)[1])),
                pl.BlockSpec((tm, tk), lambda s: (pro(s)[0], pro(s)[1] + koff)),
                pl.BlockSpec((9, tk), lambda s: (0, pro(s)[1])),
                pl.BlockSpec((1, tk), lambda s: (0, pro(s)[1])),
                pl.BlockSpec((tk, d), lambda s: (mat(s)[1], 0)),
                rowblk, vec, vec, vec, vec]
    out_specs = [rowblk]
    out_shape = [jax.ShapeDtypeStruct((m, d), F32)]
    if not final:
        out_specs.append(rowblk)
        out_shape.append(jax.ShapeDtypeStruct((m, d), BF16))
    res = pl.pallas_call(
        functools.partial(_ffn_down_kernel, on_grid=on_grid, final=final, n_i=n_i, nk=nk),
        grid=(n_i * nk + 1,),
        in_specs=in_specs,
        out_specs=out_specs,
        out_shape=out_shape,
        scratch_shapes=[pltpu.VMEM((tm + 2 * hb, tk), F32), pltpu.VMEM((tm, tk), BF16),
                        pltpu.VMEM((tm, tk), BF16)],
        compiler_params=_cp("arbitrary"),
        name=name,
    )(a, a, a, a, dw.reshape(9, f), _row(dw_b), w, x, _row(gate), _row(ng), _row(sc), _row(sh))
    return res[0] if final else res


def _sgu_kernel(u_ref, v_ref, g_ref, b_ref, ws_ref, bs_ref, o_ref):
    v = v_ref[...].astype(F32)
    mu = jnp.mean(v, axis=-1, keepdims=True)
    vc = v - mu
    var = jnp.mean(vc * vc, axis=-1, keepdims=True)
    vn = (vc * lax.rsqrt(var + NORM_EPS) * g_ref[...] + b_ref[...]).astype(BF16)
    groups = ws_ref.shape[0]
    gw = v.shape[1] // groups
    for g in range(groups):
        mixed = jnp.dot(ws_ref[g], vn[:, g * gw:(g + 1) * gw], preferred_element_type=F32) + bs_ref[g]
        o_ref[:, g * gw:(g + 1) * gw] = (u_ref[:, g * gw:(g + 1) * gw].astype(F32) * mixed).astype(o_ref.dtype)


def _sgu_mix(a, ln_g, ln_b, ws, bs):
    m = a.shape[0]
    e = a.shape[1] // 2
    q = SGU_CHUNK
    groups = ws.shape[0]
    return pl.pallas_call(
        _sgu_kernel,
        grid=(m // q,),
        in_specs=[pl.BlockSpec((q, e), lambda c: (c, 0)),
                  pl.BlockSpec((q, e), lambda c: (c, 1)),
                  pl.BlockSpec((1, e), lambda c: (0, 0)),
                  pl.BlockSpec((1, e), lambda c: (0, 0)),
                  pl.BlockSpec((groups, q, q), lambda c: (0, 0, 0)),
                  pl.BlockSpec((groups, q, 1), lambda c: (0, 0, 0))],
        out_specs=pl.BlockSpec((q, e), lambda c: (c, 0)),
        out_shape=jax.ShapeDtypeStruct((m, e), BF16),
        compiler_params=_cp("arbitrary"),
        name="sgu_mix",
    )(a, a, _row(ln_g), _row(ln_b), ws.astype(BF16), bs.reshape(groups, q, 1))


def _split3(a):
    a1 = a.astype(BF16)
    r1 = a - a1.astype(F32)
    a2 = r1.astype(BF16)
    a3 = (r1 - a2.astype(F32)).astype(BF16)
    return [a1, a2, a3]


def _ssd_kernel(*refs, reverse, mode):
    x_ref, b_ref, c_ref, dt_ref, dtt_ref, al_ref, alt_ref, s0_ref = refs[:8]
    if mode == "gated":
        yprev_ref, d_ref, z_ref, ng_ref, y_ref, sfin_ref, st_ref, yacc_ref = refs[8:]
    elif mode == "y":
        y_ref, sfin_ref, st_ref = refs[8:]
    else:
        sfin_ref, st_ref = refs[8:]
    c = pl.program_id(1)

    @pl.when(c == 0)
    def _():
        st_ref[...] = s0_ref[0]

    q = SSD_CHUNK
    hpg = dt_ref.shape[3]
    pw = 2 * SSD_HEADDIM
    row = lax.broadcasted_iota(jnp.int32, (q, q), 0)
    col = lax.broadcasted_iota(jnp.int32, (q, q), 1)
    mask = (row <= col) if reverse else (row >= col)
    mask_t = (row >= col) if reverse else (row <= col)
    mask3 = jnp.concatenate([mask.astype(BF16)] * 3, axis=1)
    mask3_t = jnp.concatenate([mask_t.astype(BF16)] * 3, axis=0)
    lo = lax.broadcasted_iota(jnp.int32, (1, pw), 1) < SSD_HEADDIM
    neg_a = -jnp.exp(al_ref[0, 0])
    neg_a_t = -jnp.exp(alt_ref[0, 0])

    def halves(a):
        return jnp.concatenate([jnp.where(lo, a, 0.0), jnp.where(lo, 0.0, a)], axis=0).astype(BF16)

    def chunk(r0):
        rows = pl.ds(r0, q)
        dtt = dtt_ref[0, 0][:, r0:r0 + q]
        dtat = dtt * neg_a_t
        cumt = jnp.dot(jnp.concatenate(_split3(dtat), axis=1), mask3_t,
                       preferred_element_type=F32)
        lastt = jnp.sum(dtat, axis=1, keepdims=True)
        vt = dtt * jnp.exp(lastt - cumt)
        dta = dt_ref[0, 0, rows, :] * neg_a
        elast = jnp.exp(jnp.sum(dta, axis=0, keepdims=True))
        bmat = b_ref[rows, :]
        bt = bmat.T
        if mode != "state":
            cum = jnp.dot(mask3, jnp.concatenate(_split3(dta), axis=0), preferred_element_type=F32)
            cmat = c_ref[rows, :]
            cb = lax.dot_general(cmat.astype(BF16), bmat.astype(BF16), (((1,), (1,)), ((), ())),
                                 preferred_element_type=F32)

        for j in range(hpg // 2):
            h0, h1 = 2 * j, 2 * j + 1
            cols = slice(j * pw, (j + 1) * pw)
            xs = x_ref[rows, cols]
            xcat = halves(xs)
            st = st_ref[j]
            btv = jnp.concatenate([bt * vt[h0:h0 + 1, :], bt * vt[h1:h1 + 1, :]], axis=1).astype(BF16)
            st_ref[j] = (st * jnp.where(lo, elast[:, h0:h0 + 1], elast[:, h1:h1 + 1])
                         + jnp.dot(btv, xcat, preferred_element_type=F32))
            if mode == "state":
                continue

            def lhs_parts(h):
                colb = jnp.broadcast_to(cum[:, h:h + 1], (q, q))
                e = jnp.where(mask, jnp.exp(colb - cumt[h:h + 1, :]), 0.0)
                return (cb * e * dtt[h:h + 1, :]).astype(BF16), (cmat * jnp.exp(colb)).astype(BF16)

            m0, c0 = lhs_parts(h0)
            m1, c1 = lhs_parts(h1)
            lhs = jnp.concatenate([m0, m1, c0, c1], axis=1)
            rhs = jnp.concatenate([xcat, halves(st)], axis=0)
            y = jnp.dot(lhs, rhs, preferred_element_type=F32)
            if mode == "gated":
                dlane = jnp.where(lo, d_ref[0][:, h0:h0 + 1], d_ref[0][:, h1:h1 + 1])
                yacc_ref[rows, cols] = y + yprev_ref[rows, cols] + xs * dlane
            else:
                y_ref[rows, cols] = y

        if mode == "gated":
            yz = yacc_ref[rows, :] * _silu(z_ref[rows, :].astype(F32))
            ms = jnp.mean(yz * yz, axis=-1, keepdims=True)
            y_ref[rows, :] = (yz * lax.rsqrt(ms + NORM_EPS) * ng_ref[...]).astype(y_ref.dtype)

    starts = list(range(0, x_ref.shape[0], q))
    for r0 in (reversed(starts) if reverse else starts):
        chunk(r0)

    @pl.when(c == pl.num_programs(1) - 1)
    def _():
        sfin_ref[0] = st_ref[...]


def _ssd_scan(xbc, dt_a, dt_b, al_a, al_b, s0, direction, mode, extra=()):
    length = xbc.shape[0]
    groups, hpg = dt_a.shape[1], dt_a.shape[3]
    q = SSD_CHUNK * SSD_BLOCK
    n = SSD_STATE
    pw = 2 * SSD_HEADDIM
    gw = hpg * SSD_HEADDIM
    d_inner = groups * gw
    assert length % q == 0
    nc = length // q
    reverse = direction == 1
    boff = d_inner // n

    def cc(c):
        return nc - 1 - c if reverse else c

    xblk = pl.BlockSpec((q, gw), lambda g, c: (cc(c), g))
    stblk = pl.BlockSpec((1, hpg // 2, n, pw), lambda g, c: (g, 0, 0, 0))
    in_specs = [xblk,
                pl.BlockSpec((q, n), lambda g, c: (cc(c), boff + g)),
                pl.BlockSpec((q, n), lambda g, c: (cc(c), boff + groups + g)),
                pl.BlockSpec((1, 1, q, hpg), lambda g, c: (direction, g, cc(c), 0)),
                pl.BlockSpec((1, 1, hpg, q), lambda g, c: (direction, g, 0, cc(c))),
                pl.BlockSpec((1, 1, 1, hpg), lambda g, c: (direction, g, 0, 0)),
                pl.BlockSpec((1, 1, hpg, 1), lambda g, c: (direction, g, 0, 0)),
                stblk]
    args = [xbc, xbc, xbc, dt_a, dt_b, al_a, al_b, s0]
    out_specs, out_shape = [], []
    scratch = [pltpu.VMEM((hpg // 2, n, pw), F32)]
    if mode == "gated":
        yprev, dskip, z, norm_g = extra
        in_specs += [xblk, pl.BlockSpec((1, 1, hpg), lambda g, c: (g, 0, 0)), xblk,
                     pl.BlockSpec((1, gw), lambda g, c: (0, g))]
        args += [yprev, dskip, z, _row(norm_g)]
        scratch.append(pltpu.VMEM((q, gw), F32))
    if mode != "state":
        out_specs.append(xblk)
        out_shape.append(jax.ShapeDtypeStruct((length, d_inner), BF16 if mode == "gated" else F32))
    out_specs.append(stblk)
    out_shape.append(jax.ShapeDtypeStruct(s0.shape, F32))
    return pl.pallas_call(
        functools.partial(_ssd_kernel, reverse=reverse, mode=mode),
        grid=(groups, nc),
        in_specs=in_specs,
        out_specs=out_specs,
        out_shape=out_shape,
        scratch_shapes=scratch,
        compiler_params=_cp("arbitrary", "arbitrary"),
        name="ssd_scan_%s_%s" % ("bwd" if reverse else "fwd", mode),
    )(*args)


def _ssd_inputs(h, w_bf, conv_w, conv_b, dt_bias, d_inner, conv_dim, heads2, name):
    length = h.shape[0]
    xbc_raw = _up(h, w_bf, conv_dim, [d_inner], None, _epi_id, BF16, name + "_xbc")
    dt = _up(h, w_bf[:, d_inner + conv_dim:], heads2, [0], dt_bias.reshape(-1), _epi_softplus, F32,
             name + "_dt", tn_pref=heads2)
    xbc = _seqconv(xbc_raw, conv_w, conv_b, tile_rows=_tile(length, 128, SUBLANE), act="silu",
                   name=name + "_conv")
    hpg = heads2 // 2 // SSD_GROUPS
    dt4 = dt.reshape(length, 2, SSD_GROUPS, hpg)
    return xbc, jnp.transpose(dt4, (1, 2, 0, 3)), jnp.transpose(dt4, (1, 2, 3, 0))


def kernel(x, c, ctx, c_ctx, norm_mix_g, norm_ffn_g, mod_wa, mod_wb, mod_b, conv_w1, conv_b1, conv_dw, conv_dw_b, conv_ln_g, conv_ln_b, conv_w2, conv_b2, ssd_w_in, ssd_conv_w, ssd_conv_b, ssd_a_log, ssd_dt_bias, ssd_d, ssd_norm_g, ssd_w_out, sgu_w1, sgu_b1, sgu_ln_g, sgu_ln_b, sgu_ws, sgu_bs, sgu_w2, sgu_b2, ffn_w_in, ffn_dw, ffn_dw_b, ffn_w_out, final_g):
    bsz, seq, d = x.shape
    assert bsz == 1
    depth = norm_mix_g.shape[0]
    xl = x.reshape(seq, d)
    xc = ctx.reshape(ctx.shape[1], d)
    grid_rows = seq // GRID_W
    f_hidden = ffn_w_out.shape[1]

    v8 = jnp.zeros((SUBLANE, d), F32).at[0].set(c[0]).at[1].set(c_ctx)
    mod = _adaln(v8, mod_wa, mod_wb, mod_b)

    def norms(layer, row):
        sh1, sc1, g1, sh2, sc2, g2 = [mod[layer, row, t * d:(t + 1) * d] for t in range(6)]
        return (norm_mix_g[layer], sc1, sh1), (norm_ffn_g[layer], sc2, sh2), g1, g2

    def next_norm(layer, row):
        if layer + 1 < depth:
            return norms(layer + 1, row)[0]
        zeros = jnp.zeros((d,), F32)
        return final_g, zeros, zeros

    ssd_layers = [i for i in range(depth) if i % DEPTH_MIXERS == 1]
    last_ctx = ssd_layers[-1] if ssd_layers else -1

    hl = _prenorm(xl, norms(0, 0)[0])
    hc = _prenorm(xc, norms(0, 1)[0]) if last_ctx >= 0 else None
    for i in range(depth):
        kind, k = i % DEPTH_MIXERS, i // DEPTH_MIXERS
        ctx_full = i < last_ctx
        _, norm2, g1, g2 = norms(i, 0)
        if i <= last_ctx:
            _, cnorm2, cg1, cg2 = norms(i, 1)
        tag = "l%d" % i

        if kind == 0:
            w1 = conv_w1[k].astype(BF16)
            w2 = conv_w2[k].astype(BF16)
            gl = _up(hl, w1, d, [0, d], conv_b1[k], _epi_glu, F32, tag + "_conv_up")
            if k % 2 == 0:
                gl = _rowconv(gl, conv_dw[k], conv_dw_b[k], seg=GRID_W, tile_rows=2 * GRID_W, name=tag + "_rowconv")
            else:
                gl = _colconv(gl.reshape(grid_rows, GRID_W, d), conv_dw[k], conv_dw_b[k],
                              tag + "_colconv").reshape(seq, d)
            al = _ln_silu(gl, conv_ln_g[k], conv_ln_b[k])
            xl = _down_mm(al, w2, conv_b2[k], xl, g1, tag + "_conv_down")
            if ctx_full:
                gc = _up(hc, w1, d, [0, d], conv_b1[k], _epi_glu, F32, tag + "_conv_up_ctx")
                gc = _rowconv(gc, conv_dw[k], conv_dw_b[k], seg=gc.shape[0], tile_rows=gc.shape[0],
                              name=tag + "_seqconv_ctx")
                ac = _ln_silu(gc, conv_ln_g[k], conv_ln_b[k])
                xc = _down_mm(ac, w2, conv_b2[k], xc, cg1, tag + "_conv_down_ctx")
        elif kind == 1:
            assert not ctx_full
            w_in = ssd_w_in[k].astype(BF16)
            w_out = ssd_w_out[k].astype(BF16)
            d_inner = w_out.shape[0]
            heads2 = 2 * ssd_a_log.shape[2]
            conv_dim = w_in.shape[1] - d_inner - heads2
            hpg = heads2 // 2 // SSD_GROUPS
            al_a = ssd_a_log[k].reshape(2, SSD_GROUPS, 1, hpg)
            al_b = ssd_a_log[k].reshape(2, SSD_GROUPS, hpg, 1)
            dskip = ssd_d[k].reshape(SSD_GROUPS, 1, hpg)
            zero_state = jnp.zeros((SSD_GROUPS, hpg // 2, SSD_STATE, 2 * SSD_HEADDIM), F32)
            xbc_c, dta_c, dtb_c = _ssd_inputs(hc, w_in, ssd_conv_w[k], ssd_conv_b[k], ssd_dt_bias[k],
                                              d_inner, conv_dim, heads2, tag + "_ssd_ctx")
            (state_f,) = _ssd_scan(xbc_c, dta_c, dtb_c, al_a, al_b, zero_state, 0, "state")
            (state_b,) = _ssd_scan(xbc_c, dta_c, dtb_c, al_a, al_b, zero_state, 1, "state")
            xbc_l, dta_l, dtb_l = _ssd_inputs(hl, w_in, ssd_conv_w[k], ssd_conv_b[k], ssd_dt_bias[k],
                                              d_inner, conv_dim, heads2, tag + "_ssd")
            z = _up(hl, w_in, d_inner, [0], None, _epi_id, BF16, tag + "_ssd_z")
            y_f, _ = _ssd_scan(xbc_l, dta_l, dtb_l, al_a, al_b, state_f, 0, "y")
            p, _ = _ssd_scan(xbc_l, dta_l, dtb_l, al_a, al_b, state_b, 1, "gated",
                             (y_f, dskip, z, ssd_norm_g[k]))
            xl = _down_mm(p, w_out, None, xl, g1, tag + "_ssd_down")
        else:
            w1 = sgu_w1[k].astype(BF16)
            w2 = sgu_w2[k].astype(BF16)

            def sgu(h, name):
                a = _up(h, w1, w1.shape[1], [0], sgu_b1[k], _epi_gelu, BF16, name + "_up")
                return _sgu_mix(a, sgu_ln_g[k], sgu_ln_b[k], sgu_ws[k], sgu_bs[k])

            xl = _down_mm(sgu(hl, tag + "_sgu"), w2, sgu_b2[k], xl, g1, tag + "_sgu_down")
            if ctx_full:
                xc = _down_mm(sgu(hc, tag + "_sgu_ctx"), w2, sgu_b2[k], xc, cg1, tag + "_sgu_down_ctx")

        fw_in = ffn_w_in[i].astype(BF16)
        fw_out = ffn_w_out[i].astype(BF16)
        final = i == depth - 1
        a = _up(_prenorm(xl, norm2), fw_in, 2 * f_hidden, [0], None, _epi_id, BF16, tag + "_ffn_up")
        res = _down_ffn(a, ffn_dw[i], ffn_dw_b[i], fw_out, xl, g2, next_norm(i, 0), True, final, tag + "_ffn_down")
        if final:
            xl = res
        else:
            xl, hl = res
        if ctx_full:
            ac = _up(_prenorm(xc, cnorm2), fw_in, 2 * f_hidden, [0], None, _epi_id, BF16, tag + "_ffn_up_ctx")
            xc, hc = _down_ffn(ac, ffn_dw[i], ffn_dw_b[i], fw_out, xc, cg2, next_norm(i, 1), False, False,
                               tag + "_ffn_down_ctx")
    return xl.reshape(bsz, seq, d)
```

```python
import functools

import jax
import jax.numpy as jnp
from jax import lax
from jax.experimental import pallas as pl
from jax.experimental.pallas import tpu as pltpu

F32 = jnp.float32
BF16 = jnp.bfloat16
HI = lax.Precision.HIGHEST

NORM_EPS = 1e-6
GRID_W = 64
DEPTH_MIXERS = 3
SSD_GROUPS = 8
SSD_HEADDIM = 64
SSD_STATE = 128
SSD_CHUNK = 128
SSD_BLOCK = 4
SGU_CHUNK = 128
SGU_GROUPS = 16
LANE = 128
SUBLANE = 8
CONV_LANES = 256
CONV_ROWS = 64
ROW_CHUNK = 64
VMEM_LIMIT = 58 * 1024 * 1024


def _cp(*sem):
    return pltpu.CompilerParams(dimension_semantics=sem, vmem_limit_bytes=VMEM_LIMIT)


def _tile(n, pref, align):
    if n <= pref:
        return n
    t = (pref // align) * align
    while t >= align:
        if n % t == 0:
            return t
        t -= align
    return n


def _row(v):
    return v.reshape(1, -1)


def _prenorm_math(x, g, sc, sh):
    ms = jnp.mean(x * x, axis=-1, keepdims=True)
    return (x * lax.rsqrt(ms + NORM_EPS)) * g * (1.0 + sc) + sh


def _silu(x):
    return x * jax.nn.sigmoid(x)


def _row_loop(nrows, fn):
    step = min(nrows, ROW_CHUNK)

    def body(r, carry):
        fn(pl.ds(pl.multiple_of(r * step, step), step))
        return carry

    lax.fori_loop(0, nrows // step, body, 0)


def _adaln_kernel(v_ref, wa_ref, wb_ref, b_ref, o_ref, t_ref):
    @pl.when(pl.program_id(1) == 0)
    def _():
        t_ref[...] = jnp.dot(_silu(v_ref[...]), wa_ref[0], preferred_element_type=F32, precision=HI)

    o_ref[0] = jnp.dot(t_ref[...], wb_ref[0], preferred_element_type=F32, precision=HI) + b_ref[0]


def _adaln(v8, wa, wb, b):
    depth, d, r = wa.shape
    n = wb.shape[2]
    tn = _tile(n, 6144, LANE)
    return pl.pallas_call(
        _adaln_kernel,
        grid=(depth, n // tn),
        in_specs=[pl.BlockSpec((SUBLANE, d), lambda l, j: (0, 0)),
                  pl.BlockSpec((1, d, r), lambda l, j: (l, 0, 0)),
                  pl.BlockSpec((1, r, tn), lambda l, j: (l, 0, j)),
                  pl.BlockSpec((1, 1, tn), lambda l, j: (l, 0, j))],
        out_specs=pl.BlockSpec((1, SUBLANE, tn), lambda l, j: (l, 0, j)),
        out_shape=jax.ShapeDtypeStruct((depth, SUBLANE, n), F32),
        scratch_shapes=[pltpu.VMEM((SUBLANE, r), F32)],
        compiler_params=_cp("arbitrary", "arbitrary"),
        name="adaln",
    )(v8, wa, wb, b.reshape(depth, 1, n))


def _prenorm_kernel(x_ref, g_ref, sc_ref, sh_ref, o_ref):
    o_ref[...] = _prenorm_math(x_ref[...], g_ref[...], sc_ref[...], sh_ref[...]).astype(o_ref.dtype)


def _prenorm(x, norm):
    g, sc, sh = norm
    m, d = x.shape
    tm = _tile(m, 256, SUBLANE)
    vec = pl.BlockSpec((1, d), lambda i: (0, 0))
    return pl.pallas_call(
        _prenorm_kernel,
        grid=(m // tm,),
        in_specs=[pl.BlockSpec((tm, d), lambda i: (i, 0)), vec, vec, vec],
        out_specs=pl.BlockSpec((tm, d), lambda i: (i, 0)),
        out_shape=jax.ShapeDtypeStruct((m, d), BF16),
        compiler_params=_cp("arbitrary"),
        name="prenorm",
    )(x, _row(g), _row(sc), _row(sh))


def _ln_silu_kernel(x_ref, g_ref, b_ref, o_ref):
    x = x_ref[...]
    mu = jnp.mean(x, axis=-1, keepdims=True)
    xc = x - mu
    var = jnp.mean(xc * xc, axis=-1, keepdims=True)
    y = xc * lax.rsqrt(var + NORM_EPS) * g_ref[...] + b_ref[...]
    o_ref[...] = _silu(y).astype(o_ref.dtype)


def _ln_silu(x, g, b):
    m, d = x.shape
    tm = _tile(m, 256, SUBLANE)
    vec = pl.BlockSpec((1, d), lambda i: (0, 0))
    return pl.pallas_call(
        _ln_silu_kernel,
        grid=(m // tm,),
        in_specs=[pl.BlockSpec((tm, d), lambda i: (i, 0)), vec, vec],
        out_specs=pl.BlockSpec((tm, d), lambda i: (i, 0)),
        out_shape=jax.ShapeDtypeStruct((m, d), BF16),
        compiler_params=_cp("arbitrary"),
        name="ln_silu",
    )(x, _row(g), _row(b))


def _up_kernel(*refs, n_w, has_bias, epi, sub):
    h_ref = refs[0]
    w_refs = refs[1:1 + n_w]
    b_refs = refs[1 + n_w:1 + 2 * n_w] if has_bias else ()
    o_ref = refs[-1]
    h = h_ref[...]
    for n0 in range(0, o_ref.shape[1], sub):
        accs = []
        for t in range(n_w):
            a = jnp.dot(h, w_refs[t][:, n0:n0 + sub], preferred_element_type=F32)
            if has_bias:
                a = a + b_refs[t][:, n0:n0 + sub]
            accs.append(a)
        o_ref[:, n0:n0 + sub] = epi(*accs).astype(o_ref.dtype)


def _epi_id(a):
    return a


def _epi_glu(a, b):
    return a * jax.nn.sigmoid(b)


def _epi_gelu(a):
    return jax.nn.gelu(a)


def _epi_softplus(a):
    return jnp.maximum(a, 0.0) + jnp.log1p(jnp.exp(-jnp.abs(a)))


def _up(h, w, n_out, col_offs, bias, epi, out_dtype, name, tn_pref=1024):
    m, k = h.shape
    tm = _tile(m, 1024, SUBLANE)
    tn = _tile(n_out, tn_pref // len(col_offs), LANE)
    for off in col_offs:
        assert off % tn == 0
    n_w = len(col_offs)
    in_specs = [pl.BlockSpec((tm, k), lambda i, j: (i, 0))]
    args = [h]
    for off in col_offs:
        in_specs.append(pl.BlockSpec((k, tn), functools.partial(lambda i, j, o: (0, j + o), o=off // tn)))
        args.append(w)
    if bias is not None:
        for off in col_offs:
            in_specs.append(pl.BlockSpec((1, tn), functools.partial(lambda i, j, o: (0, j + o), o=off // tn)))
            args.append(_row(bias))
    return pl.pallas_call(
        functools.partial(_up_kernel, n_w=n_w, has_bias=bias is not None, epi=epi,
                          sub=256 if tn % 256 == 0 else LANE),
        grid=(m // tm, n_out // tn),
        in_specs=in_specs,
        out_specs=pl.BlockSpec((tm, tn), lambda i, j: (i, j)),
        out_shape=jax.ShapeDtypeStruct((m, n_out), out_dtype),
        compiler_params=_cp("arbitrary", "arbitrary"),
        name=name,
    )(*args)


def _up_conv_kernel(h_ref, hp_ref, hn_ref, w_ref, cw_ref, cb_ref, o_ref, e0_ref, e1_ref, *, ntaps, sub):
    tm = h_ref.shape[0]
    pad = hp_ref.shape[0]
    half = ntaps // 2
    i = pl.program_id(0)
    first = i == 0
    last = i == pl.num_programs(0) - 1
    h, hp, hn = h_ref[...], hp_ref[...], hn_ref[...]
    rs = min(tm, CONV_ROWS)
    starts = list(range(0, o_ref.shape[1], sub))
    exts = (e0_ref, e1_ref)

    def project(c):
        ext_ref, wsub = exts[c % 2], w_ref[:, starts[c]:starts[c] + sub]
        ext_ref[pl.ds(0, pad), :] = jnp.where(first, 0.0, jnp.dot(hp, wsub, preferred_element_type=F32))
        ext_ref[pl.ds(pad, tm), :] = jnp.dot(h, wsub, preferred_element_type=F32)
        ext_ref[pl.ds(pad + tm, pad), :] = jnp.where(last, 0.0, jnp.dot(hn, wsub, preferred_element_type=F32))

    def conv(c):
        ext_ref, n0 = exts[c % 2], starts[c]
        bias = cb_ref[:, n0:n0 + sub]
        for r0 in range(0, tm, rs):
            acc = None
            for k in range(ntaps):
                v = ext_ref[pl.ds(pad + r0 + k - half, rs), :] * cw_ref[pl.ds(k, 1), n0:n0 + sub]
                acc = v if acc is None else acc + v
            o_ref[pl.ds(r0, rs), n0:n0 + sub] = _silu(acc + bias)

    project(0)
    for c in range(len(starts)):
        if c + 1 < len(starts):
            project(c + 1)
        conv(c)


def _up_conv(h, w, n_out, col_off, conv_w, conv_b, name):
    m, k = h.shape
    ntaps = conv_w.shape[0]
    tm = _tile(m, 1024, 2 * SUBLANE)
    tn = _tile(n_out, 1024, LANE)
    assert col_off % tn == 0
    pad = 2 * SUBLANE
    ratio = tm // pad
    nhb = m // pad
    sub = 256 if tn % 256 == 0 else LANE
    off = col_off // tn
    return pl.pallas_call(
        functools.partial(_up_conv_kernel, ntaps=ntaps, sub=sub),
        grid=(m // tm, n_out // tn),
        in_specs=[pl.BlockSpec((tm, k), lambda i, j: (i, 0)),
                  pl.BlockSpec((pad, k), lambda i, j: (jnp.maximum(i * ratio - 1, 0), 0)),
                  pl.BlockSpec((pad, k), lambda i, j: (jnp.minimum((i + 1) * ratio, nhb - 1), 0)),
                  pl.BlockSpec((k, tn), lambda i, j: (0, j + off)),
                  pl.BlockSpec((ntaps, tn), lambda i, j: (0, j)),
                  pl.BlockSpec((1, tn), lambda i, j: (0, j))],
        out_specs=pl.BlockSpec((tm, tn), lambda i, j: (i, j)),
        out_shape=jax.ShapeDtypeStruct((m, n_out), F32),
        scratch_shapes=[pltpu.VMEM((tm + 2 * pad, sub), F32), pltpu.VMEM((tm + 2 * pad, sub), F32)],
        compiler_params=_cp("arbitrary", "arbitrary"),
        name=name,
    )(h, h, h, w, conv_w, _row(conv_b))


def _seqconv_kernel(x_ref, xp_ref, xn_ref, w_ref, b_ref, o_ref, buf_ref, *, ntaps, act):
    t_rows, c = x_ref.shape
    pad = xp_ref.shape[0]
    i = pl.program_id(0)
    buf_ref[pl.ds(0, pad), :] = jnp.where(i > 0, xp_ref[...].astype(F32), 0.0)
    buf_ref[pl.ds(pad, t_rows), :] = x_ref[...].astype(F32)
    buf_ref[pl.ds(pad + t_rows, pad), :] = jnp.where(i < pl.num_programs(0) - 1, xn_ref[...].astype(F32), 0.0)
    half = ntaps // 2
    sub = min(t_rows, CONV_ROWS)
    lw = min(c, CONV_LANES)

    def body(ci, carry):
        lanes = pl.ds(pl.multiple_of(ci * lw, lw), lw)
        bias = b_ref[:, lanes]
        for s in range(t_rows // sub):
            acc = None
            for k in range(ntaps):
                v = buf_ref[pl.ds(pad + s * sub + k - half, sub), lanes] * w_ref[pl.ds(k, 1), lanes]
                acc = v if acc is None else acc + v
            acc = acc + bias
            if act == "silu":
                acc = _silu(acc)
            o_ref[pl.ds(s * sub, sub), lanes] = acc
        return carry

    lax.fori_loop(0, c // lw, body, 0)


def _seqconv(x, w, b, *, tile_rows, act, name):
    rows, c = x.shape
    ntaps = w.shape[0]
    pad = SUBLANE * (4 // x.dtype.itemsize)
    assert ntaps // 2 <= pad and rows % tile_rows == 0
    ratio = tile_rows // pad
    nhb = rows // pad
    return pl.pallas_call(
        functools.partial(_seqconv_kernel, ntaps=ntaps, act=act),
        grid=(rows // tile_rows,),
        in_specs=[pl.BlockSpec((tile_rows, c), lambda i: (i, 0)),
                  pl.BlockSpec((pad, c), lambda i: (jnp.maximum(i * ratio - 1, 0), 0)),
                  pl.BlockSpec((pad, c), lambda i: (jnp.minimum((i + 1) * ratio, nhb - 1), 0)),
                  pl.BlockSpec((ntaps, c), lambda i: (0, 0)),
                  pl.BlockSpec((1, c), lambda i: (0, 0))],
        out_specs=pl.BlockSpec((tile_rows, c), lambda i: (i, 0)),
        out_shape=jax.ShapeDtypeStruct((rows, c), F32),
        scratch_shapes=[pltpu.VMEM((tile_rows + 2 * pad, c), F32)],
        compiler_params=_cp("arbitrary"),
        name=name,
    )(x, x, x, w, _row(b))


def _rowconv_kernel(x_ref, w_ref, b_ref, g_ref, beta_ref, o_ref, pad_ref, ph_ref, y_ref, *, ntaps, seg):
    t_rows, c = x_ref.shape
    nseg = t_rows // seg
    lead = 2 * SUBLANE
    half = ntaps // 2
    amax = (lead - half + ntaps - 1) // SUBLANE
    span = seg + SUBLANE * amax
    stride = span + SUBLANE
    lw = ph_ref.shape[2]
    sub = min(seg, CONV_ROWS)
    for r in range(nseg):
        pad_ref[pl.ds(r * stride, lead), :] = jnp.zeros((lead, c), F32)
        pad_ref[pl.ds(r * stride + lead, seg), :] = x_ref[pl.ds(r * seg, seg), :]
        pad_ref[pl.ds(r * stride + lead + seg, stride - lead - seg), :] = jnp.zeros((stride - lead - seg, c), F32)

    def body(ci, carry):
        lanes = pl.ds(pl.multiple_of(ci * lw, lw), lw)
        bias = b_ref[:, lanes]
        for r in range(nseg):
            for b in range(1, SUBLANE):
                ph_ref[b] = pad_ref[pl.ds(r * stride + b, span), lanes]
            for s in range(0, seg, sub):
                acc = None
                for k in range(ntaps):
                    off = lead - half + k
                    a, b = off // SUBLANE, off % SUBLANE
                    if b == 0:
                        src = pad_ref[pl.ds(r * stride + SUBLANE * a + s, sub), lanes]
                    else:
                        src = ph_ref[b, pl.ds(SUBLANE * a + s, sub), :]
                    v = src * w_ref[pl.ds(k, 1), lanes]
                    acc = v if acc is None else acc + v
                y_ref[pl.ds(r * seg + s, sub), lanes] = acc + bias
        return carry

    lax.fori_loop(0, c // lw, body, 0)

    def norm_rows(rows):
        y = y_ref[rows, :]
        mu = jnp.mean(y, axis=-1, keepdims=True)
        yc = y - mu
        var = jnp.mean(yc * yc, axis=-1, keepdims=True)
        o_ref[rows, :] = _silu(yc * lax.rsqrt(var + NORM_EPS) * g_ref[...] + beta_ref[...]).astype(o_ref.dtype)

    _row_loop(t_rows, norm_rows)


def _rowconv(x, w, b, ln_g, ln_b, *, seg, tile_rows, name):
    rows, c = x.shape
    ntaps = w.shape[0]
    lead = 2 * SUBLANE
    assert ntaps // 2 <= lead and seg % SUBLANE == 0 and tile_rows % seg == 0
    amax = (lead - ntaps // 2 + ntaps - 1) // SUBLANE
    span = seg + SUBLANE * amax
    stride = span + SUBLANE
    lw = min(c, CONV_LANES)
    return pl.pallas_call(
        functools.partial(_rowconv_kernel, ntaps=ntaps, seg=seg),
        grid=(rows // tile_rows,),
        in_specs=[pl.BlockSpec((tile_rows, c), lambda i: (i, 0)),
                  pl.BlockSpec((ntaps, c), lambda i: (0, 0)),
                  pl.BlockSpec((1, c), lambda i: (0, 0)),
                  pl.BlockSpec((1, c), lambda i: (0, 0)),
                  pl.BlockSpec((1, c), lambda i: (0, 0))],
        out_specs=pl.BlockSpec((tile_rows, c), lambda i: (i, 0)),
        out_shape=jax.ShapeDtypeStruct((rows, c), BF16),
        scratch_shapes=[pltpu.VMEM(((tile_rows // seg) * stride, c), F32),
                        pltpu.VMEM((SUBLANE, span, lw), F32),
                        pltpu.VMEM((tile_rows, c), F32)],
        compiler_params=_cp("arbitrary"),
        name=name,
    )(x, w, _row(b), _row(ln_g), _row(ln_b))


def _colconv_kernel(x_ref, w_ref, b_ref, o_ref, buf_ref, *, ntaps):
    rows, width, lw = x_ref.shape
    half = ntaps // 2
    lead = buf_ref.shape[0] - rows - half
    zeros = jnp.zeros((width, lw), F32)
    for r in range(lead):
        buf_ref[r] = zeros
    for r in range(half):
        buf_ref[lead + rows + r] = zeros

    def copy(r, carry):
        buf_ref[lead + r] = x_ref[r]
        return carry

    lax.fori_loop(0, rows, copy, 0)

    def body(r, carry):
        acc = None
        for k in range(ntaps):
            v = buf_ref[lead - half + r + k] * w_ref[pl.ds(k, 1), :]
            acc = v if acc is None else acc + v
        o_ref[r] = acc + b_ref[...]
        return carry

    lax.fori_loop(0, rows, body, 0)


def _colconv(x3, w, b, name):
    rows, width, c = x3.shape
    ntaps = w.shape[0]
    half = ntaps // 2
    lw = min(c, LANE)
    return pl.pallas_call(
        functools.partial(_colconv_kernel, ntaps=ntaps),
        grid=(c // lw,),
        in_specs=[pl.BlockSpec((rows, width, lw), lambda j: (0, 0, j)),
                  pl.BlockSpec((ntaps, lw), lambda j: (0, j)),
                  pl.BlockSpec((1, lw), lambda j: (0, j))],
        out_specs=pl.BlockSpec((rows, width, lw), lambda j: (0, 0, j)),
        out_shape=jax.ShapeDtypeStruct((rows, width, c), F32),
        scratch_shapes=[pltpu.VMEM((rows + 2 * half, width, lw), F32)],
        compiler_params=_cp("arbitrary"),
        name=name,
    )(x3, w, _row(b))


def _down_mm_kernel(*refs, has_bias):
    if has_bias:
        a_ref, w_ref, bias_ref, x_ref, gate_ref, o_ref = refs
    else:
        a_ref, w_ref, x_ref, gate_ref, o_ref = refs
    k = pl.program_id(2)
    part = jnp.dot(a_ref[...], w_ref[...], preferred_element_type=F32)

    @pl.when(k == 0)
    def _():
        o_ref[...] = part

    @pl.when(k > 0)
    def _():
        o_ref[...] += part

    @pl.when(k == pl.num_programs(2) - 1)
    def _():
        y = o_ref[...]
        if has_bias:
            y = y + bias_ref[...]
        o_ref[...] = x_ref[...] + gate_ref[...] * y


def _down_mm(a, w, bias, x, gate, name):
    m, kdim = a.shape
    d = w.shape[1]
    tm = _tile(m, 1024, SUBLANE)
    tn = _tile(d, 1024, LANE)
    tk = _tile(kdim, 2048, LANE)
    vec = pl.BlockSpec((1, tn), lambda i, j, k: (0, j))
    in_specs = [pl.BlockSpec((tm, tk), lambda i, j, k: (i, k)),
                pl.BlockSpec((tk, tn), lambda i, j, k: (k, j))]
    args = [a, w]
    if bias is not None:
        in_specs.append(vec)
        args.append(_row(bias))
    in_specs += [pl.BlockSpec((tm, tn), lambda i, j, k: (i, j)), vec]
    args += [x, _row(gate)]
    return pl.pallas_call(
        functools.partial(_down_mm_kernel, has_bias=bias is not None),
        grid=(m // tm, d // tn, kdim // tk),
        in_specs=in_specs,
        out_specs=pl.BlockSpec((tm, tn), lambda i, j, k: (i, j)),
        out_shape=jax.ShapeDtypeStruct((m, d), F32),
        compiler_params=_cp("arbitrary", "arbitrary", "arbitrary"),
        name=name,
    )(*args)


def _ffn_down_kernel(gc_ref, gp_ref, gn_ref, up_ref, dw_ref, dwb_ref, w_ref, x_ref, gate_ref, ng_ref, sc_ref,
                     sh_ref, *rest, on_grid, final, n_i, nk):
    if final:
        o_ref, buf_ref, a0_ref, a1_ref = rest
        h_ref = None
    else:
        o_ref, h_ref, buf_ref, a0_ref, a1_ref = rest
    tm, tk = gc_ref.shape
    hb = gp_ref.shape[0]
    d = o_ref.shape[1]
    s = pl.program_id(0)
    ip = jnp.minimum(s, n_i * nk - 1) // nk
    m = jnp.maximum(s - 1, 0)
    km = lax.rem(m, nk)

    @pl.when(s == 0)
    def _():
        a1_ref[...] = jnp.zeros_like(a1_ref)

    @pl.when(km == 0)
    def _():
        def zero(rows):
            o_ref[rows, :] = jnp.zeros((min(tm, ROW_CHUNK), d), F32)

        _row_loop(tm, zero)

    sub = min(tm, CONV_ROWS)
    lw = min(tk, CONV_LANES)
    dys = (-1, 0, 1) if on_grid else (0,)
    tn = min(d, 256)

    def step(dst_ref, src_ref):
        buf_ref[pl.ds(0, hb), :] = jnp.where(ip > 0, gp_ref[...].astype(F32), 0.0)
        buf_ref[pl.ds(hb, tm), :] = gc_ref[...].astype(F32)
        buf_ref[pl.ds(hb + tm, hb), :] = jnp.where(ip < n_i - 1, gn_ref[...].astype(F32), 0.0)
        col = lax.broadcasted_iota(jnp.int32, (sub, lw), 0)

        def build(lc, r):
            lanes = pl.ds(lc * lw, lw)
            base = hb + r * sub

            def tapsum(dx):
                acc = None
                for dy in dys:
                    v = (buf_ref[pl.ds(base + GRID_W * dy + dx, sub), lanes]
                         * dw_ref[pl.ds((dy + 1) * 3 + dx + 1, 1), lanes])
                    acc = v if acc is None else acc + v
                return acc

            left, right = tapsum(-1), tapsum(1)
            if on_grid:
                left = jnp.where(col >= 1, left, 0.0)
                right = jnp.where(col <= GRID_W - 2, right, 0.0)
            conv = tapsum(0) + left + right + dwb_ref[:, lanes]
            dst_ref[pl.ds(r * sub, sub), lanes] = (
                jax.nn.gelu(conv) * up_ref[pl.ds(r * sub, sub), lanes].astype(F32)).astype(BF16)

        blocks = [(lc, r) for lc in range(tk // lw) for r in range(tm // sub)]
        n_mm = d // tn
        per = -(-len(blocks) // n_mm)
        a = src_ref[...]
        for c in range(n_mm):
            n0 = c * tn
            o_ref[:, n0:n0 + tn] += jnp.dot(a, w_ref[:, n0:n0 + tn], preferred_element_type=F32)
            for lc, r in blocks[c * per:(c + 1) * per]:
                build(lc, r)

    parity = lax.rem(s, 2)

    @pl.when(parity == 0)
    def _():
        step(a0_ref, a1_ref)

    @pl.when(parity == 1)
    def _():
        step(a1_ref, a0_ref)

    @pl.when((km == nk - 1) & (s > 0))
    def _():
        def finish(rows):
            xn = x_ref[rows, :] + gate_ref[...] * o_ref[rows, :]
            if final:
                ms = jnp.mean(xn * xn, axis=-1, keepdims=True)
                o_ref[rows, :] = (xn * lax.rsqrt(ms + NORM_EPS)) * ng_ref[...]
            else:
                o_ref[rows, :] = xn
                h_ref[rows, :] = _prenorm_math(xn, ng_ref[...], sc_ref[...], sh_ref[...]).astype(BF16)

        _row_loop(tm, finish)


def _down_ffn(a, dw, dw_b, w, x, gate, norm, on_grid, final, name):
    ng, sc, sh = norm
    m = a.shape[0]
    f, d = w.shape
    tm = _tile(m, 512, 2 * GRID_W)
    tk = _tile(f, 512, LANE)
    hb = 2 * GRID_W
    ratio = tm // hb
    nhb = m // hb
    koff = f // tk
    n_i, nk = m // tm, f // tk
    last = n_i * nk - 1

    def pro(s):
        p = jnp.minimum(s, last)
        return p // nk, lax.rem(p, nk)

    def mat(s):
        q = jnp.maximum(s - 1, 0)
        return q // nk, lax.rem(q, nk)

    vec = pl.BlockSpec((1, d), lambda s: (0, 0))
    rowblk = pl.BlockSpec((tm, d), lambda s: (mat(s)[0], 0))
    in_specs = [pl.BlockSpec((tm, tk), lambda s: pro(s)),
                pl.BlockSpec((hb, tk), lambda s: (jnp.maximum(pro(s)[0] * ratio - 1, 0), pro(s)[1])),
                pl.BlockSpec((hb, tk), lambda s: (jnp.minimum((pro(s)[0] + 1) * ratio, nhb - 1), pro(s)[1])),
                pl.BlockSpec((tm, tk), lambda s: (pro(s)[0], pro(s)[1] + koff)),
                pl.BlockSpec((9, tk), lambda s: (0, pro(s)[1])),
                pl.BlockSpec((1, tk), lambda s: (0, pro(s)[1])),
                pl.BlockSpec((tk, d), lambda s: (mat(s)[1], 0)),
                rowblk, vec, vec, vec, vec]
    out_specs = [rowblk]
    out_shape = [jax.ShapeDtypeStruct((m, d), F32)]
    if not final:
        out_specs.append(rowblk)
        out_shape.append(jax.ShapeDtypeStruct((m, d), BF16))
    res = pl.pallas_call(
        functools.partial(_ffn_down_kernel, on_grid=on_grid, final=final, n_i=n_i, nk=nk),
        grid=(n_i * nk + 1,),
        in_specs=in_specs,
        out_specs=out_specs,
        out_shape=out_shape,
        scratch_shapes=[pltpu.VMEM((tm + 2 * hb, tk), F32), pltpu.VMEM((tm, tk), BF16),
                        pltpu.VMEM((tm, tk), BF16)],
        compiler_params=_cp("arbitrary"),
        name=name,
    )(a, a, a, a, dw.reshape(9, f), _row(dw_b), w, x, _row(gate), _row(ng), _row(sc), _row(sh))
    return res[0] if final else res


def _sgu_kernel(u_ref, v_ref, g_ref, b_ref, ws_ref, bs_ref, o_ref):
    v = v_ref[...].astype(F32)
    mu = jnp.mean(v, axis=-1, keepdims=True)
    vc = v - mu
    var = jnp.mean(vc * vc, axis=-1, keepdims=True)
    vn = (vc * lax.rsqrt(var + NORM_EPS) * g_ref[...] + b_ref[...]).astype(BF16)
    groups = ws_ref.shape[0]
    gw = v.shape[1] // groups
    for g in range(groups):
        mixed = jnp.dot(ws_ref[g], vn[:, g * gw:(g + 1) * gw], preferred_element_type=F32) + bs_ref[g]
        o_ref[:, g * gw:(g + 1) * gw] = (u_ref[:, g * gw:(g + 1) * gw].astype(F32) * mixed).astype(o_ref.dtype)


def _sgu_mix(a, ln_g, ln_b, ws, bs):
    m = a.shape[0]
    e = a.shape[1] // 2
    q = SGU_CHUNK
    groups = ws.shape[0]
    return pl.pallas_call(
        _sgu_kernel,
        grid=(m // q,),
        in_specs=[pl.BlockSpec((q, e), lambda c: (c, 0)),
                  pl.BlockSpec((q, e), lambda c: (c, 1)),
                  pl.BlockSpec((1, e), lambda c: (0, 0)),
                  pl.BlockSpec((1, e), lambda c: (0, 0)),
                  pl.BlockSpec((groups, q, q), lambda c: (0, 0, 0)),
                  pl.BlockSpec((groups, q, 1), lambda c: (0, 0, 0))],
        out_specs=pl.BlockSpec((q, e), lambda c: (c, 0)),
        out_shape=jax.ShapeDtypeStruct((m, e), BF16),
        compiler_params=_cp("arbitrary"),
        name="sgu_mix",
    )(a, a, _row(ln_g), _row(ln_b), ws.astype(BF16), bs.reshape(groups, q, 1))


def _split3(a):
    a1 = a.astype(BF16)
    r1 = a - a1.astype(F32)
    a2 = r1.astype(BF16)
    a3 = (r1 - a2.astype(F32)).astype(BF16)
    return [a1, a2, a3]


def _ssd_kernel(*refs, reverse, mode):
    x_ref, b_ref, c_ref, dt_ref, dtt_ref, al_ref, alt_ref, s0_ref = refs[:8]
    if mode == "gated":
        yprev_ref, d_ref, z_ref, ng_ref, y_ref, sfin_ref, st_ref, yacc_ref = refs[8:]
    elif mode == "y":
        y_ref, sfin_ref, st_ref = refs[8:]
    else:
        sfin_ref, st_ref = refs[8:]
    c = pl.program_id(1)

    @pl.when(c == 0)
    def _():
        st_ref[...] = s0_ref[0]

    q = SSD_CHUNK
    hpg = dt_ref.shape[3]
    pw = 2 * SSD_HEADDIM
    row = lax.broadcasted_iota(jnp.int32, (q, q), 0)
    col = lax.broadcasted_iota(jnp.int32, (q, q), 1)
    mask = (row <= col) if reverse else (row >= col)
    mask_t = (row >= col) if reverse else (row <= col)
    mask3 = jnp.concatenate([mask.astype(BF16)] * 3, axis=1)
    mask3_t = jnp.concatenate([mask_t.astype(BF16)] * 3, axis=0)
    lo = lax.broadcasted_iota(jnp.int32, (1, pw), 1) < SSD_HEADDIM
    neg_a = -jnp.exp(al_ref[0, 0])
    neg_a_t = -jnp.exp(alt_ref[0, 0])

    def halves(a):
        return jnp.concatenate([jnp.where(lo, a, 0.0), jnp.where(lo, 0.0, a)], axis=0).astype(BF16)

    def chunk(r0):
        rows = pl.ds(r0, q)
        dtt = dtt_ref[0, 0][:, r0:r0 + q]
        dtat = dtt * neg_a_t
        cumt = jnp.dot(jnp.concatenate(_split3(dtat), axis=1), mask3_t,
                       preferred_element_type=F32)
        lastt = jnp.sum(dtat, axis=1, keepdims=True)
        vt = dtt * jnp.exp(lastt - cumt)
        dta = dt_ref[0, 0, rows, :] * neg_a
        elast = jnp.exp(jnp.sum(dta, axis=0, keepdims=True))
        bmat = b_ref[rows, :]
        bt = bmat.T
        if mode != "state":
            cum = jnp.dot(mask3, jnp.concatenate(_split3(dta), axis=0), preferred_element_type=F32)
            cmat = c_ref[rows, :]
            cb = lax.dot_general(cmat.astype(BF16), bmat.astype(BF16), (((1,), (1,)), ((), ())),
                                 preferred_element_type=F32)

        for j in range(hpg // 2):
            h0, h1 = 2 * j, 2 * j + 1
            cols = slice(j * pw, (j + 1) * pw)
            xs = x_ref[rows, cols]
            xcat = halves(xs)
            st = st_ref[j]
            btv = jnp.concatenate([bt * vt[h0:h0 + 1, :], bt * vt[h1:h1 + 1, :]], axis=1).astype(BF16)
            st_ref[j] = (st * jnp.where(lo, elast[:, h0:h0 + 1], elast[:, h1:h1 + 1])
                         + jnp.dot(btv, xcat, preferred_element_type=F32))
            if mode == "state":
                continue

            def lhs_parts(h):
                colb = jnp.broadcast_to(cum[:, h:h + 1], (q, q))
                e = jnp.where(mask, jnp.exp(colb - cumt[h:h + 1, :]), 0.0)
                return (cb * e * dtt[h:h + 1, :]).astype(BF16), (cmat * jnp.exp(colb)).astype(BF16)

            m0, c0 = lhs_parts(h0)
            m1, c1 = lhs_parts(h1)
            lhs = jnp.concatenate([m0, m1, c0, c1], axis=1)
            rhs = jnp.concatenate([xcat, halves(st)], axis=0)
            y = jnp.dot(lhs, rhs, preferred_element_type=F32)
            if mode == "gated":
                dlane = jnp.where(lo, d_ref[0][:, h0:h0 + 1], d_ref[0][:, h1:h1 + 1])
                yacc_ref[rows, cols] = y + yprev_ref[rows, cols] + xs * dlane
            else:
                y_ref[rows, cols] = y

        if mode == "gated":
            yz = yacc_ref[rows, :] * _silu(z_ref[rows, :].astype(F32))
            ms = jnp.mean(yz * yz, axis=-1, keepdims=True)
            y_ref[rows, :] = (yz * lax.rsqrt(ms + NORM_EPS) * ng_ref[...]).astype(y_ref.dtype)

    starts = list(range(0, x_ref.shape[0], q))
    for r0 in (reversed(starts) if reverse else starts):
        chunk(r0)

    @pl.when(c == pl.num_programs(1) - 1)
    def _():
        sfin_ref[0] = st_ref[...]


def _ssd_scan(xbc, dt_a, dt_b, al_a, al_b, s0, direction, mode, extra=()):
    length = xbc.shape[0]
    groups, hpg = dt_a.shape[1], dt_a.shape[3]
    q = SSD_CHUNK * min(SSD_BLOCK, length // SSD_CHUNK)
    n = SSD_STATE
    pw = 2 * SSD_HEADDIM
    gw = hpg * SSD_HEADDIM
    d_inner = groups * gw
    assert length % q == 0
    nc = length // q
    reverse = direction == 1
    boff = d_inner // n

    def cc(c):
        return nc - 1 - c if reverse else c

    xblk = pl.BlockSpec((q, gw), lambda g, c: (cc(c), g))
    stblk = pl.BlockSpec((1, hpg // 2, n, pw), lambda g, c: (g, 0, 0, 0))
    in_specs = [xblk,
                pl.BlockSpec((q, n), lambda g, c: (cc(c), boff + g)),
                pl.BlockSpec((q, n), lambda g, c: (cc(c), boff + groups + g)),
                pl.BlockSpec((1, 1, q, hpg), lambda g, c: (direction, g, cc(c), 0)),
                pl.BlockSpec((1, 1, hpg, q), lambda g, c: (direction, g, 0, cc(c))),
                pl.BlockSpec((1, 1, 1, hpg), lambda g, c: (direction, g, 0, 0)),
                pl.BlockSpec((1, 1, hpg, 1), lambda g, c: (direction, g, 0, 0)),
                stblk]
    args = [xbc, xbc, xbc, dt_a, dt_b, al_a, al_b, s0]
    out_specs, out_shape = [], []
    scratch = [pltpu.VMEM((hpg // 2, n, pw), F32)]
    if mode == "gated":
        yprev, dskip, z, norm_g = extra
        in_specs += [xblk, pl.BlockSpec((1, 1, hpg), lambda g, c: (g, 0, 0)), xblk,
                     pl.BlockSpec((1, gw), lambda g, c: (0, g))]
        args += [yprev, dskip, z, _row(norm_g)]
        scratch.append(pltpu.VMEM((q, gw), F32))
    if mode != "state":
        out_specs.append(xblk)
        out_shape.append(jax.ShapeDtypeStruct((length, d_inner), BF16 if mode == "gated" else F32))
    out_specs.append(stblk)
    out_shape.append(jax.ShapeDtypeStruct(s0.shape, F32))
    return pl.pallas_call(
        functools.partial(_ssd_kernel, reverse=reverse, mode=mode),
        grid=(groups, nc),
        in_specs=in_specs,
        out_specs=out_specs,
        out_shape=out_shape,
        scratch_shapes=scratch,
        compiler_params=_cp("arbitrary", "arbitrary"),
        name="ssd_scan_%s_%s" % ("bwd" if reverse else "fwd", mode),
    )(*args)


def _ssd_inputs(h, w_bf, conv_w, conv_b, dt_bias, d_inner, conv_dim, heads2, name):
    length = h.shape[0]
    xbc = _up_conv(h, w_bf, conv_dim, d_inner, conv_w, conv_b, name + "_xbc")
    dt = _up(h, w_bf[:, d_inner + conv_dim:], heads2, [0], dt_bias.reshape(-1), _epi_softplus, F32,
             name + "_dt", tn_pref=heads2)
    hpg = heads2 // 2 // SSD_GROUPS
    dt4 = dt.reshape(length, 2, SSD_GROUPS, hpg)
    return xbc, jnp.transpose(dt4, (1, 2, 0, 3)), jnp.transpose(dt4, (1, 2, 3, 0))


def kernel(x, c, ctx, c_ctx, norm_mix_g, norm_ffn_g, mod_wa, mod_wb, mod_b, conv_w1, conv_b1, conv_dw, conv_dw_b, conv_ln_g, conv_ln_b, conv_w2, conv_b2, ssd_w_in, ssd_conv_w, ssd_conv_b, ssd_a_log, ssd_dt_bias, ssd_d, ssd_norm_g, ssd_w_out, sgu_w1, sgu_b1, sgu_ln_g, sgu_ln_b, sgu_ws, sgu_bs, sgu_w2, sgu_b2, ffn_w_in, ffn_dw, ffn_dw_b, ffn_w_out, final_g):
    bsz, seq, d = x.shape
    assert bsz == 1
    depth = norm_mix_g.shape[0]
    xl = x.reshape(seq, d)
    xc = ctx.reshape(ctx.shape[1], d)
    grid_rows = seq // GRID_W
    f_hidden = ffn_w_out.shape[1]

    v8 = jnp.zeros((SUBLANE, d), F32).at[0].set(c[0]).at[1].set(c_ctx)
    mod = _adaln(v8, mod_wa, mod_wb, mod_b)

    def norms(layer, row):
        sh1, sc1, g1, sh2, sc2, g2 = [mod[layer, row, t * d:(t + 1) * d] for t in range(6)]
        return (norm_mix_g[layer], sc1, sh1), (norm_ffn_g[layer], sc2, sh2), g1, g2

    def next_norm(layer, row):
        if layer + 1 < depth:
            return norms(layer + 1, row)[0]
        zeros = jnp.zeros((d,), F32)
        return final_g, zeros, zeros

    ssd_layers = [i for i in range(depth) if i % DEPTH_MIXERS == 1]
    last_ctx = ssd_layers[-1] if ssd_layers else -1

    hl = _prenorm(xl, norms(0, 0)[0])
    hc = _prenorm(xc, norms(0, 1)[0]) if last_ctx >= 0 else None
    for i in range(depth):
        kind, k = i % DEPTH_MIXERS, i // DEPTH_MIXERS
        ctx_full = i < last_ctx
        _, norm2, g1, g2 = norms(i, 0)
        if i <= last_ctx:
            _, cnorm2, cg1, cg2 = norms(i, 1)
        tag = "l%d" % i

        if kind == 0:
            w1 = conv_w1[k].astype(BF16)
            w2 = conv_w2[k].astype(BF16)
            gl = _up(hl, w1, d, [0, d], conv_b1[k], _epi_glu, F32, tag + "_conv_up")
            if k % 2 == 0:
                al = _rowconv(gl, conv_dw[k], conv_dw_b[k], conv_ln_g[k], conv_ln_b[k], seg=GRID_W,
                              tile_rows=2 * GRID_W, name=tag + "_rowconv")
            else:
                gl = _colconv(gl.reshape(grid_rows, GRID_W, d), conv_dw[k], conv_dw_b[k],
                              tag + "_colconv").reshape(seq, d)
                al = _ln_silu(gl, conv_ln_g[k], conv_ln_b[k])
            xl = _down_mm(al, w2, conv_b2[k], xl, g1, tag + "_conv_down")
            if ctx_full:
                gc = _up(hc, w1, d, [0, d], conv_b1[k], _epi_glu, F32, tag + "_conv_up_ctx")
                ac = _rowconv(gc, conv_dw[k], conv_dw_b[k], conv_ln_g[k], conv_ln_b[k], seg=gc.shape[0],
                              tile_rows=gc.shape[0], name=tag + "_seqconv_ctx")
                xc = _down_mm(ac, w2, conv_b2[k], xc, cg1, tag + "_conv_down_ctx")
        elif kind == 1:
            assert not ctx_full
            w_in = ssd_w_in[k].astype(BF16)
            w_out = ssd_w_out[k].astype(BF16)
            d_inner = w_out.shape[0]
            heads2 = 2 * ssd_a_log.shape[2]
            conv_dim = w_in.shape[1] - d_inner - heads2
            hpg = heads2 // 2 // SSD_GROUPS
            al_a = ssd_a_log[k].reshape(2, SSD_GROUPS, 1, hpg)
            al_b = ssd_a_log[k].reshape(2, SSD_GROUPS, hpg, 1)
            dskip = ssd_d[k].reshape(SSD_GROUPS, 1, hpg)
            zero_state = jnp.zeros((SSD_GROUPS, hpg // 2, SSD_STATE, 2 * SSD_HEADDIM), F32)
            xbc_c, dta_c, dtb_c = _ssd_inputs(hc, w_in, ssd_conv_w[k], ssd_conv_b[k], ssd_dt_bias[k],
                                              d_inner, conv_dim, heads2, tag + "_ssd_ctx")
            (state_f,) = _ssd_scan(xbc_c, dta_c, dtb_c, al_a, al_b, zero_state, 0, "state")
            (state_b,) = _ssd_scan(xbc_c, dta_c, dtb_c, al_a, al_b, zero_state, 1, "state")
            xbc_l, dta_l, dtb_l = _ssd_inputs(hl, w_in, ssd_conv_w[k], ssd_conv_b[k], ssd_dt_bias[k],
                                              d_inner, conv_dim, heads2, tag + "_ssd")
            z = _up(hl, w_in, d_inner, [0], None, _epi_id, BF16, tag + "_ssd_z")
            y_f, _ = _ssd_scan(xbc_l, dta_l, dtb_l, al_a, al_b, state_f, 0, "y")
            p, _ = _ssd_scan(xbc_l, dta_l, dtb_l, al_a, al_b, state_b, 1, "gated",
                             (y_f, dskip, z, ssd_norm_g[k]))
            xl = _down_mm(p, w_out, None, xl, g1, tag + "_ssd_down")
        else:
            w1 = sgu_w1[k].astype(BF16)
            w2 = sgu_w2[k].astype(BF16)

            def sgu(h, name):
                a = _up(h, w1, w1.shape[1], [0], sgu_b1[k], _epi_gelu, BF16, name + "_up")
                return _sgu_mix(a, sgu_ln_g[k], sgu_ln_b[k], sgu_ws[k], sgu_bs[k])

            xl = _down_mm(sgu(hl, tag + "_sgu"), w2, sgu_b2[k], xl, g1, tag + "_sgu_down")
            if ctx_full:
                xc = _down_mm(sgu(hc, tag + "_sgu_ctx"), w2, sgu_b2[k], xc, cg1, tag + "_sgu_down_ctx")

        fw_in = ffn_w_in[i].astype(BF16)
        fw_out = ffn_w_out[i].astype(BF16)
        final = i == depth - 1
        a = _up(_prenorm(xl, norm2), fw_in, 2 * f_hidden, [0], None, _epi_id, BF16, tag + "_ffn_up")
        res = _down_ffn(a, ffn_dw[i], ffn_dw_b[i], fw_out, xl, g2, next_norm(i, 0), True, final, tag + "_ffn_down")
        if final:
            xl = res
        else:
            xl, hl = res
        if ctx_full:
            ac = _up(_prenorm(xc, cnorm2), fw_in, 2 * f_hidden, [0], None, _epi_id, BF16, tag + "_ffn_up_ctx")
            xc, hc = _down_ffn(ac, ffn_dw[i], ffn_dw_b[i], fw_out, xc, cg2, next_norm(i, 1), False, False,
                               tag + "_ffn_down_ctx")
    return xl.reshape(bsz, seq, d)
```

```python
import functools

import jax
import jax.numpy as jnp
from jax import lax
from jax.experimental import pallas as pl
from jax.experimental.pallas import tpu as pltpu

F32 = jnp.float32
BF16 = jnp.bfloat16
HI = lax.Precision.HIGHEST

NORM_EPS = 1e-6
GRID_W = 64
DEPTH_MIXERS = 3
SSD_GROUPS = 8
SSD_HEADDIM = 64
SSD_STATE = 128
SSD_CHUNK = 128
SSD_BLOCK = 4
SGU_CHUNK = 128
SGU_GROUPS = 16
LANE = 128
SUBLANE = 8
CONV_LANES = 256
CONV_ROWS = 64
ROW_CHUNK = 64
VMEM_LIMIT = 58 * 1024 * 1024


def _cp(*sem):
    return pltpu.CompilerParams(dimension_semantics=sem, vmem_limit_bytes=VMEM_LIMIT)


def _tile(n, pref, align):
    if n <= pref:
        return n
    t = (pref // align) * align
    while t >= align:
        if n % t == 0:
            return t
        t -= align
    return n


def _row(v):
    return v.reshape(1, -1)


def _prenorm_math(x, g, sc, sh):
    ms = jnp.mean(x * x, axis=-1, keepdims=True)
    return (x * lax.rsqrt(ms + NORM_EPS)) * g * (1.0 + sc) + sh


def _silu(x):
    return x * jax.nn.sigmoid(x)


def _row_loop(nrows, fn):
    step = min(nrows, ROW_CHUNK)

    def body(r, carry):
        fn(pl.ds(pl.multiple_of(r * step, step), step))
        return carry

    lax.fori_loop(0, nrows // step, body, 0)


def _adaln_kernel(v_ref, wa_ref, wb_ref, b_ref, o_ref, t_ref):
    @pl.when(pl.program_id(1) == 0)
    def _():
        t_ref[...] = jnp.dot(_silu(v_ref[...]), wa_ref[0], preferred_element_type=F32, precision=HI)

    o_ref[0] = jnp.dot(t_ref[...], wb_ref[0], preferred_element_type=F32, precision=HI) + b_ref[0]


def _adaln(v8, wa, wb, b):
    depth, d, r = wa.shape
    n = wb.shape[2]
    tn = _tile(n, 6144, LANE)
    return pl.pallas_call(
        _adaln_kernel,
        grid=(depth, n // tn),
        in_specs=[pl.BlockSpec((SUBLANE, d), lambda l, j: (0, 0)),
                  pl.BlockSpec((1, d, r), lambda l, j: (l, 0, 0)),
                  pl.BlockSpec((1, r, tn), lambda l, j: (l, 0, j)),
                  pl.BlockSpec((1, 1, tn), lambda l, j: (l, 0, j))],
        out_specs=pl.BlockSpec((1, SUBLANE, tn), lambda l, j: (l, 0, j)),
        out_shape=jax.ShapeDtypeStruct((depth, SUBLANE, n), F32),
        scratch_shapes=[pltpu.VMEM((SUBLANE, r), F32)],
        compiler_params=_cp("arbitrary", "arbitrary"),
        name="adaln",
    )(v8, wa, wb, b.reshape(depth, 1, n))


def _prenorm_kernel(x_ref, g_ref, sc_ref, sh_ref, o_ref):
    o_ref[...] = _prenorm_math(x_ref[...], g_ref[...], sc_ref[...], sh_ref[...]).astype(o_ref.dtype)


def _prenorm(x, norm):
    g, sc, sh = norm
    m, d = x.shape
    tm = _tile(m, 256, SUBLANE)
    vec = pl.BlockSpec((1, d), lambda i: (0, 0))
    return pl.pallas_call(
        _prenorm_kernel,
        grid=(m // tm,),
        in_specs=[pl.BlockSpec((tm, d), lambda i: (i, 0)), vec, vec, vec],
        out_specs=pl.BlockSpec((tm, d), lambda i: (i, 0)),
        out_shape=jax.ShapeDtypeStruct((m, d), BF16),
        compiler_params=_cp("arbitrary"),
        name="prenorm",
    )(x, _row(g), _row(sc), _row(sh))


def _ln_silu_kernel(x_ref, g_ref, b_ref, o_ref):
    x = x_ref[...]
    mu = jnp.mean(x, axis=-1, keepdims=True)
    xc = x - mu
    var = jnp.mean(xc * xc, axis=-1, keepdims=True)
    y = xc * lax.rsqrt(var + NORM_EPS) * g_ref[...] + b_ref[...]
    o_ref[...] = _silu(y).astype(o_ref.dtype)


def _ln_silu(x, g, b):
    m, d = x.shape
    tm = _tile(m, 256, SUBLANE)
    vec = pl.BlockSpec((1, d), lambda i: (0, 0))
    return pl.pallas_call(
        _ln_silu_kernel,
        grid=(m // tm,),
        in_specs=[pl.BlockSpec((tm, d), lambda i: (i, 0)), vec, vec],
        out_specs=pl.BlockSpec((tm, d), lambda i: (i, 0)),
        out_shape=jax.ShapeDtypeStruct((m, d), BF16),
        compiler_params=_cp("arbitrary"),
        name="ln_silu",
    )(x, _row(g), _row(b))


def _up_kernel(*refs, n_w, has_bias, epi, sub):
    h_ref = refs[0]
    w_refs = refs[1:1 + n_w]
    b_refs = refs[1 + n_w:1 + 2 * n_w] if has_bias else ()
    o_ref = refs[1 + n_w * (2 if has_bias else 1)]
    wb_refs = refs[-n_w:]

    @pl.when(pl.program_id(1) == 0)
    def _():
        for t in range(n_w):
            def cast(rows, t=t):
                wb_refs[t][rows, :] = w_refs[t][rows, :].astype(BF16)

            _row_loop(w_refs[t].shape[0], cast)

    h = h_ref[...]
    for n0 in range(0, o_ref.shape[1], sub):
        accs = []
        for t in range(n_w):
            a = jnp.dot(h, wb_refs[t][:, n0:n0 + sub], preferred_element_type=F32)
            if has_bias:
                a = a + b_refs[t][:, n0:n0 + sub]
            accs.append(a)
        o_ref[:, n0:n0 + sub] = epi(*accs).astype(o_ref.dtype)


def _epi_id(a):
    return a


def _epi_glu(a, b):
    return a * jax.nn.sigmoid(b)


def _epi_gelu(a):
    return jax.nn.gelu(a)


def _epi_softplus(a):
    return jnp.maximum(a, 0.0) + jnp.log1p(jnp.exp(-jnp.abs(a)))


def _up(h, w, n_out, col_offs, bias, epi, out_dtype, name, tn_pref=1024):
    m, k = h.shape
    tm = _tile(m, 512, SUBLANE)
    tn = _tile(n_out, tn_pref // len(col_offs), LANE)
    for off in col_offs:
        assert off % tn == 0
    n_w = len(col_offs)
    in_specs = [pl.BlockSpec((tm, k), lambda j, i: (i, 0))]
    args = [h]
    for off in col_offs:
        in_specs.append(pl.BlockSpec((k, tn), functools.partial(lambda j, i, o: (0, j + o), o=off // tn)))
        args.append(w)
    if bias is not None:
        for off in col_offs:
            in_specs.append(pl.BlockSpec((1, tn), functools.partial(lambda j, i, o: (0, j + o), o=off // tn)))
            args.append(_row(bias))
    return pl.pallas_call(
        functools.partial(_up_kernel, n_w=n_w, has_bias=bias is not None, epi=epi,
                          sub=256 if tn % 256 == 0 else LANE),
        grid=(n_out // tn, m // tm),
        in_specs=in_specs,
        out_specs=pl.BlockSpec((tm, tn), lambda j, i: (i, j)),
        out_shape=jax.ShapeDtypeStruct((m, n_out), out_dtype),
        scratch_shapes=[pltpu.VMEM((k, tn), BF16)] * n_w,
        compiler_params=_cp("arbitrary", "arbitrary"),
        name=name,
    )(*args)


def _up_conv_kernel(h_ref, hp_ref, hn_ref, w_ref, cw_ref, cb_ref, o_ref, e0_ref, e1_ref, *, ntaps, sub):
    tm = h_ref.shape[0]
    pad = hp_ref.shape[0]
    half = ntaps // 2
    i = pl.program_id(0)
    first = i == 0
    last = i == pl.num_programs(0) - 1
    h, hp, hn = h_ref[...], hp_ref[...], hn_ref[...]
    rs = min(tm, CONV_ROWS)
    starts = list(range(0, o_ref.shape[1], sub))
    exts = (e0_ref, e1_ref)

    def project(c):
        ext_ref, wsub = exts[c % 2], w_ref[:, starts[c]:starts[c] + sub]
        ext_ref[pl.ds(0, pad), :] = jnp.where(first, 0.0, jnp.dot(hp, wsub, preferred_element_type=F32))
        ext_ref[pl.ds(pad, tm), :] = jnp.dot(h, wsub, preferred_element_type=F32)
        ext_ref[pl.ds(pad + tm, pad), :] = jnp.where(last, 0.0, jnp.dot(hn, wsub, preferred_element_type=F32))

    def conv(c):
        ext_ref, n0 = exts[c % 2], starts[c]
        bias = cb_ref[:, n0:n0 + sub]
        for r0 in range(0, tm, rs):
            acc = None
            for k in range(ntaps):
                v = ext_ref[pl.ds(pad + r0 + k - half, rs), :] * cw_ref[pl.ds(k, 1), n0:n0 + sub]
                acc = v if acc is None else acc + v
            o_ref[pl.ds(r0, rs), n0:n0 + sub] = _silu(acc + bias)

    project(0)
    for c in range(len(starts)):
        if c + 1 < len(starts):
            project(c + 1)
        conv(c)


def _up_conv(h, w, n_out, col_off, conv_w, conv_b, name):
    m, k = h.shape
    ntaps = conv_w.shape[0]
    tm = _tile(m, 1024, 2 * SUBLANE)
    tn = _tile(n_out, 1024, LANE)
    assert col_off % tn == 0
    pad = 2 * SUBLANE
    ratio = tm // pad
    nhb = m // pad
    sub = 256 if tn % 256 == 0 else LANE
    off = col_off // tn
    return pl.pallas_call(
        functools.partial(_up_conv_kernel, ntaps=ntaps, sub=sub),
        grid=(m // tm, n_out // tn),
        in_specs=[pl.BlockSpec((tm, k), lambda i, j: (i, 0)),
                  pl.BlockSpec((pad, k), lambda i, j: (jnp.maximum(i * ratio - 1, 0), 0)),
                  pl.BlockSpec((pad, k), lambda i, j: (jnp.minimum((i + 1) * ratio, nhb - 1), 0)),
                  pl.BlockSpec((k, tn), lambda i, j: (0, j + off)),
                  pl.BlockSpec((ntaps, tn), lambda i, j: (0, j)),
                  pl.BlockSpec((1, tn), lambda i, j: (0, j))],
        out_specs=pl.BlockSpec((tm, tn), lambda i, j: (i, j)),
        out_shape=jax.ShapeDtypeStruct((m, n_out), F32),
        scratch_shapes=[pltpu.VMEM((tm + 2 * pad, sub), F32), pltpu.VMEM((tm + 2 * pad, sub), F32)],
        compiler_params=_cp("arbitrary", "arbitrary"),
        name=name,
    )(h, h, h, w, conv_w, _row(conv_b))


def _seqconv_kernel(x_ref, xp_ref, xn_ref, w_ref, b_ref, o_ref, buf_ref, *, ntaps, act):
    t_rows, c = x_ref.shape
    pad = xp_ref.shape[0]
    i = pl.program_id(0)
    buf_ref[pl.ds(0, pad), :] = jnp.where(i > 0, xp_ref[...].astype(F32), 0.0)
    buf_ref[pl.ds(pad, t_rows), :] = x_ref[...].astype(F32)
    buf_ref[pl.ds(pad + t_rows, pad), :] = jnp.where(i < pl.num_programs(0) - 1, xn_ref[...].astype(F32), 0.0)
    half = ntaps // 2
    sub = min(t_rows, CONV_ROWS)
    lw = min(c, CONV_LANES)

    def body(ci, carry):
        lanes = pl.ds(pl.multiple_of(ci * lw, lw), lw)
        bias = b_ref[:, lanes]
        for s in range(t_rows // sub):
            acc = None
            for k in range(ntaps):
                v = buf_ref[pl.ds(pad + s * sub + k - half, sub), lanes] * w_ref[pl.ds(k, 1), lanes]
                acc = v if acc is None else acc + v
            acc = acc + bias
            if act == "silu":
                acc = _silu(acc)
            o_ref[pl.ds(s * sub, sub), lanes] = acc
        return carry

    lax.fori_loop(0, c // lw, body, 0)


def _seqconv(x, w, b, *, tile_rows, act, name):
    rows, c = x.shape
    ntaps = w.shape[0]
    pad = SUBLANE * (4 // x.dtype.itemsize)
    assert ntaps // 2 <= pad and rows % tile_rows == 0
    ratio = tile_rows // pad
    nhb = rows // pad
    return pl.pallas_call(
        functools.partial(_seqconv_kernel, ntaps=ntaps, act=act),
        grid=(rows // tile_rows,),
        in_specs=[pl.BlockSpec((tile_rows, c), lambda i: (i, 0)),
                  pl.BlockSpec((pad, c), lambda i: (jnp.maximum(i * ratio - 1, 0), 0)),
                  pl.BlockSpec((pad, c), lambda i: (jnp.minimum((i + 1) * ratio, nhb - 1), 0)),
                  pl.BlockSpec((ntaps, c), lambda i: (0, 0)),
                  pl.BlockSpec((1, c), lambda i: (0, 0))],
        out_specs=pl.BlockSpec((tile_rows, c), lambda i: (i, 0)),
        out_shape=jax.ShapeDtypeStruct((rows, c), F32),
        scratch_shapes=[pltpu.VMEM((tile_rows + 2 * pad, c), F32)],
        compiler_params=_cp("arbitrary"),
        name=name,
    )(x, x, x, w, _row(b))


def _rowconv_kernel(x_ref, w_ref, b_ref, g_ref, beta_ref, o_ref, pad_ref, ph_ref, y_ref, *, ntaps, seg):
    t_rows, c = x_ref.shape
    nseg = t_rows // seg
    lead = 2 * SUBLANE
    half = ntaps // 2
    amax = (lead - half + ntaps - 1) // SUBLANE
    span = seg + SUBLANE * amax
    stride = span + SUBLANE
    lw = ph_ref.shape[2]
    sub = min(seg, CONV_ROWS)
    for r in range(nseg):
        pad_ref[pl.ds(r * stride, lead), :] = jnp.zeros((lead, c), F32)
        pad_ref[pl.ds(r * stride + lead, seg), :] = x_ref[pl.ds(r * seg, seg), :]
        pad_ref[pl.ds(r * stride + lead + seg, stride - lead - seg), :] = jnp.zeros((stride - lead - seg, c), F32)

    def body(ci, carry):
        lanes = pl.ds(pl.multiple_of(ci * lw, lw), lw)
        bias = b_ref[:, lanes]
        for r in range(nseg):
            for b in range(1, SUBLANE):
                ph_ref[b] = pad_ref[pl.ds(r * stride + b, span), lanes]
            for s in range(0, seg, sub):
                acc = None
                for k in range(ntaps):
                    off = lead - half + k
                    a, b = off // SUBLANE, off % SUBLANE
                    if b == 0:
                        src = pad_ref[pl.ds(r * stride + SUBLANE * a + s, sub), lanes]
                    else:
                        src = ph_ref[b, pl.ds(SUBLANE * a + s, sub), :]
                    v = src * w_ref[pl.ds(k, 1), lanes]
                    acc = v if acc is None else acc + v
                y_ref[pl.ds(r * seg + s, sub), lanes] = acc + bias
        return carry

    lax.fori_loop(0, c // lw, body, 0)

    def norm_rows(rows):
        y = y_ref[rows, :]
        mu = jnp.mean(y, axis=-1, keepdims=True)
        yc = y - mu
        var = jnp.mean(yc * yc, axis=-1, keepdims=True)
        o_ref[rows, :] = _silu(yc * lax.rsqrt(var + NORM_EPS) * g_ref[...] + beta_ref[...]).astype(o_ref.dtype)

    _row_loop(t_rows, norm_rows)


def _rowconv(x, w, b, ln_g, ln_b, *, seg, tile_rows, name):
    rows, c = x.shape
    ntaps = w.shape[0]
    lead = 2 * SUBLANE
    assert ntaps // 2 <= lead and seg % SUBLANE == 0 and tile_rows % seg == 0
    amax = (lead - ntaps // 2 + ntaps - 1) // SUBLANE
    span = seg + SUBLANE * amax
    stride = span + SUBLANE
    lw = min(c, CONV_LANES)
    return pl.pallas_call(
        functools.partial(_rowconv_kernel, ntaps=ntaps, seg=seg),
        grid=(rows // tile_rows,),
        in_specs=[pl.BlockSpec((tile_rows, c), lambda i: (i, 0)),
                  pl.BlockSpec((ntaps, c), lambda i: (0, 0)),
                  pl.BlockSpec((1, c), lambda i: (0, 0)),
                  pl.BlockSpec((1, c), lambda i: (0, 0)),
                  pl.BlockSpec((1, c), lambda i: (0, 0))],
        out_specs=pl.BlockSpec((tile_rows, c), lambda i: (i, 0)),
        out_shape=jax.ShapeDtypeStruct((rows, c), BF16),
        scratch_shapes=[pltpu.VMEM(((tile_rows // seg) * stride, c), F32),
                        pltpu.VMEM((SUBLANE, span, lw), F32),
                        pltpu.VMEM((tile_rows, c), F32)],
        compiler_params=_cp("arbitrary"),
        name=name,
    )(x, w, _row(b), _row(ln_g), _row(ln_b))


def _colconv_kernel(x_ref, w_ref, b_ref, o_ref, buf_ref, *, ntaps):
    rows, width, lw = x_ref.shape
    half = ntaps // 2
    lead = buf_ref.shape[0] - rows - half
    zeros = jnp.zeros((width, lw), F32)
    for r in range(lead):
        buf_ref[r] = zeros
    for r in range(half):
        buf_ref[lead + rows + r] = zeros

    def copy(r, carry):
        buf_ref[lead + r] = x_ref[r]
        return carry

    lax.fori_loop(0, rows, copy, 0)

    def body(r, carry):
        acc = None
        for k in range(ntaps):
            v = buf_ref[lead - half + r + k] * w_ref[pl.ds(k, 1), :]
            acc = v if acc is None else acc + v
        o_ref[r] = acc + b_ref[...]
        return carry

    lax.fori_loop(0, rows, body, 0)


def _colconv(x3, w, b, name):
    rows, width, c = x3.shape
    ntaps = w.shape[0]
    half = ntaps // 2
    lw = min(c, LANE)
    return pl.pallas_call(
        functools.partial(_colconv_kernel, ntaps=ntaps),
        grid=(c // lw,),
        in_specs=[pl.BlockSpec((rows, width, lw), lambda j: (0, 0, j)),
                  pl.BlockSpec((ntaps, lw), lambda j: (0, j)),
                  pl.BlockSpec((1, lw), lambda j: (0, j))],
        out_specs=pl.BlockSpec((rows, width, lw), lambda j: (0, 0, j)),
        out_shape=jax.ShapeDtypeStruct((rows, width, c), F32),
        scratch_shapes=[pltpu.VMEM((rows + 2 * half, width, lw), F32)],
        compiler_params=_cp("arbitrary"),
        name=name,
    )(x3, w, _row(b))


def _down_mm_kernel(*refs, has_bias):
    if has_bias:
        a_ref, w_ref, bias_ref, x_ref, gate_ref, o_ref = refs
    else:
        a_ref, w_ref, x_ref, gate_ref, o_ref = refs
    k = pl.program_id(2)
    part = jnp.dot(a_ref[...], w_ref[...], preferred_element_type=F32)

    @pl.when(k == 0)
    def _():
        o_ref[...] = part

    @pl.when(k > 0)
    def _():
        o_ref[...] += part

    @pl.when(k == pl.num_programs(2) - 1)
    def _():
        y = o_ref[...]
        if has_bias:
            y = y + bias_ref[...]
        o_ref[...] = x_ref[...] + gate_ref[...] * y


def _down_mm(a, w, bias, x, gate, name):
    m, kdim = a.shape
    d = w.shape[1]
    tm = _tile(m, 1024, SUBLANE)
    tn = _tile(d, 1024, LANE)
    tk = _tile(kdim, 2048, LANE)
    vec = pl.BlockSpec((1, tn), lambda i, j, k: (0, j))
    in_specs = [pl.BlockSpec((tm, tk), lambda i, j, k: (i, k)),
                pl.BlockSpec((tk, tn), lambda i, j, k: (k, j))]
    args = [a, w]
    if bias is not None:
        in_specs.append(vec)
        args.append(_row(bias))
    in_specs += [pl.BlockSpec((tm, tn), lambda i, j, k: (i, j)), vec]
    args += [x, _row(gate)]
    return pl.pallas_call(
        functools.partial(_down_mm_kernel, has_bias=bias is not None),
        grid=(m // tm, d // tn, kdim // tk),
        in_specs=in_specs,
        out_specs=pl.BlockSpec((tm, tn), lambda i, j, k: (i, j)),
        out_shape=jax.ShapeDtypeStruct((m, d), F32),
        compiler_params=_cp("arbitrary", "arbitrary", "arbitrary"),
        name=name,
    )(*args)


def _ffn_down_kernel(gc_ref, gp_ref, gn_ref, up_ref, dw_ref, dwb_ref, w_ref, x_ref, gate_ref, ng_ref, sc_ref,
                     sh_ref, *rest, on_grid, final, n_i, nk):
    if final:
        o_ref, buf_ref, a0_ref, a1_ref = rest
        h_ref = None
    else:
        o_ref, h_ref, buf_ref, a0_ref, a1_ref = rest
    tm, tk = gc_ref.shape
    hb = gp_ref.shape[0]
    d = o_ref.shape[1]
    s = pl.program_id(0)
    ip = jnp.minimum(s, n_i * nk - 1) // nk
    m = jnp.maximum(s - 1, 0)
    km = lax.rem(m, nk)

    @pl.when(s == 0)
    def _():
        a1_ref[...] = jnp.zeros_like(a1_ref)

    @pl.when(km == 0)
    def _():
        def zero(rows):
            o_ref[rows, :] = jnp.zeros((min(tm, ROW_CHUNK), d), F32)

        _row_loop(tm, zero)

    sub = min(tm, CONV_ROWS)
    lw = min(tk, CONV_LANES)
    dys = (-1, 0, 1) if on_grid else (0,)
    tn = min(d, 256)

    def step(dst_ref, src_ref):
        buf_ref[pl.ds(0, hb), :] = jnp.where(ip > 0, gp_ref[...].astype(F32), 0.0)
        buf_ref[pl.ds(hb, tm), :] = gc_ref[...].astype(F32)
        buf_ref[pl.ds(hb + tm, hb), :] = jnp.where(ip < n_i - 1, gn_ref[...].astype(F32), 0.0)
        col = lax.broadcasted_iota(jnp.int32, (sub, lw), 0)

        def build(lc, r):
            lanes = pl.ds(lc * lw, lw)
            base = hb + r * sub

            def tapsum(dx):
                acc = None
                for dy in dys:
                    v = (buf_ref[pl.ds(base + GRID_W * dy + dx, sub), lanes]
                         * dw_ref[pl.ds((dy + 1) * 3 + dx + 1, 1), lanes])
                    acc = v if acc is None else acc + v
                return acc

            left, right = tapsum(-1), tapsum(1)
            if on_grid:
                left = jnp.where(col >= 1, left, 0.0)
                right = jnp.where(col <= GRID_W - 2, right, 0.0)
            conv = tapsum(0) + left + right + dwb_ref[:, lanes]
            dst_ref[pl.ds(r * sub, sub), lanes] = (
                jax.nn.gelu(conv) * up_ref[pl.ds(r * sub, sub), lanes].astype(F32)).astype(BF16)

        blocks = [(lc, r) for lc in range(tk // lw) for r in range(tm // sub)]
        n_mm = d // tn
        per = -(-len(blocks) // n_mm)
        a = src_ref[...]
        for c in range(n_mm):
            n0 = c * tn
            o_ref[:, n0:n0 + tn] += jnp.dot(a, w_ref[:, n0:n0 + tn], preferred_element_type=F32)
            for lc, r in blocks[c * per:(c + 1) * per]:
                build(lc, r)

    parity = lax.rem(s, 2)

    @pl.when(parity == 0)
    def _():
        step(a0_ref, a1_ref)

    @pl.when(parity == 1)
    def _():
        step(a1_ref, a0_ref)

    @pl.when((km == nk - 1) & (s > 0))
    def _():
        def finish(rows):
            xn = x_ref[rows, :] + gate_ref[...] * o_ref[rows, :]
            if final:
                ms = jnp.mean(xn * xn, axis=-1, keepdims=True)
                o_ref[rows, :] = (xn * lax.rsqrt(ms + NORM_EPS)) * ng_ref[...]
            else:
                o_ref[rows, :] = xn
                h_ref[rows, :] = _prenorm_math(xn, ng_ref[...], sc_ref[...], sh_ref[...]).astype(BF16)

        _row_loop(tm, finish)


def _down_ffn(a, dw, dw_b, w, x, gate, norm, on_grid, final, name):
    ng, sc, sh = norm
    m = a.shape[0]
    f, d = w.shape
    tm = _tile(m, 512, 2 * GRID_W)
    tk = _tile(f, 512, LANE)
    hb = 2 * GRID_W
    ratio = tm // hb
    nhb = m // hb
    koff = f // tk
    n_i, nk = m // tm, f // tk
    last = n_i * nk - 1

    def pro(s):
        p = jnp.minimum(s, last)
        return p // nk, lax.rem(p, nk)

    def mat(s):
        q = jnp.maximum(s - 1, 0)
        return q // nk, lax.rem(q, nk)

    vec = pl.BlockSpec((1, d), lambda s: (0, 0))
    rowblk = pl.BlockSpec((tm, d), lambda s: (mat(s)[0], 0))
    in_specs = [pl.BlockSpec((tm, tk), lambda s: pro(s)),
                pl.BlockSpec((hb, tk), lambda s: (jnp.maximum(pro(s)[0] * ratio - 1, 0), pro(s)[1])),
                pl.BlockSpec((hb, tk), lambda s: (jnp.minimum((pro(s)[0] + 1) * ratio, nhb - 1), pro(s)[1])),
                pl.BlockSpec((tm, tk), lambda s: (pro(s)[0], pro(s)[1] + koff)),
                pl.BlockSpec((9, tk), lambda s: (0, pro(s)[1])),
                pl.BlockSpec((1, tk), lambda s: (0, pro(s)[1])),
                pl.BlockSpec((tk, d), lambda s: (mat(s)[1], 0)),
                rowblk, vec, vec, vec, vec]
    out_specs = [rowblk]
    out_shape = [jax.ShapeDtypeStruct((m, d), F32)]
    if not final:
        out_specs.append(rowblk)
        out_shape.append(jax.ShapeDtypeStruct((m, d), BF16))
    res = pl.pallas_call(
        functools.partial(_ffn_down_kernel, on_grid=on_grid, final=final, n_i=n_i, nk=nk),
        grid=(n_i * nk + 1,),
        in_specs=in_specs,
        out_specs=out_specs,
        out_shape=out_shape,
        scratch_shapes=[pltpu.VMEM((tm + 2 * hb, tk), F32), pltpu.VMEM((tm, tk), BF16),
                        pltpu.VMEM((tm, tk), BF16)],
        compiler_params=_cp("arbitrary"),
        name=name,
    )(a, a, a, a, dw.reshape(9, f), _row(dw_b), w, x, _row(gate), _row(ng), _row(sc), _row(sh))
    return res[0] if final else res


def _sgu_kernel(u_ref, v_ref, g_ref, b_ref, ws_ref, bs_ref, o_ref):
    v = v_ref[...].astype(F32)
    mu = jnp.mean(v, axis=-1, keepdims=True)
    vc = v - mu
    var = jnp.mean(vc * vc, axis=-1, keepdims=True)
    vn = (vc * lax.rsqrt(var + NORM_EPS) * g_ref[...] + b_ref[...]).astype(BF16)
    groups = ws_ref.shape[0]
    gw = v.shape[1] // groups
    for g in range(groups):
        mixed = jnp.dot(ws_ref[g], vn[:, g * gw:(g + 1) * gw], preferred_element_type=F32) + bs_ref[g]
        o_ref[:, g * gw:(g + 1) * gw] = (u_ref[:, g * gw:(g + 1) * gw].astype(F32) * mixed).astype(o_ref.dtype)


def _sgu_mix(a, ln_g, ln_b, ws, bs):
    m = a.shape[0]
    e = a.shape[1] // 2
    q = SGU_CHUNK
    groups = ws.shape[0]
    return pl.pallas_call(
        _sgu_kernel,
        grid=(m // q,),
        in_specs=[pl.BlockSpec((q, e), lambda c: (c, 0)),
                  pl.BlockSpec((q, e), lambda c: (c, 1)),
                  pl.BlockSpec((1, e), lambda c: (0, 0)),
                  pl.BlockSpec((1, e), lambda c: (0, 0)),
                  pl.BlockSpec((groups, q, q), lambda c: (0, 0, 0)),
                  pl.BlockSpec((groups, q, 1), lambda c: (0, 0, 0))],
        out_specs=pl.BlockSpec((q, e), lambda c: (c, 0)),
        out_shape=jax.ShapeDtypeStruct((m, e), BF16),
        compiler_params=_cp("arbitrary"),
        name="sgu_mix",
    )(a, a, _row(ln_g), _row(ln_b), ws.astype(BF16), bs.reshape(groups, q, 1))


def _split3(a):
    a1 = a.astype(BF16)
    r1 = a - a1.astype(F32)
    a2 = r1.astype(BF16)
    a3 = (r1 - a2.astype(F32)).astype(BF16)
    return [a1, a2, a3]


def _ssd_kernel(*refs, reverse, mode):
    x_ref, b_ref, c_ref, dt_ref, dtt_ref, al_ref, alt_ref, s0_ref = refs[:8]
    if mode == "gated":
        yprev_ref, d_ref, z_ref, ng_ref, y_ref, sfin_ref, st_ref, yacc_ref = refs[8:]
    elif mode == "y":
        y_ref, sfin_ref, st_ref = refs[8:]
    else:
        sfin_ref, st_ref = refs[8:]
    c = pl.program_id(1)

    @pl.when(c == 0)
    def _():
        st_ref[...] = s0_ref[0]

    q = SSD_CHUNK
    hpg = dt_ref.shape[3]
    pw = 2 * SSD_HEADDIM
    row = lax.broadcasted_iota(jnp.int32, (q, q), 0)
    col = lax.broadcasted_iota(jnp.int32, (q, q), 1)
    mask = (row <= col) if reverse else (row >= col)
    mask_t = (row >= col) if reverse else (row <= col)
    mask3 = jnp.concatenate([mask.astype(BF16)] * 3, axis=1)
    mask3_t = jnp.concatenate([mask_t.astype(BF16)] * 3, axis=0)
    lo = lax.broadcasted_iota(jnp.int32, (1, pw), 1) < SSD_HEADDIM
    neg_a = -jnp.exp(al_ref[0, 0])
    neg_a_t = -jnp.exp(alt_ref[0, 0])

    def halves(a):
        return jnp.concatenate([jnp.where(lo, a, 0.0), jnp.where(lo, 0.0, a)], axis=0).astype(BF16)

    def chunk(r0):
        rows = pl.ds(r0, q)
        dtt = dtt_ref[0, 0][:, r0:r0 + q]
        dtat = dtt * neg_a_t
        cumt = jnp.dot(jnp.concatenate(_split3(dtat), axis=1), mask3_t,
                       preferred_element_type=F32)
        lastt = jnp.sum(dtat, axis=1, keepdims=True)
        vt = dtt * jnp.exp(lastt - cumt)
        dta = dt_ref[0, 0, rows, :] * neg_a
        elast = jnp.exp(jnp.sum(dta, axis=0, keepdims=True))
        bmat = b_ref[rows, :]
        bt = bmat.T
        if mode != "state":
            cum = jnp.dot(mask3, jnp.concatenate(_split3(dta), axis=0), preferred_element_type=F32)
            cmat = c_ref[rows, :]
            cb = lax.dot_general(cmat.astype(BF16), bmat.astype(BF16), (((1,), (1,)), ((), ())),
                                 preferred_element_type=F32)

        for j in range(hpg // 2):
            h0, h1 = 2 * j, 2 * j + 1
            cols = slice(j * pw, (j + 1) * pw)
            xs = x_ref[rows, cols]
            xcat = halves(xs)
            st = st_ref[j]
            btv = jnp.concatenate([bt * vt[h0:h0 + 1, :], bt * vt[h1:h1 + 1, :]], axis=1).astype(BF16)
            st_ref[j] = (st * jnp.where(lo, elast[:, h0:h0 + 1], elast[:, h1:h1 + 1])
                         + jnp.dot(btv, xcat, preferred_element_type=F32))
            if mode == "state":
                continue

            def lhs_parts(h):
                colb = jnp.broadcast_to(cum[:, h:h + 1], (q, q))
                e = jnp.where(mask, jnp.exp(colb - cumt[h:h + 1, :]), 0.0)
                return (cb * e * dtt[h:h + 1, :]).astype(BF16), (cmat * jnp.exp(colb)).astype(BF16)

            m0, c0 = lhs_parts(h0)
            m1, c1 = lhs_parts(h1)
            lhs = jnp.concatenate([m0, m1, c0, c1], axis=1)
            rhs = jnp.concatenate([xcat, halves(st)], axis=0)
            y = jnp.dot(lhs, rhs, preferred_element_type=F32)
            if mode == "gated":
                dlane = jnp.where(lo, d_ref[0][:, h0:h0 + 1], d_ref[0][:, h1:h1 + 1])
                yacc_ref[rows, cols] = y + yprev_ref[rows, cols] + xs * dlane
            else:
                y_ref[rows, cols] = y

        if mode == "gated":
            yz = yacc_ref[rows, :] * _silu(z_ref[rows, :].astype(F32))
            ms = jnp.mean(yz * yz, axis=-1, keepdims=True)
            y_ref[rows, :] = (yz * lax.rsqrt(ms + NORM_EPS) * ng_ref[...]).astype(y_ref.dtype)

    starts = list(range(0, x_ref.shape[0], q))
    for r0 in (reversed(starts) if reverse else starts):
        chunk(r0)

    @pl.when(c == pl.num_programs(1) - 1)
    def _():
        sfin_ref[0] = st_ref[...]


def _ssd_scan(xbc, dt_a, dt_b, al_a, al_b, s0, direction, mode, extra=()):
    length = xbc.shape[0]
    groups, hpg = dt_a.shape[1], dt_a.shape[3]
    q = SSD_CHUNK * min(SSD_BLOCK, length // SSD_CHUNK)
    n = SSD_STATE
    pw = 2 * SSD_HEADDIM
    gw = hpg * SSD_HEADDIM
    d_inner = groups * gw
    assert length % q == 0
    nc = length // q
    reverse = direction == 1
    boff = d_inner // n

    def cc(c):
        return nc - 1 - c if reverse else c

    xblk = pl.BlockSpec((q, gw), lambda g, c: (cc(c), g))
    stblk = pl.BlockSpec((1, hpg // 2, n, pw), lambda g, c: (g, 0, 0, 0))
    in_specs = [xblk,
                pl.BlockSpec((q, n), lambda g, c: (cc(c), boff + g)),
                pl.BlockSpec((q, n), lambda g, c: (cc(c), boff + groups + g)),
                pl.BlockSpec((1, 1, q, hpg), lambda g, c: (direction, g, cc(c), 0)),
                pl.BlockSpec((1, 1, hpg, q), lambda g, c: (direction, g, 0, cc(c))),
                pl.BlockSpec((1, 1, 1, hpg), lambda g, c: (direction, g, 0, 0)),
                pl.BlockSpec((1, 1, hpg, 1), lambda g, c: (direction, g, 0, 0)),
                stblk]
    args = [xbc, xbc, xbc, dt_a, dt_b, al_a, al_b, s0]
    out_specs, out_shape = [], []
    scratch = [pltpu.VMEM((hpg // 2, n, pw), F32)]
    if mode == "gated":
        yprev, dskip, z, norm_g = extra
        in_specs += [xblk, pl.BlockSpec((1, 1, hpg), lambda g, c: (g, 0, 0)), xblk,
                     pl.BlockSpec((1, gw), lambda g, c: (0, g))]
        args += [yprev, dskip, z, _row(norm_g)]
        scratch.append(pltpu.VMEM((q, gw), F32))
    if mode != "state":
        out_specs.append(xblk)
        out_shape.append(jax.ShapeDtypeStruct((length, d_inner), BF16 if mode == "gated" else F32))
    out_specs.append(stblk)
    out_shape.append(jax.ShapeDtypeStruct(s0.shape, F32))
    return pl.pallas_call(
        functools.partial(_ssd_kernel, reverse=reverse, mode=mode),
        grid=(groups, nc),
        in_specs=in_specs,
        out_specs=out_specs,
        out_shape=out_shape,
        scratch_shapes=scratch,
        compiler_params=_cp("arbitrary", "arbitrary"),
        name="ssd_scan_%s_%s" % ("bwd" if reverse else "fwd", mode),
    )(*args)


def _ssd_inputs(h, w_xbc, w_dt, conv_w, conv_b, dt_bias, heads2, name):
    length = h.shape[0]
    xbc = _up_conv(h, w_xbc, w_xbc.shape[1], 0, conv_w, conv_b, name + "_xbc")
    dt = _up(h, w_dt, heads2, [0], dt_bias.reshape(-1), _epi_softplus, F32, name + "_dt", tn_pref=heads2)
    hpg = heads2 // 2 // SSD_GROUPS
    dt4 = dt.reshape(length, 2, SSD_GROUPS, hpg)
    return xbc, jnp.transpose(dt4, (1, 2, 0, 3)), jnp.transpose(dt4, (1, 2, 3, 0))


def kernel(x, c, ctx, c_ctx, norm_mix_g, norm_ffn_g, mod_wa, mod_wb, mod_b, conv_w1, conv_b1, conv_dw, conv_dw_b, conv_ln_g, conv_ln_b, conv_w2, conv_b2, ssd_w_in, ssd_conv_w, ssd_conv_b, ssd_a_log, ssd_dt_bias, ssd_d, ssd_norm_g, ssd_w_out, sgu_w1, sgu_b1, sgu_ln_g, sgu_ln_b, sgu_ws, sgu_bs, sgu_w2, sgu_b2, ffn_w_in, ffn_dw, ffn_dw_b, ffn_w_out, final_g):
    bsz, seq, d = x.shape
    assert bsz == 1
    depth = norm_mix_g.shape[0]
    xl = x.reshape(seq, d)
    xc = ctx.reshape(ctx.shape[1], d)
    grid_rows = seq // GRID_W
    f_hidden = ffn_w_out.shape[1]

    v8 = jnp.zeros((SUBLANE, d), F32).at[0].set(c[0]).at[1].set(c_ctx)
    mod = _adaln(v8, mod_wa, mod_wb, mod_b)

    def norms(layer, row):
        sh1, sc1, g1, sh2, sc2, g2 = [mod[layer, row, t * d:(t + 1) * d] for t in range(6)]
        return (norm_mix_g[layer], sc1, sh1), (norm_ffn_g[layer], sc2, sh2), g1, g2

    def next_norm(layer, row):
        if layer + 1 < depth:
            return norms(layer + 1, row)[0]
        zeros = jnp.zeros((d,), F32)
        return final_g, zeros, zeros

    ssd_layers = [i for i in range(depth) if i % DEPTH_MIXERS == 1]
    last_ctx = ssd_layers[-1] if ssd_layers else -1

    hl = _prenorm(xl, norms(0, 0)[0])
    hc = _prenorm(xc, norms(0, 1)[0]) if last_ctx >= 0 else None
    for i in range(depth):
        kind, k = i % DEPTH_MIXERS, i // DEPTH_MIXERS
        ctx_full = i < last_ctx
        _, norm2, g1, g2 = norms(i, 0)
        if i <= last_ctx:
            _, cnorm2, cg1, cg2 = norms(i, 1)
        tag = "l%d" % i

        if kind == 0:
            w1 = conv_w1[k]
            w2 = conv_w2[k].astype(BF16)
            gl = _up(hl, w1, d, [0, d], conv_b1[k], _epi_glu, F32, tag + "_conv_up")
            if k % 2 == 0:
                al = _rowconv(gl, conv_dw[k], conv_dw_b[k], conv_ln_g[k], conv_ln_b[k], seg=GRID_W,
                              tile_rows=2 * GRID_W, name=tag + "_rowconv")
            else:
                gl = _colconv(gl.reshape(grid_rows, GRID_W, d), conv_dw[k], conv_dw_b[k],
                              tag + "_colconv").reshape(seq, d)
                al = _ln_silu(gl, conv_ln_g[k], conv_ln_b[k])
            xl = _down_mm(al, w2, conv_b2[k], xl, g1, tag + "_conv_down")
            if ctx_full:
                gc = _up(hc, w1, d, [0, d], conv_b1[k], _epi_glu, F32, tag + "_conv_up_ctx")
                ac = _rowconv(gc, conv_dw[k], conv_dw_b[k], conv_ln_g[k], conv_ln_b[k], seg=gc.shape[0],
                              tile_rows=gc.shape[0], name=tag + "_seqconv_ctx")
                xc = _down_mm(ac, w2, conv_b2[k], xc, cg1, tag + "_conv_down_ctx")
        elif kind == 1:
            assert not ctx_full
            w_in = ssd_w_in[k]
            w_out = ssd_w_out[k].astype(BF16)
            d_inner = w_out.shape[0]
            heads2 = 2 * ssd_a_log.shape[2]
            conv_dim = w_in.shape[1] - d_inner - heads2
            w_xbc = w_in[:, d_inner:d_inner + conv_dim].astype(BF16)
            w_dt = w_in[:, d_inner + conv_dim:]
            hpg = heads2 // 2 // SSD_GROUPS
            al_a = ssd_a_log[k].reshape(2, SSD_GROUPS, 1, hpg)
            al_b = ssd_a_log[k].reshape(2, SSD_GROUPS, hpg, 1)
            dskip = ssd_d[k].reshape(SSD_GROUPS, 1, hpg)
            zero_state = jnp.zeros((SSD_GROUPS, hpg // 2, SSD_STATE, 2 * SSD_HEADDIM), F32)
            xbc_c, dta_c, dtb_c = _ssd_inputs(hc, w_xbc, w_dt, ssd_conv_w[k], ssd_conv_b[k], ssd_dt_bias[k],
                                              heads2, tag + "_ssd_ctx")
            (state_f,) = _ssd_scan(xbc_c, dta_c, dtb_c, al_a, al_b, zero_state, 0, "state")
            (state_b,) = _ssd_scan(xbc_c, dta_c, dtb_c, al_a, al_b, zero_state, 1, "state")
            xbc_l, dta_l, dtb_l = _ssd_inputs(hl, w_xbc, w_dt, ssd_conv_w[k], ssd_conv_b[k], ssd_dt_bias[k],
                                              heads2, tag + "_ssd")
            z = _up(hl, w_in, d_inner, [0], None, _epi_id, BF16, tag + "_ssd_z")
            y_f, _ = _ssd_scan(xbc_l, dta_l, dtb_l, al_a, al_b, state_f, 0, "y")
            p, _ = _ssd_scan(xbc_l, dta_l, dtb_l, al_a, al_b, state_b, 1, "gated",
                             (y_f, dskip, z, ssd_norm_g[k]))
            xl = _down_mm(p, w_out, None, xl, g1, tag + "_ssd_down")
        else:
            w1 = sgu_w1[k]
            w2 = sgu_w2[k].astype(BF16)

            def sgu(h, name):
                a = _up(h, w1, w1.shape[1], [0], sgu_b1[k], _epi_gelu, BF16, name + "_up")
                return _sgu_mix(a, sgu_ln_g[k], sgu_ln_b[k], sgu_ws[k], sgu_bs[k])

            xl = _down_mm(sgu(hl, tag + "_sgu"), w2, sgu_b2[k], xl, g1, tag + "_sgu_down")
            if ctx_full:
                xc = _down_mm(sgu(hc, tag + "_sgu_ctx"), w2, sgu_b2[k], xc, cg1, tag + "_sgu_down_ctx")

        fw_in = ffn_w_in[i]
        fw_out = ffn_w_out[i].astype(BF16)
        final = i == depth - 1
        a = _up(_prenorm(xl, norm2), fw_in, 2 * f_hidden, [0], None, _epi_id, BF16, tag + "_ffn_up")
        res = _down_ffn(a, ffn_dw[i], ffn_dw_b[i], fw_out, xl, g2, next_norm(i, 0), True, final, tag + "_ffn_down")
        if final:
            xl = res
        else:
            xl, hl = res
        if ctx_full:
            ac = _up(_prenorm(xc, cnorm2), fw_in, 2 * f_hidden, [0], None, _epi_id, BF16, tag + "_ffn_up_ctx")
            xc, hc = _down_ffn(ac, ffn_dw[i], ffn_dw_b[i], fw_out, xc, cg2, next_norm(i, 1), False, False,
                               tag + "_ffn_down_ctx")
    return xl.reshape(bsz, seq, d)
```

```python
import functools

import jax
import jax.numpy as jnp
from jax import lax
from jax.experimental import pallas as pl
from jax.experimental.pallas import tpu as pltpu

F32 = jnp.float32
BF16 = jnp.bfloat16
HI = lax.Precision.HIGHEST

NORM_EPS = 1e-6
GRID_W = 64
DEPTH_MIXERS = 3
SSD_GROUPS = 8
SSD_HEADDIM = 64
SSD_STATE = 128
SSD_CHUNK = 128
SSD_BLOCK = 4
SGU_CHUNK = 128
SGU_GROUPS = 16
LANE = 128
SUBLANE = 8
CONV_LANES = 256
CONV_ROWS = 64
UP_TM = 1024
ROW_CHUNK = 64
VMEM_LIMIT = 58 * 1024 * 1024


def _cp(*sem):
    return pltpu.CompilerParams(dimension_semantics=sem, vmem_limit_bytes=VMEM_LIMIT)


def _tile(n, pref, align):
    if n <= pref:
        return n
    t = (pref // align) * align
    while t >= align:
        if n % t == 0:
            return t
        t -= align
    return n


def _row(v):
    return v.reshape(1, -1)


def _prenorm_math(x, g, sc, sh):
    ms = jnp.mean(x * x, axis=-1, keepdims=True)
    return (x * lax.rsqrt(ms + NORM_EPS)) * g * (1.0 + sc) + sh


def _silu(x):
    return x * jax.nn.sigmoid(x)


def _row_loop(nrows, fn):
    step = min(nrows, ROW_CHUNK)

    def body(r, carry):
        fn(pl.ds(pl.multiple_of(r * step, step), step))
        return carry

    lax.fori_loop(0, nrows // step, body, 0)


def _adaln_kernel(v_ref, wa_ref, wb_ref, b_ref, o_ref, t_ref):
    @pl.when(pl.program_id(1) == 0)
    def _():
        t_ref[...] = jnp.dot(_silu(v_ref[...]), wa_ref[0], preferred_element_type=F32, precision=HI)

    o_ref[0] = jnp.dot(t_ref[...], wb_ref[0], preferred_element_type=F32, precision=HI) + b_ref[0]


def _adaln(v8, wa, wb, b):
    depth, d, r = wa.shape
    n = wb.shape[2]
    tn = _tile(n, 6144, LANE)
    return pl.pallas_call(
        _adaln_kernel,
        grid=(depth, n // tn),
        in_specs=[pl.BlockSpec((SUBLANE, d), lambda l, j: (0, 0)),
                  pl.BlockSpec((1, d, r), lambda l, j: (l, 0, 0)),
                  pl.BlockSpec((1, r, tn), lambda l, j: (l, 0, j)),
                  pl.BlockSpec((1, 1, tn), lambda l, j: (l, 0, j))],
        out_specs=pl.BlockSpec((1, SUBLANE, tn), lambda l, j: (l, 0, j)),
        out_shape=jax.ShapeDtypeStruct((depth, SUBLANE, n), F32),
        scratch_shapes=[pltpu.VMEM((SUBLANE, r), F32)],
        compiler_params=_cp("arbitrary", "arbitrary"),
        name="adaln",
    )(v8, wa, wb, b.reshape(depth, 1, n))


def _prenorm_kernel(x_ref, g_ref, sc_ref, sh_ref, o_ref):
    o_ref[...] = _prenorm_math(x_ref[...], g_ref[...], sc_ref[...], sh_ref[...]).astype(o_ref.dtype)


def _prenorm(x, norm):
    g, sc, sh = norm
    m, d = x.shape
    tm = _tile(m, 256, SUBLANE)
    vec = pl.BlockSpec((1, d), lambda i: (0, 0))
    return pl.pallas_call(
        _prenorm_kernel,
        grid=(m // tm,),
        in_specs=[pl.BlockSpec((tm, d), lambda i: (i, 0)), vec, vec, vec],
        out_specs=pl.BlockSpec((tm, d), lambda i: (i, 0)),
        out_shape=jax.ShapeDtypeStruct((m, d), BF16),
        compiler_params=_cp("arbitrary"),
        name="prenorm",
    )(x, _row(g), _row(sc), _row(sh))


def _ln_silu_kernel(x_ref, g_ref, b_ref, o_ref):
    x = x_ref[...]
    mu = jnp.mean(x, axis=-1, keepdims=True)
    xc = x - mu
    var = jnp.mean(xc * xc, axis=-1, keepdims=True)
    y = xc * lax.rsqrt(var + NORM_EPS) * g_ref[...] + b_ref[...]
    o_ref[...] = _silu(y).astype(o_ref.dtype)


def _ln_silu(x, g, b):
    m, d = x.shape
    tm = _tile(m, 256, SUBLANE)
    vec = pl.BlockSpec((1, d), lambda i: (0, 0))
    return pl.pallas_call(
        _ln_silu_kernel,
        grid=(m // tm,),
        in_specs=[pl.BlockSpec((tm, d), lambda i: (i, 0)), vec, vec],
        out_specs=pl.BlockSpec((tm, d), lambda i: (i, 0)),
        out_shape=jax.ShapeDtypeStruct((m, d), BF16),
        compiler_params=_cp("arbitrary"),
        name="ln_silu",
    )(x, _row(g), _row(b))


def _up_kernel(*refs, n_w, has_bias, epi, sub):
    h_ref = refs[0]
    w_refs = refs[1:1 + n_w]
    b_refs = refs[1 + n_w:1 + 2 * n_w] if has_bias else ()
    o_ref = refs[-1]
    h = h_ref[...]
    for n0 in range(0, o_ref.shape[1], sub):
        accs = []
        for t in range(n_w):
            a = jnp.dot(h, w_refs[t][:, n0:n0 + sub], preferred_element_type=F32)
            if has_bias:
                a = a + b_refs[t][:, n0:n0 + sub]
            accs.append(a)
        o_ref[:, n0:n0 + sub] = epi(*accs).astype(o_ref.dtype)


def _epi_id(a):
    return a


def _epi_glu(a, b):
    return a * jax.nn.sigmoid(b)


def _epi_gelu(a):
    return jax.nn.gelu(a)


def _epi_softplus(a):
    return jnp.maximum(a, 0.0) + jnp.log1p(jnp.exp(-jnp.abs(a)))


def _up(h, w, n_out, col_offs, bias, epi, out_dtype, name, tn_pref=1024):
    m, k = h.shape
    tm = _tile(m, UP_TM, SUBLANE)
    tn = _tile(n_out, tn_pref // len(col_offs), LANE)
    for off in col_offs:
        assert off % tn == 0
    n_w = len(col_offs)
    in_specs = [pl.BlockSpec((tm, k), lambda i, j: (i, 0))]
    args = [h]
    for off in col_offs:
        in_specs.append(pl.BlockSpec((k, tn), functools.partial(lambda i, j, o: (0, j + o), o=off // tn)))
        args.append(w)
    if bias is not None:
        for off in col_offs:
            in_specs.append(pl.BlockSpec((1, tn), functools.partial(lambda i, j, o: (0, j + o), o=off // tn)))
            args.append(_row(bias))
    return pl.pallas_call(
        functools.partial(_up_kernel, n_w=n_w, has_bias=bias is not None, epi=epi,
                          sub=256 if tn % 256 == 0 else LANE),
        grid=(m // tm, n_out // tn),
        in_specs=in_specs,
        out_specs=pl.BlockSpec((tm, tn), lambda i, j: (i, j)),
        out_shape=jax.ShapeDtypeStruct((m, n_out), out_dtype),
        compiler_params=_cp("arbitrary", "arbitrary"),
        name=name,
    )(*args)


def _up_norm_kernel(xc_ref, h0_ref, ng_ref, sc_ref, sh_ref, w_ref, o_ref, hs0_ref, hs1_ref, *, sub, nchunks):
    i = pl.program_id(0)
    j = pl.program_id(1)
    chunk = xc_ref.shape[0]

    @pl.when((i == 0) & (j == 0))
    def _():
        def copy(rows):
            hs0_ref[rows, :] = h0_ref[rows, :]

        _row_loop(h0_ref.shape[0], copy)

    def step(cur_ref, nxt_ref):
        r0 = pl.multiple_of(jnp.minimum(j, nchunks - 1) * chunk, chunk)
        nxt_ref[pl.ds(r0, chunk), :] = _prenorm_math(xc_ref[...], ng_ref[...], sc_ref[...], sh_ref[...]).astype(BF16)
        h = cur_ref[...]
        for n0 in range(0, o_ref.shape[1], sub):
            o_ref[:, n0:n0 + sub] = jnp.dot(h, w_ref[:, n0:n0 + sub],
                                            preferred_element_type=F32).astype(o_ref.dtype)

    parity = lax.rem(i, 2)

    @pl.when(parity == 0)
    def _():
        step(hs0_ref, hs1_ref)

    @pl.when(parity == 1)
    def _():
        step(hs1_ref, hs0_ref)


def _up_norm(x, norm, w, n_out, out_dtype, name):
    m, k = x.shape
    tm = _tile(m, UP_TM, SUBLANE)
    tn = _tile(n_out, 1024, LANE)
    chunk = min(tm, ROW_CHUNK)
    nchunks = tm // chunk
    n_i, n_j = m // tm, n_out // tn
    assert n_j >= nchunks or n_i == 1
    h0 = _prenorm(x[:tm], norm)
    vec = pl.BlockSpec((1, k), lambda i, j: (0, 0))

    def next_chunk(i, j):
        return jnp.minimum(i + 1, n_i - 1) * nchunks + jnp.minimum(j, nchunks - 1), 0

    return pl.pallas_call(
        functools.partial(_up_norm_kernel, sub=256 if tn % 256 == 0 else LANE, nchunks=nchunks),
        grid=(n_i, n_j),
        in_specs=[pl.BlockSpec((chunk, k), next_chunk),
                  pl.BlockSpec((tm, k), lambda i, j: (0, 0), pipeline_mode=pl.Buffered(1)),
                  vec, vec, vec,
                  pl.BlockSpec((k, tn), lambda i, j: (0, j))],
        out_specs=pl.BlockSpec((tm, tn), lambda i, j: (i, j)),
        out_shape=jax.ShapeDtypeStruct((m, n_out), out_dtype),
        scratch_shapes=[pltpu.VMEM((tm, k), BF16), pltpu.VMEM((tm, k), BF16)],
        compiler_params=_cp("arbitrary", "arbitrary"),
        name=name,
    )(x, h0, _row(norm[0]), _row(norm[1]), _row(norm[2]), w)


def _up_conv_kernel(h_ref, hp_ref, hn_ref, w_ref, cw_ref, cb_ref, o_ref, e0_ref, e1_ref, *, ntaps, sub):
    tm = h_ref.shape[0]
    pad = hp_ref.shape[0]
    half = ntaps // 2
    i = pl.program_id(0)
    first = i == 0
    last = i == pl.num_programs(0) - 1
    h, hp, hn = h_ref[...], hp_ref[...], hn_ref[...]
    rs = min(tm, CONV_ROWS)
    starts = list(range(0, o_ref.shape[1], sub))
    exts = (e0_ref, e1_ref)

    def project(c):
        ext_ref, wsub = exts[c % 2], w_ref[:, starts[c]:starts[c] + sub]
        ext_ref[pl.ds(0, pad), :] = jnp.where(first, 0.0, jnp.dot(hp, wsub, preferred_element_type=F32))
        ext_ref[pl.ds(pad, tm), :] = jnp.dot(h, wsub, preferred_element_type=F32)
        ext_ref[pl.ds(pad + tm, pad), :] = jnp.where(last, 0.0, jnp.dot(hn, wsub, preferred_element_type=F32))

    def conv(c):
        ext_ref, n0 = exts[c % 2], starts[c]
        bias = cb_ref[:, n0:n0 + sub]
        for r0 in range(0, tm, rs):
            acc = None
            for k in range(ntaps):
                v = ext_ref[pl.ds(pad + r0 + k - half, rs), :] * cw_ref[pl.ds(k, 1), n0:n0 + sub]
                acc = v if acc is None else acc + v
            o_ref[pl.ds(r0, rs), n0:n0 + sub] = _silu(acc + bias)

    project(0)
    for c in range(len(starts)):
        if c + 1 < len(starts):
            project(c + 1)
        conv(c)


def _up_conv(h, w, n_out, col_off, conv_w, conv_b, name):
    m, k = h.shape
    ntaps = conv_w.shape[0]
    tm = _tile(m, 1024, 2 * SUBLANE)
    tn = _tile(n_out, 1024, LANE)
    assert col_off % tn == 0
    pad = 2 * SUBLANE
    ratio = tm // pad
    nhb = m // pad
    sub = 256 if tn % 256 == 0 else LANE
    off = col_off // tn
    return pl.pallas_call(
        functools.partial(_up_conv_kernel, ntaps=ntaps, sub=sub),
        grid=(m // tm, n_out // tn),
        in_specs=[pl.BlockSpec((tm, k), lambda i, j: (i, 0)),
                  pl.BlockSpec((pad, k), lambda i, j: (jnp.maximum(i * ratio - 1, 0), 0)),
                  pl.BlockSpec((pad, k), lambda i, j: (jnp.minimum((i + 1) * ratio, nhb - 1), 0)),
                  pl.BlockSpec((k, tn), lambda i, j: (0, j + off)),
                  pl.BlockSpec((ntaps, tn), lambda i, j: (0, j)),
                  pl.BlockSpec((1, tn), lambda i, j: (0, j))],
        out_specs=pl.BlockSpec((tm, tn), lambda i, j: (i, j)),
        out_shape=jax.ShapeDtypeStruct((m, n_out), F32),
        scratch_shapes=[pltpu.VMEM((tm + 2 * pad, sub), F32), pltpu.VMEM((tm + 2 * pad, sub), F32)],
        compiler_params=_cp("arbitrary", "arbitrary"),
        name=name,
    )(h, h, h, w, conv_w, _row(conv_b))


def _seqconv_kernel(x_ref, xp_ref, xn_ref, w_ref, b_ref, o_ref, buf_ref, *, ntaps, act):
    t_rows, c = x_ref.shape
    pad = xp_ref.shape[0]
    i = pl.program_id(0)
    buf_ref[pl.ds(0, pad), :] = jnp.where(i > 0, xp_ref[...].astype(F32), 0.0)
    buf_ref[pl.ds(pad, t_rows), :] = x_ref[...].astype(F32)
    buf_ref[pl.ds(pad + t_rows, pad), :] = jnp.where(i < pl.num_programs(0) - 1, xn_ref[...].astype(F32), 0.0)
    half = ntaps // 2
    sub = min(t_rows, CONV_ROWS)
    lw = min(c, CONV_LANES)

    def body(ci, carry):
        lanes = pl.ds(pl.multiple_of(ci * lw, lw), lw)
        bias = b_ref[:, lanes]
        for s in range(t_rows // sub):
            acc = None
            for k in range(ntaps):
                v = buf_ref[pl.ds(pad + s * sub + k - half, sub), lanes] * w_ref[pl.ds(k, 1), lanes]
                acc = v if acc is None else acc + v
            acc = acc + bias
            if act == "silu":
                acc = _silu(acc)
            o_ref[pl.ds(s * sub, sub), lanes] = acc
        return carry

    lax.fori_loop(0, c // lw, body, 0)


def _seqconv(x, w, b, *, tile_rows, act, name):
    rows, c = x.shape
    ntaps = w.shape[0]
    pad = SUBLANE * (4 // x.dtype.itemsize)
    assert ntaps // 2 <= pad and rows % tile_rows == 0
    ratio = tile_rows // pad
    nhb = rows // pad
    return pl.pallas_call(
        functools.partial(_seqconv_kernel, ntaps=ntaps, act=act),
        grid=(rows // tile_rows,),
        in_specs=[pl.BlockSpec((tile_rows, c), lambda i: (i, 0)),
                  pl.BlockSpec((pad, c), lambda i: (jnp.maximum(i * ratio - 1, 0), 0)),
                  pl.BlockSpec((pad, c), lambda i: (jnp.minimum((i + 1) * ratio, nhb - 1), 0)),
                  pl.BlockSpec((ntaps, c), lambda i: (0, 0)),
                  pl.BlockSpec((1, c), lambda i: (0, 0))],
        out_specs=pl.BlockSpec((tile_rows, c), lambda i: (i, 0)),
        out_shape=jax.ShapeDtypeStruct((rows, c), F32),
        scratch_shapes=[pltpu.VMEM((tile_rows + 2 * pad, c), F32)],
        compiler_params=_cp("arbitrary"),
        name=name,
    )(x, x, x, w, _row(b))


def _rowconv_kernel(x_ref, w_ref, b_ref, g_ref, beta_ref, o_ref, pad_ref, ph_ref, y_ref, *, ntaps, seg):
    t_rows, c = x_ref.shape
    nseg = t_rows // seg
    lead = 2 * SUBLANE
    half = ntaps // 2
    amax = (lead - half + ntaps - 1) // SUBLANE
    span = seg + SUBLANE * amax
    stride = span + SUBLANE
    lw = ph_ref.shape[2]
    sub = min(seg, CONV_ROWS)
    for r in range(nseg):
        pad_ref[pl.ds(r * stride, lead), :] = jnp.zeros((lead, c), F32)
        pad_ref[pl.ds(r * stride + lead, seg), :] = x_ref[pl.ds(r * seg, seg), :]
        pad_ref[pl.ds(r * stride + lead + seg, stride - lead - seg), :] = jnp.zeros((stride - lead - seg, c), F32)

    def body(ci, carry):
        lanes = pl.ds(pl.multiple_of(ci * lw, lw), lw)
        bias = b_ref[:, lanes]
        for r in range(nseg):
            for b in range(1, SUBLANE):
                ph_ref[b] = pad_ref[pl.ds(r * stride + b, span), lanes]
            for s in range(0, seg, sub):
                acc = None
                for k in range(ntaps):
                    off = lead - half + k
                    a, b = off // SUBLANE, off % SUBLANE
                    if b == 0:
                        src = pad_ref[pl.ds(r * stride + SUBLANE * a + s, sub), lanes]
                    else:
                        src = ph_ref[b, pl.ds(SUBLANE * a + s, sub), :]
                    v = src * w_ref[pl.ds(k, 1), lanes]
                    acc = v if acc is None else acc + v
                y_ref[pl.ds(r * seg + s, sub), lanes] = acc + bias
        return carry

    lax.fori_loop(0, c // lw, body, 0)

    def norm_rows(rows):
        y = y_ref[rows, :]
        mu = jnp.mean(y, axis=-1, keepdims=True)
        yc = y - mu
        var = jnp.mean(yc * yc, axis=-1, keepdims=True)
        o_ref[rows, :] = _silu(yc * lax.rsqrt(var + NORM_EPS) * g_ref[...] + beta_ref[...]).astype(o_ref.dtype)

    _row_loop(t_rows, norm_rows)


def _rowconv(x, w, b, ln_g, ln_b, *, seg, tile_rows, name):
    rows, c = x.shape
    ntaps = w.shape[0]
    lead = 2 * SUBLANE
    assert ntaps // 2 <= lead and seg % SUBLANE == 0 and tile_rows % seg == 0
    amax = (lead - ntaps // 2 + ntaps - 1) // SUBLANE
    span = seg + SUBLANE * amax
    stride = span + SUBLANE
    lw = min(c, CONV_LANES)
    return pl.pallas_call(
        functools.partial(_rowconv_kernel, ntaps=ntaps, seg=seg),
        grid=(rows // tile_rows,),
        in_specs=[pl.BlockSpec((tile_rows, c), lambda i: (i, 0)),
                  pl.BlockSpec((ntaps, c), lambda i: (0, 0)),
                  pl.BlockSpec((1, c), lambda i: (0, 0)),
                  pl.BlockSpec((1, c), lambda i: (0, 0)),
                  pl.BlockSpec((1, c), lambda i: (0, 0))],
        out_specs=pl.BlockSpec((tile_rows, c), lambda i: (i, 0)),
        out_shape=jax.ShapeDtypeStruct((rows, c), BF16),
        scratch_shapes=[pltpu.VMEM(((tile_rows // seg) * stride, c), F32),
                        pltpu.VMEM((SUBLANE, span, lw), F32),
                        pltpu.VMEM((tile_rows, c), F32)],
        compiler_params=_cp("arbitrary"),
        name=name,
    )(x, w, _row(b), _row(ln_g), _row(ln_b))


def _colconv_kernel(x_ref, w_ref, b_ref, o_ref, buf_ref, *, ntaps):
    rows, width, lw = x_ref.shape
    half = ntaps // 2
    lead = buf_ref.shape[0] - rows - half
    zeros = jnp.zeros((width, lw), F32)
    for r in range(lead):
        buf_ref[r] = zeros
    for r in range(half):
        buf_ref[lead + rows + r] = zeros

    def copy(r, carry):
        buf_ref[lead + r] = x_ref[r]
        return carry

    lax.fori_loop(0, rows, copy, 0)

    def body(r, carry):
        acc = None
        for k in range(ntaps):
            v = buf_ref[lead - half + r + k] * w_ref[pl.ds(k, 1), :]
            acc = v if acc is None else acc + v
        o_ref[r] = acc + b_ref[...]
        return carry

    lax.fori_loop(0, rows, body, 0)


def _colconv(x3, w, b, name):
    rows, width, c = x3.shape
    ntaps = w.shape[0]
    half = ntaps // 2
    lw = min(c, LANE)
    return pl.pallas_call(
        functools.partial(_colconv_kernel, ntaps=ntaps),
        grid=(c // lw,),
        in_specs=[pl.BlockSpec((rows, width, lw), lambda j: (0, 0, j)),
                  pl.BlockSpec((ntaps, lw), lambda j: (0, j)),
                  pl.BlockSpec((1, lw), lambda j: (0, j))],
        out_specs=pl.BlockSpec((rows, width, lw), lambda j: (0, 0, j)),
        out_shape=jax.ShapeDtypeStruct((rows, width, c), F32),
        scratch_shapes=[pltpu.VMEM((rows + 2 * half, width, lw), F32)],
        compiler_params=_cp("arbitrary"),
        name=name,
    )(x3, w, _row(b))


def _down_mm_kernel(*refs, has_bias):
    if has_bias:
        a_ref, w_ref, bias_ref, x_ref, gate_ref, o_ref = refs
    else:
        a_ref, w_ref, x_ref, gate_ref, o_ref = refs
    k = pl.program_id(2)
    part = jnp.dot(a_ref[...], w_ref[...], preferred_element_type=F32)

    @pl.when(k == 0)
    def _():
        o_ref[...] = part

    @pl.when(k > 0)
    def _():
        o_ref[...] += part

    @pl.when(k == pl.num_programs(2) - 1)
    def _():
        y = o_ref[...]
        if has_bias:
            y = y + bias_ref[...]
        o_ref[...] = x_ref[...] + gate_ref[...] * y


def _down_mm(a, w, bias, x, gate, name):
    m, kdim = a.shape
    d = w.shape[1]
    tm = _tile(m, 1024, SUBLANE)
    tn = _tile(d, 1024, LANE)
    tk = _tile(kdim, 2048, LANE)
    vec = pl.BlockSpec((1, tn), lambda i, j, k: (0, j))
    in_specs = [pl.BlockSpec((tm, tk), lambda i, j, k: (i, k)),
                pl.BlockSpec((tk, tn), lambda i, j, k: (k, j))]
    args = [a, w]
    if bias is not None:
        in_specs.append(vec)
        args.append(_row(bias))
    in_specs += [pl.BlockSpec((tm, tn), lambda i, j, k: (i, j)), vec]
    args += [x, _row(gate)]
    return pl.pallas_call(
        functools.partial(_down_mm_kernel, has_bias=bias is not None),
        grid=(m // tm, d // tn, kdim // tk),
        in_specs=in_specs,
        out_specs=pl.BlockSpec((tm, tn), lambda i, j, k: (i, j)),
        out_shape=jax.ShapeDtypeStruct((m, d), F32),
        compiler_params=_cp("arbitrary", "arbitrary", "arbitrary"),
        name=name,
    )(*args)


def _ffn_down_kernel(gc_ref, gp_ref, gn_ref, up_ref, dw_ref, dwb_ref, w_ref, x_ref, gate_ref, ng_ref, sc_ref,
                     sh_ref, *rest, on_grid, final, n_i, nk):
    if final:
        o_ref, buf_ref, a0_ref, a1_ref = rest
        h_ref = None
    else:
        o_ref, h_ref, buf_ref, a0_ref, a1_ref = rest
    tm, tk = gc_ref.shape
    hb = gp_ref.shape[0]
    d = o_ref.shape[1]
    s = pl.program_id(0)
    ip = jnp.minimum(s, n_i * nk - 1) // nk
    m = jnp.maximum(s - 1, 0)
    km = lax.rem(m, nk)

    @pl.when(s == 0)
    def _():
        a1_ref[...] = jnp.zeros_like(a1_ref)

    @pl.when(km == 0)
    def _():
        def zero(rows):
            o_ref[rows, :] = jnp.zeros((min(tm, ROW_CHUNK), d), F32)

        _row_loop(tm, zero)

    sub = min(tm, CONV_ROWS)
    lw = min(tk, CONV_LANES)
    dys = (-1, 0, 1) if on_grid else (0,)
    tn = min(d, 256)

    def step(dst_ref, src_ref):
        buf_ref[pl.ds(0, hb), :] = jnp.where(ip > 0, gp_ref[...].astype(F32), 0.0)
        buf_ref[pl.ds(hb, tm), :] = gc_ref[...].astype(F32)
        buf_ref[pl.ds(hb + tm, hb), :] = jnp.where(ip < n_i - 1, gn_ref[...].astype(F32), 0.0)
        col = lax.broadcasted_iota(jnp.int32, (sub, lw), 0)

        def build(lc, r):
            lanes = pl.ds(lc * lw, lw)
            base = hb + r * sub

            def tapsum(dx):
                acc = None
                for dy in dys:
                    v = (buf_ref[pl.ds(base + GRID_W * dy + dx, sub), lanes]
                         * dw_ref[pl.ds((dy + 1) * 3 + dx + 1, 1), lanes])
                    acc = v if acc is None else acc + v
                return acc

            left, right = tapsum(-1), tapsum(1)
            if on_grid:
                left = jnp.where(col >= 1, left, 0.0)
                right = jnp.where(col <= GRID_W - 2, right, 0.0)
            conv = tapsum(0) + left + right + dwb_ref[:, lanes]
            dst_ref[pl.ds(r * sub, sub), lanes] = (
                jax.nn.gelu(conv) * up_ref[pl.ds(r * sub, sub), lanes].astype(F32)).astype(BF16)

        blocks = [(lc, r) for lc in range(tk // lw) for r in range(tm // sub)]
        n_mm = d // tn
        per = -(-len(blocks) // n_mm)
        a = src_ref[...]
        for c in range(n_mm):
            n0 = c * tn
            o_ref[:, n0:n0 + tn] += jnp.dot(a, w_ref[:, n0:n0 + tn], preferred_element_type=F32)
            for lc, r in blocks[c * per:(c + 1) * per]:
                build(lc, r)

    parity = lax.rem(s, 2)

    @pl.when(parity == 0)
    def _():
        step(a0_ref, a1_ref)

    @pl.when(parity == 1)
    def _():
        step(a1_ref, a0_ref)

    @pl.when((km == nk - 1) & (s > 0))
    def _():
        def finish(rows):
            xn = x_ref[rows, :] + gate_ref[...] * o_ref[rows, :]
            if final:
                ms = jnp.mean(xn * xn, axis=-1, keepdims=True)
                o_ref[rows, :] = (xn * lax.rsqrt(ms + NORM_EPS)) * ng_ref[...]
            else:
                o_ref[rows, :] = xn
                h_ref[rows, :] = _prenorm_math(xn, ng_ref[...], sc_ref[...], sh_ref[...]).astype(BF16)

        _row_loop(tm, finish)


def _down_ffn(a, dw, dw_b, w, x, gate, norm, on_grid, final, name):
    ng, sc, sh = norm
    m = a.shape[0]
    f, d = w.shape
    tm = _tile(m, 512, 2 * GRID_W)
    tk = _tile(f, 512, LANE)
    hb = 2 * GRID_W
    ratio = tm // hb
    nhb = m // hb
    koff = f // tk
    n_i, nk = m // tm, f // tk
    last = n_i * nk - 1

    def pro(s):
        p = jnp.minimum(s, last)
        return p // nk, lax.rem(p, nk)

    def mat(s):
        q = jnp.maximum(s - 1, 0)
        return q // nk, lax.rem(q, nk)

    vec = pl.BlockSpec((1, d), lambda s: (0, 0))
    rowblk = pl.BlockSpec((tm, d), lambda s: (mat(s)[0], 0))
    in_specs = [pl.BlockSpec((tm, tk), lambda s: pro(s)),
                pl.BlockSpec((hb, tk), lambda s: (jnp.maximum(pro(s)[0] * ratio - 1, 0), pro(s)[1])),
                pl.BlockSpec((hb, tk), lambda s: (jnp.minimum((pro(s)[0] + 1) * ratio, nhb - 1), pro(s)[1])),
                pl.BlockSpec((tm, tk), lambda s: (pro(s)[0], pro(s)[1] + koff)),
                pl.BlockSpec((9, tk), lambda s: (0, pro(s)[1])),
                pl.BlockSpec((1, tk), lambda s: (0, pro(s)[1])),
                pl.BlockSpec((tk, d), lambda s: (mat(s)[1], 0)),
                rowblk, vec, vec, vec, vec]
    out_specs = [rowblk]
    out_shape = [jax.ShapeDtypeStruct((m, d), F32)]
    if not final:
        out_specs.append(rowblk)
        out_shape.append(jax.ShapeDtypeStruct((m, d), BF16))
    res = pl.pallas_call(
        functools.partial(_ffn_down_kernel, on_grid=on_grid, final=final, n_i=n_i, nk=nk),
        grid=(n_i * nk + 1,),
        in_specs=in_specs,
        out_specs=out_specs,
        out_shape=out_shape,
        scratch_shapes=[pltpu.VMEM((tm + 2 * hb, tk), F32), pltpu.VMEM((tm, tk), BF16),
                        pltpu.VMEM((tm, tk), BF16)],
        compiler_params=_cp("arbitrary"),
        name=name,
    )(a, a, a, a, dw.reshape(9, f), _row(dw_b), w, x, _row(gate), _row(ng), _row(sc), _row(sh))
    return res[0] if final else res


def _sgu_kernel(u_ref, v_ref, g_ref, b_ref, ws_ref, bs_ref, o_ref):
    v = v_ref[...].astype(F32)
    mu = jnp.mean(v, axis=-1, keepdims=True)
    vc = v - mu
    var = jnp.mean(vc * vc, axis=-1, keepdims=True)
    vn = (vc * lax.rsqrt(var + NORM_EPS) * g_ref[...] + b_ref[...]).astype(BF16)
    groups = ws_ref.shape[0]
    gw = v.shape[1] // groups
    for g in range(groups):
        mixed = jnp.dot(ws_ref[g], vn[:, g * gw:(g + 1) * gw], preferred_element_type=F32) + bs_ref[g]
        o_ref[:, g * gw:(g + 1) * gw] = (u_ref[:, g * gw:(g + 1) * gw].astype(F32) * mixed).astype(o_ref.dtype)


def _sgu_mix(a, ln_g, ln_b, ws, bs):
    m = a.shape[0]
    e = a.shape[1] // 2
    q = SGU_CHUNK
    groups = ws.shape[0]
    return pl.pallas_call(
        _sgu_kernel,
        grid=(m // q,),
        in_specs=[pl.BlockSpec((q, e), lambda c: (c, 0)),
                  pl.BlockSpec((q, e), lambda c: (c, 1)),
                  pl.BlockSpec((1, e), lambda c: (0, 0)),
                  pl.BlockSpec((1, e), lambda c: (0, 0)),
                  pl.BlockSpec((groups, q, q), lambda c: (0, 0, 0)),
                  pl.BlockSpec((groups, q, 1), lambda c: (0, 0, 0))],
        out_specs=pl.BlockSpec((q, e), lambda c: (c, 0)),
        out_shape=jax.ShapeDtypeStruct((m, e), BF16),
        compiler_params=_cp("arbitrary"),
        name="sgu_mix",
    )(a, a, _row(ln_g), _row(ln_b), ws.astype(BF16), bs.reshape(groups, q, 1))


def _split3(a):
    a1 = a.astype(BF16)
    r1 = a - a1.astype(F32)
    a2 = r1.astype(BF16)
    a3 = (r1 - a2.astype(F32)).astype(BF16)
    return [a1, a2, a3]


def _ssd_kernel(*refs, reverse, mode):
    x_ref, b_ref, c_ref, dt_ref, dtt_ref, al_ref, alt_ref, s0_ref = refs[:8]
    if mode == "gated":
        yprev_ref, d_ref, z_ref, ng_ref, y_ref, sfin_ref, st_ref, yacc_ref = refs[8:]
    elif mode == "y":
        y_ref, sfin_ref, st_ref = refs[8:]
    else:
        sfin_ref, st_ref = refs[8:]
    c = pl.program_id(1)

    @pl.when(c == 0)
    def _():
        st_ref[...] = s0_ref[0]

    q = SSD_CHUNK
    hpg = dt_ref.shape[3]
    pw = 2 * SSD_HEADDIM
    row = lax.broadcasted_iota(jnp.int32, (q, q), 0)
    col = lax.broadcasted_iota(jnp.int32, (q, q), 1)
    mask = (row <= col) if reverse else (row >= col)
    mask_t = (row >= col) if reverse else (row <= col)
    mask3 = jnp.concatenate([mask.astype(BF16)] * 3, axis=1)
    mask3_t = jnp.concatenate([mask_t.astype(BF16)] * 3, axis=0)
    lo = lax.broadcasted_iota(jnp.int32, (1, pw), 1) < SSD_HEADDIM
    neg_a = -jnp.exp(al_ref[0, 0])
    neg_a_t = -jnp.exp(alt_ref[0, 0])

    def halves(a):
        return jnp.concatenate([jnp.where(lo, a, 0.0), jnp.where(lo, 0.0, a)], axis=0).astype(BF16)

    def chunk(r0):
        rows = pl.ds(r0, q)
        dtt = dtt_ref[0, 0][:, r0:r0 + q]
        dtat = dtt * neg_a_t
        cumt = jnp.dot(jnp.concatenate(_split3(dtat), axis=1), mask3_t,
                       preferred_element_type=F32)
        lastt = jnp.sum(dtat, axis=1, keepdims=True)
        vt = dtt * jnp.exp(lastt - cumt)
        dta = dt_ref[0, 0, rows, :] * neg_a
        elast = jnp.exp(jnp.sum(dta, axis=0, keepdims=True))
        bmat = b_ref[rows, :]
        bt = bmat.T
        if mode != "state":
            cum = jnp.dot(mask3, jnp.concatenate(_split3(dta), axis=0), preferred_element_type=F32)
            cmat = c_ref[rows, :]
            cb = lax.dot_general(cmat.astype(BF16), bmat.astype(BF16), (((1,), (1,)), ((), ())),
                                 preferred_element_type=F32)

        for j in range(hpg // 2):
            h0, h1 = 2 * j, 2 * j + 1
            cols = slice(j * pw, (j + 1) * pw)
            xs = x_ref[rows, cols]
            xcat = halves(xs)
            st = st_ref[j]
            btv = jnp.concatenate([bt * vt[h0:h0 + 1, :], bt * vt[h1:h1 + 1, :]], axis=1).astype(BF16)
            st_ref[j] = (st * jnp.where(lo, elast[:, h0:h0 + 1], elast[:, h1:h1 + 1])
                         + jnp.dot(btv, xcat, preferred_element_type=F32))
            if mode == "state":
                continue

            def lhs_parts(h):
                colb = jnp.broadcast_to(cum[:, h:h + 1], (q, q))
                e = jnp.where(mask, jnp.exp(colb - cumt[h:h + 1, :]), 0.0)
                return (cb * e * dtt[h:h + 1, :]).astype(BF16), (cmat * jnp.exp(colb)).astype(BF16)

            m0, c0 = lhs_parts(h0)
            m1, c1 = lhs_parts(h1)
            lhs = jnp.concatenate([m0, m1, c0, c1], axis=1)
            rhs = jnp.concatenate([xcat, halves(st)], axis=0)
            y = jnp.dot(lhs, rhs, preferred_element_type=F32)
            if mode == "gated":
                dlane = jnp.where(lo, d_ref[0][:, h0:h0 + 1], d_ref[0][:, h1:h1 + 1])
                yacc_ref[rows, cols] = y + yprev_ref[rows, cols] + xs * dlane
            else:
                y_ref[rows, cols] = y

        if mode == "gated":
            yz = yacc_ref[rows, :] * _silu(z_ref[rows, :].astype(F32))
            ms = jnp.mean(yz * yz, axis=-1, keepdims=True)
            y_ref[rows, :] = (yz * lax.rsqrt(ms + NORM_EPS) * ng_ref[...]).astype(y_ref.dtype)

    starts = list(range(0, x_ref.shape[0], q))
    for r0 in (reversed(starts) if reverse else starts):
        chunk(r0)

    @pl.when(c == pl.num_programs(1) - 1)
    def _():
        sfin_ref[0] = st_ref[...]


def _ssd_scan(xbc, dt_a, dt_b, al_a, al_b, s0, direction, mode, extra=()):
    length = xbc.shape[0]
    groups, hpg = dt_a.shape[1], dt_a.shape[3]
    q = SSD_CHUNK * min(SSD_BLOCK, length // SSD_CHUNK)
    n = SSD_STATE
    pw = 2 * SSD_HEADDIM
    gw = hpg * SSD_HEADDIM
    d_inner = groups * gw
    assert length % q == 0
    nc = length // q
    reverse = direction == 1
    boff = d_inner // n

    def cc(c):
        return nc - 1 - c if reverse else c

    xblk = pl.BlockSpec((q, gw), lambda g, c: (cc(c), g))
    stblk = pl.BlockSpec((1, hpg // 2, n, pw), lambda g, c: (g, 0, 0, 0))
    in_specs = [xblk,
                pl.BlockSpec((q, n), lambda g, c: (cc(c), boff + g)),
                pl.BlockSpec((q, n), lambda g, c: (cc(c), boff + groups + g)),
                pl.BlockSpec((1, 1, q, hpg), lambda g, c: (direction, g, cc(c), 0)),
                pl.BlockSpec((1, 1, hpg, q), lambda g, c: (direction, g, 0, cc(c))),
                pl.BlockSpec((1, 1, 1, hpg), lambda g, c: (direction, g, 0, 0)),
                pl.BlockSpec((1, 1, hpg, 1), lambda g, c: (direction, g, 0, 0)),
                stblk]
    args = [xbc, xbc, xbc, dt_a, dt_b, al_a, al_b, s0]
    out_specs, out_shape = [], []
    scratch = [pltpu.VMEM((hpg // 2, n, pw), F32)]
    if mode == "gated":
        yprev, dskip, z, norm_g = extra
        in_specs += [xblk, pl.BlockSpec((1, 1, hpg), lambda g, c: (g, 0, 0)), xblk,
                     pl.BlockSpec((1, gw), lambda g, c: (0, g))]
        args += [yprev, dskip, z, _row(norm_g)]
        scratch.append(pltpu.VMEM((q, gw), F32))
    if mode != "state":
        out_specs.append(xblk)
        out_shape.append(jax.ShapeDtypeStruct((length, d_inner), BF16 if mode == "gated" else F32))
    out_specs.append(stblk)
    out_shape.append(jax.ShapeDtypeStruct(s0.shape, F32))
    return pl.pallas_call(
        functools.partial(_ssd_kernel, reverse=reverse, mode=mode),
        grid=(groups, nc),
        in_specs=in_specs,
        out_specs=out_specs,
        out_shape=out_shape,
        scratch_shapes=scratch,
        compiler_params=_cp("arbitrary", "arbitrary"),
        name="ssd_scan_%s_%s" % ("bwd" if reverse else "fwd", mode),
    )(*args)


def _ssd_inputs(h, w_bf, conv_w, conv_b, dt_bias, d_inner, conv_dim, heads2, name):
    length = h.shape[0]
    xbc = _up_conv(h, w_bf, conv_dim, d_inner, conv_w, conv_b, name + "_xbc")
    dt = _up(h, w_bf[:, d_inner + conv_dim:], heads2, [0], dt_bias.reshape(-1), _epi_softplus, F32,
             name + "_dt", tn_pref=heads2)
    hpg = heads2 // 2 // SSD_GROUPS
    dt4 = dt.reshape(length, 2, SSD_GROUPS, hpg)
    return xbc, jnp.transpose(dt4, (1, 2, 0, 3)), jnp.transpose(dt4, (1, 2, 3, 0))


def kernel(x, c, ctx, c_ctx, norm_mix_g, norm_ffn_g, mod_wa, mod_wb, mod_b, conv_w1, conv_b1, conv_dw, conv_dw_b, conv_ln_g, conv_ln_b, conv_w2, conv_b2, ssd_w_in, ssd_conv_w, ssd_conv_b, ssd_a_log, ssd_dt_bias, ssd_d, ssd_norm_g, ssd_w_out, sgu_w1, sgu_b1, sgu_ln_g, sgu_ln_b, sgu_ws, sgu_bs, sgu_w2, sgu_b2, ffn_w_in, ffn_dw, ffn_dw_b, ffn_w_out, final_g):
    bsz, seq, d = x.shape
    assert bsz == 1
    depth = norm_mix_g.shape[0]
    xl = x.reshape(seq, d)
    xc = ctx.reshape(ctx.shape[1], d)
    grid_rows = seq // GRID_W
    f_hidden = ffn_w_out.shape[1]

    v8 = jnp.zeros((SUBLANE, d), F32).at[0].set(c[0]).at[1].set(c_ctx)
    mod = _adaln(v8, mod_wa, mod_wb, mod_b)

    def norms(layer, row):
        sh1, sc1, g1, sh2, sc2, g2 = [mod[layer, row, t * d:(t + 1) * d] for t in range(6)]
        return (norm_mix_g[layer], sc1, sh1), (norm_ffn_g[layer], sc2, sh2), g1, g2

    def next_norm(layer, row):
        if layer + 1 < depth:
            return norms(layer + 1, row)[0]
        zeros = jnp.zeros((d,), F32)
        return final_g, zeros, zeros

    ssd_layers = [i for i in range(depth) if i % DEPTH_MIXERS == 1]
    last_ctx = ssd_layers[-1] if ssd_layers else -1

    hl = _prenorm(xl, norms(0, 0)[0])
    hc = _prenorm(xc, norms(0, 1)[0]) if last_ctx >= 0 else None
    for i in range(depth):
        kind, k = i % DEPTH_MIXERS, i // DEPTH_MIXERS
        ctx_full = i < last_ctx
        _, norm2, g1, g2 = norms(i, 0)
        if i <= last_ctx:
            _, cnorm2, cg1, cg2 = norms(i, 1)
        tag = "l%d" % i

        if kind == 0:
            w1 = conv_w1[k].astype(BF16)
            w2 = conv_w2[k].astype(BF16)
            gl = _up(hl, w1, d, [0, d], conv_b1[k], _epi_glu, F32, tag + "_conv_up")
            if k % 2 == 0:
                al = _rowconv(gl, conv_dw[k], conv_dw_b[k], conv_ln_g[k], conv_ln_b[k], seg=GRID_W,
                              tile_rows=2 * GRID_W, name=tag + "_rowconv")
            else:
                gl = _colconv(gl.reshape(grid_rows, GRID_W, d), conv_dw[k], conv_dw_b[k],
                              tag + "_colconv").reshape(seq, d)
                al = _ln_silu(gl, conv_ln_g[k], conv_ln_b[k])
            xl = _down_mm(al, w2, conv_b2[k], xl, g1, tag + "_conv_down")
            if ctx_full:
                gc = _up(hc, w1, d, [0, d], conv_b1[k], _epi_glu, F32, tag + "_conv_up_ctx")
                ac = _rowconv(gc, conv_dw[k], conv_dw_b[k], conv_ln_g[k], conv_ln_b[k], seg=gc.shape[0],
                              tile_rows=gc.shape[0], name=tag + "_seqconv_ctx")
                xc = _down_mm(ac, w2, conv_b2[k], xc, cg1, tag + "_conv_down_ctx")
        elif kind == 1:
            assert not ctx_full
            w_in = ssd_w_in[k].astype(BF16)
            w_out = ssd_w_out[k].astype(BF16)
            d_inner = w_out.shape[0]
            heads2 = 2 * ssd_a_log.shape[2]
            conv_dim = w_in.shape[1] - d_inner - heads2
            hpg = heads2 // 2 // SSD_GROUPS
            al_a = ssd_a_log[k].reshape(2, SSD_GROUPS, 1, hpg)
            al_b = ssd_a_log[k].reshape(2, SSD_GROUPS, hpg, 1)
            dskip = ssd_d[k].reshape(SSD_GROUPS, 1, hpg)
            zero_state = jnp.zeros((SSD_GROUPS, hpg // 2, SSD_STATE, 2 * SSD_HEADDIM), F32)
            xbc_c, dta_c, dtb_c = _ssd_inputs(hc, w_in, ssd_conv_w[k], ssd_conv_b[k], ssd_dt_bias[k],
                                              d_inner, conv_dim, heads2, tag + "_ssd_ctx")
            (state_f,) = _ssd_scan(xbc_c, dta_c, dtb_c, al_a, al_b, zero_state, 0, "state")
            (state_b,) = _ssd_scan(xbc_c, dta_c, dtb_c, al_a, al_b, zero_state, 1, "state")
            xbc_l, dta_l, dtb_l = _ssd_inputs(hl, w_in, ssd_conv_w[k], ssd_conv_b[k], ssd_dt_bias[k],
                                              d_inner, conv_dim, heads2, tag + "_ssd")
            z = _up(hl, w_in, d_inner, [0], None, _epi_id, BF16, tag + "_ssd_z")
            y_f, _ = _ssd_scan(xbc_l, dta_l, dtb_l, al_a, al_b, state_f, 0, "y")
            p, _ = _ssd_scan(xbc_l, dta_l, dtb_l, al_a, al_b, state_b, 1, "gated",
                             (y_f, dskip, z, ssd_norm_g[k]))
            xl = _down_mm(p, w_out, None, xl, g1, tag + "_ssd_down")
        else:
            w1 = sgu_w1[k].astype(BF16)
            w2 = sgu_w2[k].astype(BF16)

            def sgu(h, name):
                a = _up(h, w1, w1.shape[1], [0], sgu_b1[k], _epi_gelu, BF16, name + "_up")
                return _sgu_mix(a, sgu_ln_g[k], sgu_ln_b[k], sgu_ws[k], sgu_bs[k])

            xl = _down_mm(sgu(hl, tag + "_sgu"), w2, sgu_b2[k], xl, g1, tag + "_sgu_down")
            if ctx_full:
                xc = _down_mm(sgu(hc, tag + "_sgu_ctx"), w2, sgu_b2[k], xc, cg1, tag + "_sgu_down_ctx")

        fw_in = ffn_w_in[i].astype(BF16)
        fw_out = ffn_w_out[i].astype(BF16)
        final = i == depth - 1
        a = _up_norm(xl, norm2, fw_in, 2 * f_hidden, BF16, tag + "_ffn_up")
        res = _down_ffn(a, ffn_dw[i], ffn_dw_b[i], fw_out, xl, g2, next_norm(i, 0), True, final, tag + "_ffn_down")
        if final:
            xl = res
        else:
            xl, hl = res
        if ctx_full:
            ac = _up_norm(xc, cnorm2, fw_in, 2 * f_hidden, BF16, tag + "_ffn_up_ctx")
            xc, hc = _down_ffn(ac, ffn_dw[i], ffn_dw_b[i], fw_out, xc, cg2, next_norm(i, 1), False, False,
                               tag + "_ffn_down_ctx")
    return xl.reshape(bsz, seq, d)
```

```python
import functools

import jax
import jax.numpy as jnp
from jax import lax
from jax.experimental import pallas as pl
from jax.experimental.pallas import tpu as pltpu

F32 = jnp.float32
BF16 = jnp.bfloat16
HI = lax.Precision.HIGHEST

NORM_EPS = 1e-6
GRID_W = 64
DEPTH_MIXERS = 3
SSD_GROUPS = 8
SSD_HEADDIM = 64
SSD_STATE = 128
SSD_CHUNK = 128
SSD_BLOCK = 4
SGU_CHUNK = 128
SGU_GROUPS = 16
LANE = 128
SUBLANE = 8
CONV_LANES = 256
CONV_ROWS = 64
UP_TM = 1024
ROW_CHUNK = 64
VMEM_LIMIT = 58 * 1024 * 1024


def _cp(*sem):
    return pltpu.CompilerParams(dimension_semantics=sem, vmem_limit_bytes=VMEM_LIMIT)


def _tile(n, pref, align):
    if n <= pref:
        return n
    t = (pref // align) * align
    while t >= align:
        if n % t == 0:
            return t
        t -= align
    return n


def _row(v):
    return v.reshape(1, -1)


def _prenorm_math(x, g, sc, sh):
    ms = jnp.mean(x * x, axis=-1, keepdims=True)
    return (x * lax.rsqrt(ms + NORM_EPS)) * g * (1.0 + sc) + sh


def _silu(x):
    return x * jax.nn.sigmoid(x)


def _row_loop(nrows, fn):
    step = min(nrows, ROW_CHUNK)

    def body(r, carry):
        fn(pl.ds(pl.multiple_of(r * step, step), step))
        return carry

    lax.fori_loop(0, nrows // step, body, 0)


def _adaln_kernel(v_ref, wa_ref, wb_ref, b_ref, o_ref, t_ref):
    @pl.when(pl.program_id(1) == 0)
    def _():
        t_ref[...] = jnp.dot(_silu(v_ref[...]), wa_ref[0], preferred_element_type=F32, precision=HI)

    o_ref[0] = jnp.dot(t_ref[...], wb_ref[0], preferred_element_type=F32, precision=HI) + b_ref[0]


def _adaln(v8, wa, wb, b):
    depth, d, r = wa.shape
    n = wb.shape[2]
    tn = _tile(n, 6144, LANE)
    return pl.pallas_call(
        _adaln_kernel,
        grid=(depth, n // tn),
        in_specs=[pl.BlockSpec((SUBLANE, d), lambda l, j: (0, 0)),
                  pl.BlockSpec((1, d, r), lambda l, j: (l, 0, 0)),
                  pl.BlockSpec((1, r, tn), lambda l, j: (l, 0, j)),
                  pl.BlockSpec((1, 1, tn), lambda l, j: (l, 0, j))],
        out_specs=pl.BlockSpec((1, SUBLANE, tn), lambda l, j: (l, 0, j)),
        out_shape=jax.ShapeDtypeStruct((depth, SUBLANE, n), F32),
        scratch_shapes=[pltpu.VMEM((SUBLANE, r), F32)],
        compiler_params=_cp("arbitrary", "arbitrary"),
        name="adaln",
    )(v8, wa, wb, b.reshape(depth, 1, n))


def _prenorm_kernel(x_ref, g_ref, sc_ref, sh_ref, o_ref):
    o_ref[...] = _prenorm_math(x_ref[...], g_ref[...], sc_ref[...], sh_ref[...]).astype(o_ref.dtype)


def _prenorm(x, norm):
    g, sc, sh = norm
    m, d = x.shape
    tm = _tile(m, 256, SUBLANE)
    vec = pl.BlockSpec((1, d), lambda i: (0, 0))
    return pl.pallas_call(
        _prenorm_kernel,
        grid=(m // tm,),
        in_specs=[pl.BlockSpec((tm, d), lambda i: (i, 0)), vec, vec, vec],
        out_specs=pl.BlockSpec((tm, d), lambda i: (i, 0)),
        out_shape=jax.ShapeDtypeStruct((m, d), BF16),
        compiler_params=_cp("arbitrary"),
        name="prenorm",
    )(x, _row(g), _row(sc), _row(sh))


def _ln_silu_kernel(x_ref, g_ref, b_ref, o_ref):
    x = x_ref[...]
    mu = jnp.mean(x, axis=-1, keepdims=True)
    xc = x - mu
    var = jnp.mean(xc * xc, axis=-1, keepdims=True)
    y = xc * lax.rsqrt(var + NORM_EPS) * g_ref[...] + b_ref[...]
    o_ref[...] = _silu(y).astype(o_ref.dtype)


def _ln_silu(x, g, b):
    m, d = x.shape
    tm = _tile(m, 256, SUBLANE)
    vec = pl.BlockSpec((1, d), lambda i: (0, 0))
    return pl.pallas_call(
        _ln_silu_kernel,
        grid=(m // tm,),
        in_specs=[pl.BlockSpec((tm, d), lambda i: (i, 0)), vec, vec],
        out_specs=pl.BlockSpec((tm, d), lambda i: (i, 0)),
        out_shape=jax.ShapeDtypeStruct((m, d), BF16),
        compiler_params=_cp("arbitrary"),
        name="ln_silu",
    )(x, _row(g), _row(b))


def _up_kernel(*refs, n_w, has_bias, epi, sub):
    h_ref = refs[0]
    w_refs = refs[1:1 + n_w]
    b_refs = refs[1 + n_w:1 + 2 * n_w] if has_bias else ()
    o_ref = refs[-1]
    h = h_ref[...]
    for n0 in range(0, o_ref.shape[1], sub):
        accs = []
        for t in range(n_w):
            a = jnp.dot(h, w_refs[t][:, n0:n0 + sub], preferred_element_type=F32)
            if has_bias:
                a = a + b_refs[t][:, n0:n0 + sub]
            accs.append(a)
        o_ref[:, n0:n0 + sub] = epi(*accs).astype(o_ref.dtype)


def _epi_id(a):
    return a


def _epi_glu(a, b):
    return a * jax.nn.sigmoid(b)


def _epi_gelu(a):
    return jax.nn.gelu(a)


def _epi_softplus(a):
    return jnp.maximum(a, 0.0) + jnp.log1p(jnp.exp(-jnp.abs(a)))


def _up(h, w, n_out, col_offs, bias, epi, out_dtype, name, tn_pref=1024):
    m, k = h.shape
    tm = _tile(m, UP_TM, SUBLANE)
    tn = _tile(n_out, tn_pref // len(col_offs), LANE)
    for off in col_offs:
        assert off % tn == 0
    n_w = len(col_offs)
    in_specs = [pl.BlockSpec((tm, k), lambda i, j: (i, 0))]
    args = [h]
    for off in col_offs:
        in_specs.append(pl.BlockSpec((k, tn), functools.partial(lambda i, j, o: (0, j + o), o=off // tn)))
        args.append(w)
    if bias is not None:
        for off in col_offs:
            in_specs.append(pl.BlockSpec((1, tn), functools.partial(lambda i, j, o: (0, j + o), o=off // tn)))
            args.append(_row(bias))
    return pl.pallas_call(
        functools.partial(_up_kernel, n_w=n_w, has_bias=bias is not None, epi=epi,
                          sub=256 if tn % 256 == 0 else LANE),
        grid=(m // tm, n_out // tn),
        in_specs=in_specs,
        out_specs=pl.BlockSpec((tm, tn), lambda i, j: (i, j)),
        out_shape=jax.ShapeDtypeStruct((m, n_out), out_dtype),
        compiler_params=_cp("arbitrary", "arbitrary"),
        name=name,
    )(*args)


def _up_norm_kernel(xc_ref, h0_ref, ng_ref, sc_ref, sh_ref, w_ref, o_ref, hs0_ref, hs1_ref, *, sub, nchunks):
    i = pl.program_id(0)
    j = pl.program_id(1)
    chunk = xc_ref.shape[0]

    @pl.when((i == 0) & (j == 0))
    def _():
        def copy(rows):
            hs0_ref[rows, :] = h0_ref[rows, :]

        _row_loop(h0_ref.shape[0], copy)

    def step(cur_ref, nxt_ref):
        r0 = pl.multiple_of(jnp.minimum(j, nchunks - 1) * chunk, chunk)
        nxt_ref[pl.ds(r0, chunk), :] = _prenorm_math(xc_ref[...], ng_ref[...], sc_ref[...], sh_ref[...]).astype(BF16)
        h = cur_ref[...]
        for n0 in range(0, o_ref.shape[1], sub):
            o_ref[:, n0:n0 + sub] = jnp.dot(h, w_ref[:, n0:n0 + sub],
                                            preferred_element_type=F32).astype(o_ref.dtype)

    parity = lax.rem(i, 2)

    @pl.when(parity == 0)
    def _():
        step(hs0_ref, hs1_ref)

    @pl.when(parity == 1)
    def _():
        step(hs1_ref, hs0_ref)


def _up_norm(x, norm, w, n_out, out_dtype, name):
    m, k = x.shape
    tm = _tile(m, UP_TM, SUBLANE)
    tn = _tile(n_out, 1024, LANE)
    chunk = min(tm, ROW_CHUNK)
    nchunks = tm // chunk
    n_i, n_j = m // tm, n_out // tn
    assert n_j >= nchunks or n_i == 1
    h0 = _prenorm(x[:tm], norm)
    vec = pl.BlockSpec((1, k), lambda i, j: (0, 0))

    def next_chunk(i, j):
        return jnp.minimum(i + 1, n_i - 1) * nchunks + jnp.minimum(j, nchunks - 1), 0

    return pl.pallas_call(
        functools.partial(_up_norm_kernel, sub=256 if tn % 256 == 0 else LANE, nchunks=nchunks),
        grid=(n_i, n_j),
        in_specs=[pl.BlockSpec((chunk, k), next_chunk),
                  pl.BlockSpec((tm, k), lambda i, j: (0, 0), pipeline_mode=pl.Buffered(1)),
                  vec, vec, vec,
                  pl.BlockSpec((k, tn), lambda i, j: (0, j))],
        out_specs=pl.BlockSpec((tm, tn), lambda i, j: (i, j)),
        out_shape=jax.ShapeDtypeStruct((m, n_out), out_dtype),
        scratch_shapes=[pltpu.VMEM((tm, k), BF16), pltpu.VMEM((tm, k), BF16)],
        compiler_params=_cp("arbitrary", "arbitrary"),
        name=name,
    )(x, h0, _row(norm[0]), _row(norm[1]), _row(norm[2]), w)


def _up_conv_kernel(h_ref, hp_ref, hn_ref, w_ref, cw_ref, cb_ref, o_ref, e0_ref, e1_ref, *, ntaps, sub):
    tm = h_ref.shape[0]
    pad = hp_ref.shape[0]
    half = ntaps // 2
    i = pl.program_id(0)
    first = i == 0
    last = i == pl.num_programs(0) - 1
    h, hp, hn = h_ref[...], hp_ref[...], hn_ref[...]
    rs = min(tm, CONV_ROWS)
    starts = list(range(0, o_ref.shape[1], sub))
    exts = (e0_ref, e1_ref)

    def project(c):
        ext_ref, wsub = exts[c % 2], w_ref[:, starts[c]:starts[c] + sub]
        ext_ref[pl.ds(0, pad), :] = jnp.where(first, 0.0, jnp.dot(hp, wsub, preferred_element_type=F32))
        ext_ref[pl.ds(pad, tm), :] = jnp.dot(h, wsub, preferred_element_type=F32)
        ext_ref[pl.ds(pad + tm, pad), :] = jnp.where(last, 0.0, jnp.dot(hn, wsub, preferred_element_type=F32))

    def conv(c):
        ext_ref, n0 = exts[c % 2], starts[c]
        bias = cb_ref[:, n0:n0 + sub]
        for r0 in range(0, tm, rs):
            acc = None
            for k in range(ntaps):
                v = ext_ref[pl.ds(pad + r0 + k - half, rs), :] * cw_ref[pl.ds(k, 1), n0:n0 + sub]
                acc = v if acc is None else acc + v
            o_ref[pl.ds(r0, rs), n0:n0 + sub] = _silu(acc + bias)

    project(0)
    for c in range(len(starts)):
        if c + 1 < len(starts):
            project(c + 1)
        conv(c)


def _up_conv(h, w, n_out, col_off, conv_w, conv_b, name):
    m, k = h.shape
    ntaps = conv_w.shape[0]
    tm = _tile(m, 1024, 2 * SUBLANE)
    tn = _tile(n_out, 1024, LANE)
    assert col_off % tn == 0
    pad = 2 * SUBLANE
    ratio = tm // pad
    nhb = m // pad
    sub = 256 if tn % 256 == 0 else LANE
    off = col_off // tn
    return pl.pallas_call(
        functools.partial(_up_conv_kernel, ntaps=ntaps, sub=sub),
        grid=(m // tm, n_out // tn),
        in_specs=[pl.BlockSpec((tm, k), lambda i, j: (i, 0)),
                  pl.BlockSpec((pad, k), lambda i, j: (jnp.maximum(i * ratio - 1, 0), 0)),
                  pl.BlockSpec((pad, k), lambda i, j: (jnp.minimum((i + 1) * ratio, nhb - 1), 0)),
                  pl.BlockSpec((k, tn), lambda i, j: (0, j + off)),
                  pl.BlockSpec((ntaps, tn), lambda i, j: (0, j)),
                  pl.BlockSpec((1, tn), lambda i, j: (0, j))],
        out_specs=pl.BlockSpec((tm, tn), lambda i, j: (i, j)),
        out_shape=jax.ShapeDtypeStruct((m, n_out), F32),
        scratch_shapes=[pltpu.VMEM((tm + 2 * pad, sub), F32), pltpu.VMEM((tm + 2 * pad, sub), F32)],
        compiler_params=_cp("arbitrary", "arbitrary"),
        name=name,
    )(h, h, h, w, conv_w, _row(conv_b))


def _seqconv_kernel(x_ref, xp_ref, xn_ref, w_ref, b_ref, o_ref, buf_ref, *, ntaps, act):
    t_rows, c = x_ref.shape
    pad = xp_ref.shape[0]
    i = pl.program_id(0)
    buf_ref[pl.ds(0, pad), :] = jnp.where(i > 0, xp_ref[...].astype(F32), 0.0)
    buf_ref[pl.ds(pad, t_rows), :] = x_ref[...].astype(F32)
    buf_ref[pl.ds(pad + t_rows, pad), :] = jnp.where(i < pl.num_programs(0) - 1, xn_ref[...].astype(F32), 0.0)
    half = ntaps // 2
    sub = min(t_rows, CONV_ROWS)
    lw = min(c, CONV_LANES)

    def body(ci, carry):
        lanes = pl.ds(pl.multiple_of(ci * lw, lw), lw)
        bias = b_ref[:, lanes]
        for s in range(t_rows // sub):
            acc = None
            for k in range(ntaps):
                v = buf_ref[pl.ds(pad + s * sub + k - half, sub), lanes] * w_ref[pl.ds(k, 1), lanes]
                acc = v if acc is None else acc + v
            acc = acc + bias
            if act == "silu":
                acc = _silu(acc)
            o_ref[pl.ds(s * sub, sub), lanes] = acc
        return carry

    lax.fori_loop(0, c // lw, body, 0)


def _seqconv(x, w, b, *, tile_rows, act, name):
    rows, c = x.shape
    ntaps = w.shape[0]
    pad = SUBLANE * (4 // x.dtype.itemsize)
    assert ntaps // 2 <= pad and rows % tile_rows == 0
    ratio = tile_rows // pad
    nhb = rows // pad
    return pl.pallas_call(
        functools.partial(_seqconv_kernel, ntaps=ntaps, act=act),
        grid=(rows // tile_rows,),
        in_specs=[pl.BlockSpec((tile_rows, c), lambda i: (i, 0)),
                  pl.BlockSpec((pad, c), lambda i: (jnp.maximum(i * ratio - 1, 0), 0)),
                  pl.BlockSpec((pad, c), lambda i: (jnp.minimum((i + 1) * ratio, nhb - 1), 0)),
                  pl.BlockSpec((ntaps, c), lambda i: (0, 0)),
                  pl.BlockSpec((1, c), lambda i: (0, 0))],
        out_specs=pl.BlockSpec((tile_rows, c), lambda i: (i, 0)),
        out_shape=jax.ShapeDtypeStruct((rows, c), F32),
        scratch_shapes=[pltpu.VMEM((tile_rows + 2 * pad, c), F32)],
        compiler_params=_cp("arbitrary"),
        name=name,
    )(x, x, x, w, _row(b))


def _rowconv_kernel(x_ref, w_ref, b_ref, g_ref, beta_ref, o_ref, pad_ref, ph_ref, y_ref, *, ntaps, seg):
    t_rows, c = x_ref.shape
    nseg = t_rows // seg
    lead = 2 * SUBLANE
    half = ntaps // 2
    amax = (lead - half + ntaps - 1) // SUBLANE
    span = seg + SUBLANE * amax
    stride = span + SUBLANE
    lw = ph_ref.shape[2]
    sub = min(seg, CONV_ROWS)
    for r in range(nseg):
        pad_ref[pl.ds(r * stride, lead), :] = jnp.zeros((lead, c), F32)
        pad_ref[pl.ds(r * stride + lead, seg), :] = x_ref[pl.ds(r * seg, seg), :]
        pad_ref[pl.ds(r * stride + lead + seg, stride - lead - seg), :] = jnp.zeros((stride - lead - seg, c), F32)

    def body(ci, carry):
        lanes = pl.ds(pl.multiple_of(ci * lw, lw), lw)
        bias = b_ref[:, lanes]
        for r in range(nseg):
            for b in range(1, SUBLANE):
                ph_ref[b] = pad_ref[pl.ds(r * stride + b, span), lanes]
            for s in range(0, seg, sub):
                acc = None
                for k in range(ntaps):
                    off = lead - half + k
                    a, b = off // SUBLANE, off % SUBLANE
                    if b == 0:
                        src = pad_ref[pl.ds(r * stride + SUBLANE * a + s, sub), lanes]
                    else:
                        src = ph_ref[b, pl.ds(SUBLANE * a + s, sub), :]
                    v = src * w_ref[pl.ds(k, 1), lanes]
                    acc = v if acc is None else acc + v
                y_ref[pl.ds(r * seg + s, sub), lanes] = acc + bias
        return carry

    lax.fori_loop(0, c // lw, body, 0)

    def norm_rows(rows):
        y = y_ref[rows, :]
        mu = jnp.mean(y, axis=-1, keepdims=True)
        yc = y - mu
        var = jnp.mean(yc * yc, axis=-1, keepdims=True)
        o_ref[rows, :] = _silu(yc * lax.rsqrt(var + NORM_EPS) * g_ref[...] + beta_ref[...]).astype(o_ref.dtype)

    _row_loop(t_rows, norm_rows)


def _rowconv(x, w, b, ln_g, ln_b, *, seg, tile_rows, name):
    rows, c = x.shape
    ntaps = w.shape[0]
    lead = 2 * SUBLANE
    assert ntaps // 2 <= lead and seg % SUBLANE == 0 and tile_rows % seg == 0
    amax = (lead - ntaps // 2 + ntaps - 1) // SUBLANE
    span = seg + SUBLANE * amax
    stride = span + SUBLANE
    lw = min(c, CONV_LANES)
    return pl.pallas_call(
        functools.partial(_rowconv_kernel, ntaps=ntaps, seg=seg),
        grid=(rows // tile_rows,),
        in_specs=[pl.BlockSpec((tile_rows, c), lambda i: (i, 0)),
                  pl.BlockSpec((ntaps, c), lambda i: (0, 0)),
                  pl.BlockSpec((1, c), lambda i: (0, 0)),
                  pl.BlockSpec((1, c), lambda i: (0, 0)),
                  pl.BlockSpec((1, c), lambda i: (0, 0))],
        out_specs=pl.BlockSpec((tile_rows, c), lambda i: (i, 0)),
        out_shape=jax.ShapeDtypeStruct((rows, c), BF16),
        scratch_shapes=[pltpu.VMEM(((tile_rows // seg) * stride, c), F32),
                        pltpu.VMEM((SUBLANE, span, lw), F32),
                        pltpu.VMEM((tile_rows, c), F32)],
        compiler_params=_cp("arbitrary"),
        name=name,
    )(x, w, _row(b), _row(ln_g), _row(ln_b))


def _colconv_kernel(x_ref, w_ref, b_ref, o_ref, buf_ref, *, ntaps):
    rows, width, lw = x_ref.shape
    half = ntaps // 2
    lead = buf_ref.shape[0] - rows - half
    zeros = jnp.zeros((width, lw), F32)
    for r in range(lead):
        buf_ref[r] = zeros
    for r in range(half):
        buf_ref[lead + rows + r] = zeros

    def copy(r, carry):
        buf_ref[lead + r] = x_ref[r]
        return carry

    lax.fori_loop(0, rows, copy, 0)

    def body(r, carry):
        acc = None
        for k in range(ntaps):
            v = buf_ref[lead - half + r + k] * w_ref[pl.ds(k, 1), :]
            acc = v if acc is None else acc + v
        o_ref[r] = acc + b_ref[...]
        return carry

    lax.fori_loop(0, rows, body, 0)


def _colconv(x3, w, b, name):
    rows, width, c = x3.shape
    ntaps = w.shape[0]
    half = ntaps // 2
    lw = min(c, LANE)
    return pl.pallas_call(
        functools.partial(_colconv_kernel, ntaps=ntaps),
        grid=(c // lw,),
        in_specs=[pl.BlockSpec((rows, width, lw), lambda j: (0, 0, j)),
                  pl.BlockSpec((ntaps, lw), lambda j: (0, j)),
                  pl.BlockSpec((1, lw), lambda j: (0, j))],
        out_specs=pl.BlockSpec((rows, width, lw), lambda j: (0, 0, j)),
        out_shape=jax.ShapeDtypeStruct((rows, width, c), F32),
        scratch_shapes=[pltpu.VMEM((rows + 2 * half, width, lw), F32)],
        compiler_params=_cp("arbitrary"),
        name=name,
    )(x3, w, _row(b))


def _down_mm_kernel(*refs, has_bias, nk):
    if has_bias:
        a_ref, w_ref, bias_ref, x_ref, gate_ref, o_ref = refs
    else:
        a_ref, w_ref, x_ref, gate_ref, o_ref = refs
    k = pl.program_id(2)
    sub = 256 if o_ref.shape[1] % 256 == 0 else LANE
    blocks = [slice(n0, n0 + sub) for n0 in range(0, o_ref.shape[1], sub)]

    def finish(acc, cols):
        if has_bias:
            acc = acc + bias_ref[:, cols]
        return x_ref[:, cols] + gate_ref[:, cols] * acc

    @pl.when(k == 0)
    def _():
        a = a_ref[...]
        for cols in blocks:
            part = jnp.dot(a, w_ref[:, cols], preferred_element_type=F32)
            o_ref[:, cols] = finish(part, cols) if nk == 1 else part

    @pl.when(k > 0)
    def _():
        a = a_ref[...]
        for cols in blocks:
            acc = o_ref[:, cols] + jnp.dot(a, w_ref[:, cols], preferred_element_type=F32)
            o_ref[:, cols] = jnp.where(k == nk - 1, finish(acc, cols), acc)


def _down_mm(a, w, bias, x, gate, name):
    m, kdim = a.shape
    d = w.shape[1]
    tm = _tile(m, 1024, SUBLANE)
    tn = _tile(d, 1024, LANE)
    tk = _tile(kdim, 2048, LANE)
    vec = pl.BlockSpec((1, tn), lambda i, j, k: (0, j))
    in_specs = [pl.BlockSpec((tm, tk), lambda i, j, k: (i, k)),
                pl.BlockSpec((tk, tn), lambda i, j, k: (k, j))]
    args = [a, w]
    if bias is not None:
        in_specs.append(vec)
        args.append(_row(bias))
    in_specs += [pl.BlockSpec((tm, tn), lambda i, j, k: (i, j)), vec]
    args += [x, _row(gate)]
    return pl.pallas_call(
        functools.partial(_down_mm_kernel, has_bias=bias is not None, nk=kdim // tk),
        grid=(m // tm, d // tn, kdim // tk),
        in_specs=in_specs,
        out_specs=pl.BlockSpec((tm, tn), lambda i, j, k: (i, j)),
        out_shape=jax.ShapeDtypeStruct((m, d), F32),
        compiler_params=_cp("arbitrary", "arbitrary", "arbitrary"),
        name=name,
    )(*args)


def _ffn_down_kernel(gc_ref, gp_ref, gn_ref, up_ref, dw_ref, dwb_ref, w_ref, x_ref, gate_ref, ng_ref, sc_ref,
                     sh_ref, *rest, on_grid, final, n_i, nk):
    if final:
        o_ref, buf_ref, a0_ref, a1_ref = rest
        h_ref = None
    else:
        o_ref, h_ref, buf_ref, a0_ref, a1_ref = rest
    tm, tk = gc_ref.shape
    hb = gp_ref.shape[0]
    d = o_ref.shape[1]
    s = pl.program_id(0)
    ip = jnp.minimum(s, n_i * nk - 1) // nk
    m = jnp.maximum(s - 1, 0)
    km = lax.rem(m, nk)

    @pl.when(s == 0)
    def _():
        a1_ref[...] = jnp.zeros_like(a1_ref)

    @pl.when(km == 0)
    def _():
        def zero(rows):
            o_ref[rows, :] = jnp.zeros((min(tm, ROW_CHUNK), d), F32)

        _row_loop(tm, zero)

    sub = min(tm, CONV_ROWS)
    lw = min(tk, CONV_LANES)
    dys = (-1, 0, 1) if on_grid else (0,)
    tn = min(d, 256)

    def step(dst_ref, src_ref):
        buf_ref[pl.ds(0, hb), :] = jnp.where(ip > 0, gp_ref[...].astype(F32), 0.0)
        buf_ref[pl.ds(hb, tm), :] = gc_ref[...].astype(F32)
        buf_ref[pl.ds(hb + tm, hb), :] = jnp.where(ip < n_i - 1, gn_ref[...].astype(F32), 0.0)
        col = lax.broadcasted_iota(jnp.int32, (sub, lw), 0)

        def build(lc, r):
            lanes = pl.ds(lc * lw, lw)
            base = hb + r * sub

            def tapsum(dx):
                acc = None
                for dy in dys:
                    v = (buf_ref[pl.ds(base + GRID_W * dy + dx, sub), lanes]
                         * dw_ref[pl.ds((dy + 1) * 3 + dx + 1, 1), lanes])
                    acc = v if acc is None else acc + v
                return acc

            left, right = tapsum(-1), tapsum(1)
            if on_grid:
                left = jnp.where(col >= 1, left, 0.0)
                right = jnp.where(col <= GRID_W - 2, right, 0.0)
            conv = tapsum(0) + left + right + dwb_ref[:, lanes]
            dst_ref[pl.ds(r * sub, sub), lanes] = (
                jax.nn.gelu(conv) * up_ref[pl.ds(r * sub, sub), lanes].astype(F32)).astype(BF16)

        blocks = [(lc, r) for lc in range(tk // lw) for r in range(tm // sub)]
        n_mm = d // tn
        per = -(-len(blocks) // n_mm)
        a = src_ref[...]
        for c in range(n_mm):
            n0 = c * tn
            o_ref[:, n0:n0 + tn] += jnp.dot(a, w_ref[:, n0:n0 + tn], preferred_element_type=F32)
            for lc, r in blocks[c * per:(c + 1) * per]:
                build(lc, r)

    parity = lax.rem(s, 2)

    @pl.when(parity == 0)
    def _():
        step(a0_ref, a1_ref)

    @pl.when(parity == 1)
    def _():
        step(a1_ref, a0_ref)

    @pl.when((km == nk - 1) & (s > 0))
    def _():
        def finish(rows):
            xn = x_ref[rows, :] + gate_ref[...] * o_ref[rows, :]
            if final:
                ms = jnp.mean(xn * xn, axis=-1, keepdims=True)
                o_ref[rows, :] = (xn * lax.rsqrt(ms + NORM_EPS)) * ng_ref[...]
            else:
                o_ref[rows, :] = xn
                h_ref[rows, :] = _prenorm_math(xn, ng_ref[...], sc_ref[...], sh_ref[...]).astype(BF16)

        _row_loop(tm, finish)


def _down_ffn(a, dw, dw_b, w, x, gate, norm, on_grid, final, name):
    ng, sc, sh = norm
    m = a.shape[0]
    f, d = w.shape
    tm = _tile(m, 512, 2 * GRID_W)
    tk = _tile(f, 512, LANE)
    hb = 2 * GRID_W
    ratio = tm // hb
    nhb = m // hb
    koff = f // tk
    n_i, nk = m // tm, f // tk
    last = n_i * nk - 1

    def pro(s):
        p = jnp.minimum(s, last)
        return p // nk, lax.rem(p, nk)

    def mat(s):
        q = jnp.maximum(s - 1, 0)
        return q // nk, lax.rem(q, nk)

    vec = pl.BlockSpec((1, d), lambda s: (0, 0))
    rowblk = pl.BlockSpec((tm, d), lambda s: (mat(s)[0], 0))
    in_specs = [pl.BlockSpec((tm, tk), lambda s: pro(s)),
                pl.BlockSpec((hb, tk), lambda s: (jnp.maximum(pro(s)[0] * ratio - 1, 0), pro(s)[1])),
                pl.BlockSpec((hb, tk), lambda s: (jnp.minimum((pro(s)[0] + 1) * ratio, nhb - 1), pro(s)[1])),
                pl.BlockSpec((tm, tk), lambda s: (pro(s)[0], pro(s)[1] + koff)),
                pl.BlockSpec((9, tk), lambda s: (0, pro(s)[1])),
                pl.BlockSpec((1, tk), lambda s: (0, pro(s)[1])),
                pl.BlockSpec((tk, d), lambda s: (mat(s)[1], 0)),
                rowblk, vec, vec, vec, vec]
    out_specs = [rowblk]
    out_shape = [jax.ShapeDtypeStruct((m, d), F32)]
    if not final:
        out_specs.append(rowblk)
        out_shape.append(jax.ShapeDtypeStruct((m, d), BF16))
    res = pl.pallas_call(
        functools.partial(_ffn_down_kernel, on_grid=on_grid, final=final, n_i=n_i, nk=nk),
        grid=(n_i * nk + 1,),
        in_specs=in_specs,
        out_specs=out_specs,
        out_shape=out_shape,
        scratch_shapes=[pltpu.VMEM((tm + 2 * hb, tk), F32), pltpu.VMEM((tm, tk), BF16),
                        pltpu.VMEM((tm, tk), BF16)],
        compiler_params=_cp("arbitrary"),
        name=name,
    )(a, a, a, a, dw.reshape(9, f), _row(dw_b), w, x, _row(gate), _row(ng), _row(sc), _row(sh))
    return res[0] if final else res


def _sgu_kernel(u_ref, v_ref, g_ref, b_ref, ws_ref, bs_ref, o_ref):
    v = v_ref[...].astype(F32)
    mu = jnp.mean(v, axis=-1, keepdims=True)
    vc = v - mu
    var = jnp.mean(vc * vc, axis=-1, keepdims=True)
    vn = (vc * lax.rsqrt(var + NORM_EPS) * g_ref[...] + b_ref[...]).astype(BF16)
    groups = ws_ref.shape[0]
    gw = v.shape[1] // groups
    for g in range(groups):
        mixed = jnp.dot(ws_ref[g], vn[:, g * gw:(g + 1) * gw], preferred_element_type=F32) + bs_ref[g]
        o_ref[:, g * gw:(g + 1) * gw] = (u_ref[:, g * gw:(g + 1) * gw].astype(F32) * mixed).astype(o_ref.dtype)


def _sgu_mix(a, ln_g, ln_b, ws, bs):
    m = a.shape[0]
    e = a.shape[1] // 2
    q = SGU_CHUNK
    groups = ws.shape[0]
    return pl.pallas_call(
        _sgu_kernel,
        grid=(m // q,),
        in_specs=[pl.BlockSpec((q, e), lambda c: (c, 0)),
                  pl.BlockSpec((q, e), lambda c: (c, 1)),
                  pl.BlockSpec((1, e), lambda c: (0, 0)),
                  pl.BlockSpec((1, e), lambda c: (0, 0)),
                  pl.BlockSpec((groups, q, q), lambda c: (0, 0, 0)),
                  pl.BlockSpec((groups, q, 1), lambda c: (0, 0, 0))],
        out_specs=pl.BlockSpec((q, e), lambda c: (c, 0)),
        out_shape=jax.ShapeDtypeStruct((m, e), BF16),
        compiler_params=_cp("arbitrary"),
        name="sgu_mix",
    )(a, a, _row(ln_g), _row(ln_b), ws.astype(BF16), bs.reshape(groups, q, 1))


def _split3(a):
    a1 = a.astype(BF16)
    r1 = a - a1.astype(F32)
    a2 = r1.astype(BF16)
    a3 = (r1 - a2.astype(F32)).astype(BF16)
    return [a1, a2, a3]


def _ssd_kernel(*refs, reverse, mode):
    x_ref, b_ref, c_ref, dt_ref, dtt_ref, al_ref, alt_ref, s0_ref = refs[:8]
    if mode == "gated":
        yprev_ref, d_ref, z_ref, ng_ref, y_ref, sfin_ref, st_ref, yacc_ref = refs[8:]
    elif mode == "y":
        y_ref, sfin_ref, st_ref = refs[8:]
    else:
        sfin_ref, st_ref = refs[8:]
    c = pl.program_id(1)

    @pl.when(c == 0)
    def _():
        st_ref[...] = s0_ref[0]

    q = SSD_CHUNK
    hpg = dt_ref.shape[3]
    pw = 2 * SSD_HEADDIM
    row = lax.broadcasted_iota(jnp.int32, (q, q), 0)
    col = lax.broadcasted_iota(jnp.int32, (q, q), 1)
    mask = (row <= col) if reverse else (row >= col)
    mask_t = (row >= col) if reverse else (row <= col)
    mask3 = jnp.concatenate([mask.astype(BF16)] * 3, axis=1)
    mask3_t = jnp.concatenate([mask_t.astype(BF16)] * 3, axis=0)
    lo = lax.broadcasted_iota(jnp.int32, (1, pw), 1) < SSD_HEADDIM
    neg_a = -jnp.exp(al_ref[0, 0])
    neg_a_t = -jnp.exp(alt_ref[0, 0])

    def halves(a):
        return jnp.concatenate([jnp.where(lo, a, 0.0), jnp.where(lo, 0.0, a)], axis=0).astype(BF16)

    def chunk(r0):
        rows = pl.ds(r0, q)
        dtt = dtt_ref[0, 0][:, r0:r0 + q]
        dtat = dtt * neg_a_t
        cumt = jnp.dot(jnp.concatenate(_split3(dtat), axis=1), mask3_t,
                       preferred_element_type=F32)
        lastt = jnp.sum(dtat, axis=1, keepdims=True)
        vt = dtt * jnp.exp(lastt - cumt)
        dta = dt_ref[0, 0, rows, :] * neg_a
        elast = jnp.exp(jnp.sum(dta, axis=0, keepdims=True))
        bmat = b_ref[rows, :]
        bt = bmat.T
        if mode != "state":
            cum = jnp.dot(mask3, jnp.concatenate(_split3(dta), axis=0), preferred_element_type=F32)
            cmat = c_ref[rows, :]
            cb = lax.dot_general(cmat.astype(BF16), bmat.astype(BF16), (((1,), (1,)), ((), ())),
                                 preferred_element_type=F32)

        for j in range(hpg // 2):
            h0, h1 = 2 * j, 2 * j + 1
            cols = slice(j * pw, (j + 1) * pw)
            xs = x_ref[rows, cols]
            xcat = halves(xs)
            st = st_ref[j]
            btv = jnp.concatenate([bt * vt[h0:h0 + 1, :], bt * vt[h1:h1 + 1, :]], axis=1).astype(BF16)
            st_ref[j] = (st * jnp.where(lo, elast[:, h0:h0 + 1], elast[:, h1:h1 + 1])
                         + jnp.dot(btv, xcat, preferred_element_type=F32))
            if mode == "state":
                continue

            def lhs_parts(h):
                colb = jnp.broadcast_to(cum[:, h:h + 1], (q, q))
                e = jnp.where(mask, jnp.exp(colb - cumt[h:h + 1, :]), 0.0)
                return (cb * e * dtt[h:h + 1, :]).astype(BF16), (cmat * jnp.exp(colb)).astype(BF16)

            m0, c0 = lhs_parts(h0)
            m1, c1 = lhs_parts(h1)
            lhs = jnp.concatenate([m0, m1, c0, c1], axis=1)
            rhs = jnp.concatenate([xcat, halves(st)], axis=0)
            y = jnp.dot(lhs, rhs, preferred_element_type=F32)
            if mode == "gated":
                dlane = jnp.where(lo, d_ref[0][:, h0:h0 + 1], d_ref[0][:, h1:h1 + 1])
                yacc_ref[rows, cols] = y + yprev_ref[rows, cols] + xs * dlane
            else:
                y_ref[rows, cols] = y

        if mode == "gated":
            yz = yacc_ref[rows, :] * _silu(z_ref[rows, :].astype(F32))
            ms = jnp.mean(yz * yz, axis=-1, keepdims=True)
            y_ref[rows, :] = (yz * lax.rsqrt(ms + NORM_EPS) * ng_ref[...]).astype(y_ref.dtype)

    starts = list(range(0, x_ref.shape[0], q))
    for r0 in (reversed(starts) if reverse else starts):
        chunk(r0)

    @pl.when(c == pl.num_programs(1) - 1)
    def _():
        sfin_ref[0] = st_ref[...]


def _ssd_scan(xbc, dt_a, dt_b, al_a, al_b, s0, direction, mode, extra=()):
    length = xbc.shape[0]
    groups, hpg = dt_a.shape[1], dt_a.shape[3]
    q = SSD_CHUNK * min(SSD_BLOCK, length // SSD_CHUNK)
    n = SSD_STATE
    pw = 2 * SSD_HEADDIM
    gw = hpg * SSD_HEADDIM
    d_inner = groups * gw
    assert length % q == 0
    nc = length // q
    reverse = direction == 1
    boff = d_inner // n

    def cc(c):
        return nc - 1 - c if reverse else c

    xblk = pl.BlockSpec((q, gw), lambda g, c: (cc(c), g))
    stblk = pl.BlockSpec((1, hpg // 2, n, pw), lambda g, c: (g, 0, 0, 0))
    in_specs = [xblk,
                pl.BlockSpec((q, n), lambda g, c: (cc(c), boff + g)),
                pl.BlockSpec((q, n), lambda g, c: (cc(c), boff + groups + g)),
                pl.BlockSpec((1, 1, q, hpg), lambda g, c: (direction, g, cc(c), 0)),
                pl.BlockSpec((1, 1, hpg, q), lambda g, c: (direction, g, 0, cc(c))),
                pl.BlockSpec((1, 1, 1, hpg), lambda g, c: (direction, g, 0, 0)),
                pl.BlockSpec((1, 1, hpg, 1), lambda g, c: (direction, g, 0, 0)),
                stblk]
    args = [xbc, xbc, xbc, dt_a, dt_b, al_a, al_b, s0]
    out_specs, out_shape = [], []
    scratch = [pltpu.VMEM((hpg // 2, n, pw), F32)]
    if mode == "gated":
        yprev, dskip, z, norm_g = extra
        in_specs += [xblk, pl.BlockSpec((1, 1, hpg), lambda g, c: (g, 0, 0)), xblk,
                     pl.BlockSpec((1, gw), lambda g, c: (0, g))]
        args += [yprev, dskip, z, _row(norm_g)]
        scratch.append(pltpu.VMEM((q, gw), F32))
    if mode != "state":
        out_specs.append(xblk)
        out_shape.append(jax.ShapeDtypeStruct((length, d_inner), BF16 if mode == "gated" else F32))
    out_specs.append(stblk)
    out_shape.append(jax.ShapeDtypeStruct(s0.shape, F32))
    return pl.pallas_call(
        functools.partial(_ssd_kernel, reverse=reverse, mode=mode),
        grid=(groups, nc),
        in_specs=in_specs,
        out_specs=out_specs,
        out_shape=out_shape,
        scratch_shapes=scratch,
        compiler_params=_cp("arbitrary", "arbitrary"),
        name="ssd_scan_%s_%s" % ("bwd" if reverse else "fwd", mode),
    )(*args)


def _ssd_inputs(h, w_bf, conv_w, conv_b, dt_bias, d_inner, conv_dim, heads2, name):
    length = h.shape[0]
    xbc = _up_conv(h, w_bf, conv_dim, d_inner, conv_w, conv_b, name + "_xbc")
    dt = _up(h, w_bf[:, d_inner + conv_dim:], heads2, [0], dt_bias.reshape(-1), _epi_softplus, F32,
             name + "_dt", tn_pref=heads2)
    hpg = heads2 // 2 // SSD_GROUPS
    dt4 = dt.reshape(length, 2, SSD_GROUPS, hpg)
    return xbc, jnp.transpose(dt4, (1, 2, 0, 3)), jnp.transpose(dt4, (1, 2, 3, 0))


def kernel(x, c, ctx, c_ctx, norm_mix_g, norm_ffn_g, mod_wa, mod_wb, mod_b, conv_w1, conv_b1, conv_dw, conv_dw_b, conv_ln_g, conv_ln_b, conv_w2, conv_b2, ssd_w_in, ssd_conv_w, ssd_conv_b, ssd_a_log, ssd_dt_bias, ssd_d, ssd_norm_g, ssd_w_out, sgu_w1, sgu_b1, sgu_ln_g, sgu_ln_b, sgu_ws, sgu_bs, sgu_w2, sgu_b2, ffn_w_in, ffn_dw, ffn_dw_b, ffn_w_out, final_g):
    bsz, seq, d = x.shape
    assert bsz == 1
    depth = norm_mix_g.shape[0]
    xl = x.reshape(seq, d)
    xc = ctx.reshape(ctx.shape[1], d)
    grid_rows = seq // GRID_W
    f_hidden = ffn_w_out.shape[1]

    v8 = jnp.zeros((SUBLANE, d), F32).at[0].set(c[0]).at[1].set(c_ctx)
    mod = _adaln(v8, mod_wa, mod_wb, mod_b)

    def norms(layer, row):
        sh1, sc1, g1, sh2, sc2, g2 = [mod[layer, row, t * d:(t + 1) * d] for t in range(6)]
        return (norm_mix_g[layer], sc1, sh1), (norm_ffn_g[layer], sc2, sh2), g1, g2

    def next_norm(layer, row):
        if layer + 1 < depth:
            return norms(layer + 1, row)[0]
        zeros = jnp.zeros((d,), F32)
        return final_g, zeros, zeros

    ssd_layers = [i for i in range(depth) if i % DEPTH_MIXERS == 1]
    last_ctx = ssd_layers[-1] if ssd_layers else -1

    hl = _prenorm(xl, norms(0, 0)[0])
    hc = _prenorm(xc, norms(0, 1)[0]) if last_ctx >= 0 else None
    for i in range(depth):
        kind, k = i % DEPTH_MIXERS, i // DEPTH_MIXERS
        ctx_full = i < last_ctx
        _, norm2, g1, g2 = norms(i, 0)
        if i <= last_ctx:
            _, cnorm2, cg1, cg2 = norms(i, 1)
        tag = "l%d" % i

        if kind == 0:
            w1 = conv_w1[k].astype(BF16)
            w2 = conv_w2[k].astype(BF16)
            gl = _up(hl, w1, d, [0, d], conv_b1[k], _epi_glu, F32, tag + "_conv_up")
            if k % 2 == 0:
                al = _rowconv(gl, conv_dw[k], conv_dw_b[k], conv_ln_g[k], conv_ln_b[k], seg=GRID_W,
                              tile_rows=2 * GRID_W, name=tag + "_rowconv")
            else:
                gl = _colconv(gl.reshape(grid_rows, GRID_W, d), conv_dw[k], conv_dw_b[k],
                              tag + "_colconv").reshape(seq, d)
                al = _ln_silu(gl, conv_ln_g[k], conv_ln_b[k])
            xl = _down_mm(al, w2, conv_b2[k], xl, g1, tag + "_conv_down")
            if ctx_full:
                gc = _up(hc, w1, d, [0, d], conv_b1[k], _epi_glu, F32, tag + "_conv_up_ctx")
                ac = _rowconv(gc, conv_dw[k], conv_dw_b[k], conv_ln_g[k], conv_ln_b[k], seg=gc.shape[0],
                              tile_rows=gc.shape[0], name=tag + "_seqconv_ctx")
                xc = _down_mm(ac, w2, conv_b2[k], xc, cg1, tag + "_conv_down_ctx")
        elif kind == 1:
            assert not ctx_full
            w_in = ssd_w_in[k].astype(BF16)
            w_out = ssd_w_out[k].astype(BF16)
            d_inner = w_out.shape[0]
            heads2 = 2 * ssd_a_log.shape[2]
            conv_dim = w_in.shape[1] - d_inner - heads2
            hpg = heads2 // 2 // SSD_GROUPS
            al_a = ssd_a_log[k].reshape(2, SSD_GROUPS, 1, hpg)
            al_b = ssd_a_log[k].reshape(2, SSD_GROUPS, hpg, 1)
            dskip = ssd_d[k].reshape(SSD_GROUPS, 1, hpg)
            zero_state = jnp.zeros((SSD_GROUPS, hpg // 2, SSD_STATE, 2 * SSD_HEADDIM), F32)
            xbc_c, dta_c, dtb_c = _ssd_inputs(hc, w_in, ssd_conv_w[k], ssd_conv_b[k], ssd_dt_bias[k],
                                              d_inner, conv_dim, heads2, tag + "_ssd_ctx")
            (state_f,) = _ssd_scan(xbc_c, dta_c, dtb_c, al_a, al_b, zero_state, 0, "state")
            (state_b,) = _ssd_scan(xbc_c, dta_c, dtb_c, al_a, al_b, zero_state, 1, "state")
            xbc_l, dta_l, dtb_l = _ssd_inputs(hl, w_in, ssd_conv_w[k], ssd_conv_b[k], ssd_dt_bias[k],
                                              d_inner, conv_dim, heads2, tag + "_ssd")
            z = _up(hl, w_in, d_inner, [0], None, _epi_id, BF16, tag + "_ssd_z")
            y_f, _ = _ssd_scan(xbc_l, dta_l, dtb_l, al_a, al_b, state_f, 0, "y")
            p, _ = _ssd_scan(xbc_l, dta_l, dtb_l, al_a, al_b, state_b, 1, "gated",
                             (y_f, dskip, z, ssd_norm_g[k]))
            xl = _down_mm(p, w_out, None, xl, g1, tag + "_ssd_down")
        else:
            w1 = sgu_w1[k].astype(BF16)
            w2 = sgu_w2[k].astype(BF16)

            def sgu(h, name):
                a = _up(h, w1, w1.shape[1], [0], sgu_b1[k], _epi_gelu, BF16, name + "_up")
                return _sgu_mix(a, sgu_ln_g[k], sgu_ln_b[k], sgu_ws[k], sgu_bs[k])

            xl = _down_mm(sgu(hl, tag + "_sgu"), w2, sgu_b2[k], xl, g1, tag + "_sgu_down")
            if ctx_full:
                xc = _down_mm(sgu(hc, tag + "_sgu_ctx"), w2, sgu_b2[k], xc, cg1, tag + "_sgu_down_ctx")

        fw_in = ffn_w_in[i].astype(BF16)
        fw_out = ffn_w_out[i].astype(BF16)
        final = i == depth - 1
        a = _up_norm(xl, norm2, fw_in, 2 * f_hidden, BF16, tag + "_ffn_up")
        res = _down_ffn(a, ffn_dw[i], ffn_dw_b[i], fw_out, xl, g2, next_norm(i, 0), True, final, tag + "_ffn_down")
        if final:
            xl = res
        else:
            xl, hl = res
        if ctx_full:
            ac = _up_norm(xc, cnorm2, fw_in, 2 * f_hidden, BF16, tag + "_ffn_up_ctx")
            xc, hc = _down_ffn(ac, ffn_dw[i], ffn_dw_b[i], fw_out, xc, cg2, next_norm(i, 1), False, False,
                               tag + "_ffn_down_ctx")
    return xl.reshape(bsz, seq, d)
```

```python
import functools

import jax
import jax.numpy as jnp
from jax import lax
from jax.experimental import pallas as pl
from jax.experimental.pallas import tpu as pltpu

F32 = jnp.float32
BF16 = jnp.bfloat16
HI = lax.Precision.HIGHEST

NORM_EPS = 1e-6
GRID_W = 64
DEPTH_MIXERS = 3
SSD_GROUPS = 8
SSD_HEADDIM = 64
SSD_STATE = 128
SSD_CHUNK = 128
SSD_BLOCK = 4
SGU_CHUNK = 128
SGU_GROUPS = 16
LANE = 128
SUBLANE = 8
CONV_LANES = 256
CONV_ROWS = 64
UP_TM = 1024
MM_TN = 1024
MM_SUB = 256
DOWN_TK = 4096
FFN_TM = 512
FFN_TK = 512
ROW_TM = 256
ROW_CHUNK = 64
VMEM_LIMIT = 58 * 1024 * 1024


def _cp(*sem):
    return pltpu.CompilerParams(dimension_semantics=sem, vmem_limit_bytes=VMEM_LIMIT)


def _tile(n, pref, align):
    if n <= pref:
        return n
    t = (pref // align) * align
    while t >= align:
        if n % t == 0:
            return t
        t -= align
    return n


def _row(v):
    return v.reshape(1, -1)


def _prenorm_math(x, g, sc, sh):
    ms = jnp.mean(x * x, axis=-1, keepdims=True)
    return (x * lax.rsqrt(ms + NORM_EPS)) * g * (1.0 + sc) + sh


def _silu(x):
    return x * jax.nn.sigmoid(x)


def _row_loop(nrows, fn):
    step = min(nrows, ROW_CHUNK)

    def body(r, carry):
        fn(pl.ds(pl.multiple_of(r * step, step), step))
        return carry

    lax.fori_loop(0, nrows // step, body, 0)


def _adaln_kernel(v_ref, wa_ref, wb_ref, b_ref, o_ref, t_ref):
    @pl.when(pl.program_id(1) == 0)
    def _():
        t_ref[...] = jnp.dot(_silu(v_ref[...]), wa_ref[0], preferred_element_type=F32, precision=HI)

    o_ref[0] = jnp.dot(t_ref[...], wb_ref[0], preferred_element_type=F32, precision=HI) + b_ref[0]


def _adaln(v8, wa, wb, b):
    depth, d, r = wa.shape
    n = wb.shape[2]
    tn = _tile(n, 6144, LANE)
    return pl.pallas_call(
        _adaln_kernel,
        grid=(depth, n // tn),
        in_specs=[pl.BlockSpec((SUBLANE, d), lambda l, j: (0, 0)),
                  pl.BlockSpec((1, d, r), lambda l, j: (l, 0, 0)),
                  pl.BlockSpec((1, r, tn), lambda l, j: (l, 0, j)),
                  pl.BlockSpec((1, 1, tn), lambda l, j: (l, 0, j))],
        out_specs=pl.BlockSpec((1, SUBLANE, tn), lambda l, j: (l, 0, j)),
        out_shape=jax.ShapeDtypeStruct((depth, SUBLANE, n), F32),
        scratch_shapes=[pltpu.VMEM((SUBLANE, r), F32)],
        compiler_params=_cp("arbitrary", "arbitrary"),
        name="adaln",
    )(v8, wa, wb, b.reshape(depth, 1, n))


def _prenorm_kernel(x_ref, g_ref, sc_ref, sh_ref, o_ref):
    o_ref[...] = _prenorm_math(x_ref[...], g_ref[...], sc_ref[...], sh_ref[...]).astype(o_ref.dtype)


def _prenorm(x, norm):
    g, sc, sh = norm
    m, d = x.shape
    tm = _tile(m, ROW_TM, SUBLANE)
    vec = pl.BlockSpec((1, d), lambda i: (0, 0))
    return pl.pallas_call(
        _prenorm_kernel,
        grid=(m // tm,),
        in_specs=[pl.BlockSpec((tm, d), lambda i: (i, 0)), vec, vec, vec],
        out_specs=pl.BlockSpec((tm, d), lambda i: (i, 0)),
        out_shape=jax.ShapeDtypeStruct((m, d), BF16),
        compiler_params=_cp("arbitrary"),
        name="prenorm",
    )(x, _row(g), _row(sc), _row(sh))


def _ln_silu_kernel(x_ref, g_ref, b_ref, o_ref):
    x = x_ref[...]
    mu = jnp.mean(x, axis=-1, keepdims=True)
    xc = x - mu
    var = jnp.mean(xc * xc, axis=-1, keepdims=True)
    y = xc * lax.rsqrt(var + NORM_EPS) * g_ref[...] + b_ref[...]
    o_ref[...] = _silu(y).astype(o_ref.dtype)


def _ln_silu(x, g, b):
    m, d = x.shape
    tm = _tile(m, ROW_TM, SUBLANE)
    vec = pl.BlockSpec((1, d), lambda i: (0, 0))
    return pl.pallas_call(
        _ln_silu_kernel,
        grid=(m // tm,),
        in_specs=[pl.BlockSpec((tm, d), lambda i: (i, 0)), vec, vec],
        out_specs=pl.BlockSpec((tm, d), lambda i: (i, 0)),
        out_shape=jax.ShapeDtypeStruct((m, d), BF16),
        compiler_params=_cp("arbitrary"),
        name="ln_silu",
    )(x, _row(g), _row(b))


def _up_kernel(*refs, n_w, has_bias, epi, sub):
    h_ref = refs[0]
    w_refs = refs[1:1 + n_w]
    b_refs = refs[1 + n_w:1 + 2 * n_w] if has_bias else ()
    o_ref = refs[-1]
    h = h_ref[...]
    for n0 in range(0, o_ref.shape[1], sub):
        accs = []
        for t in range(n_w):
            a = jnp.dot(h, w_refs[t][:, n0:n0 + sub], preferred_element_type=F32)
            if has_bias:
                a = a + b_refs[t][:, n0:n0 + sub]
            accs.append(a)
        o_ref[:, n0:n0 + sub] = epi(*accs).astype(o_ref.dtype)


def _epi_id(a):
    return a


def _epi_glu(a, b):
    return a * jax.nn.sigmoid(b)


def _epi_gelu(a):
    return jax.nn.gelu(a)


def _epi_softplus(a):
    return jnp.maximum(a, 0.0) + jnp.log1p(jnp.exp(-jnp.abs(a)))


def _up(h, w, n_out, col_offs, bias, epi, out_dtype, name, tn_pref=MM_TN):
    m, k = h.shape
    tm = _tile(m, UP_TM, SUBLANE)
    tn = _tile(n_out, tn_pref // len(col_offs), LANE)
    for off in col_offs:
        assert off % tn == 0
    n_w = len(col_offs)
    in_specs = [pl.BlockSpec((tm, k), lambda i, j: (i, 0))]
    args = [h]
    for off in col_offs:
        in_specs.append(pl.BlockSpec((k, tn), functools.partial(lambda i, j, o: (0, j + o), o=off // tn)))
        args.append(w)
    if bias is not None:
        for off in col_offs:
            in_specs.append(pl.BlockSpec((1, tn), functools.partial(lambda i, j, o: (0, j + o), o=off // tn)))
            args.append(_row(bias))
    return pl.pallas_call(
        functools.partial(_up_kernel, n_w=n_w, has_bias=bias is not None, epi=epi,
                          sub=MM_SUB if tn % MM_SUB == 0 else LANE),
        grid=(m // tm, n_out // tn),
        in_specs=in_specs,
        out_specs=pl.BlockSpec((tm, tn), lambda i, j: (i, j)),
        out_shape=jax.ShapeDtypeStruct((m, n_out), out_dtype),
        compiler_params=_cp("arbitrary", "arbitrary"),
        name=name,
    )(*args)


def _up_norm_kernel(xc_ref, h0_ref, ng_ref, sc_ref, sh_ref, w_ref, o_ref, hs0_ref, hs1_ref, *, sub, nchunks):
    i = pl.program_id(0)
    j = pl.program_id(1)
    chunk = xc_ref.shape[0]

    @pl.when((i == 0) & (j == 0))
    def _():
        def copy(rows):
            hs0_ref[rows, :] = h0_ref[rows, :]

        _row_loop(h0_ref.shape[0], copy)

    def step(cur_ref, nxt_ref):
        r0 = pl.multiple_of(jnp.minimum(j, nchunks - 1) * chunk, chunk)
        nxt_ref[pl.ds(r0, chunk), :] = _prenorm_math(xc_ref[...], ng_ref[...], sc_ref[...], sh_ref[...]).astype(BF16)
        h = cur_ref[...]
        for n0 in range(0, o_ref.shape[1], sub):
            o_ref[:, n0:n0 + sub] = jnp.dot(h, w_ref[:, n0:n0 + sub],
                                            preferred_element_type=F32).astype(o_ref.dtype)

    parity = lax.rem(i, 2)

    @pl.when(parity == 0)
    def _():
        step(hs0_ref, hs1_ref)

    @pl.when(parity == 1)
    def _():
        step(hs1_ref, hs0_ref)


def _up_norm(x, norm, w, n_out, out_dtype, name):
    m, k = x.shape
    tm = _tile(m, UP_TM, SUBLANE)
    tn = _tile(n_out, MM_TN, LANE)
    chunk = min(tm, ROW_CHUNK)
    nchunks = tm // chunk
    n_i, n_j = m // tm, n_out // tn
    assert n_j >= nchunks or n_i == 1
    h0 = _prenorm(x[:tm], norm)
    vec = pl.BlockSpec((1, k), lambda i, j: (0, 0))

    def next_chunk(i, j):
        return jnp.minimum(i + 1, n_i - 1) * nchunks + jnp.minimum(j, nchunks - 1), 0

    return pl.pallas_call(
        functools.partial(_up_norm_kernel, sub=MM_SUB if tn % MM_SUB == 0 else LANE, nchunks=nchunks),
        grid=(n_i, n_j),
        in_specs=[pl.BlockSpec((chunk, k), next_chunk),
                  pl.BlockSpec((tm, k), lambda i, j: (0, 0), pipeline_mode=pl.Buffered(1)),
                  vec, vec, vec,
                  pl.BlockSpec((k, tn), lambda i, j: (0, j))],
        out_specs=pl.BlockSpec((tm, tn), lambda i, j: (i, j)),
        out_shape=jax.ShapeDtypeStruct((m, n_out), out_dtype),
        scratch_shapes=[pltpu.VMEM((tm, k), BF16), pltpu.VMEM((tm, k), BF16)],
        compiler_params=_cp("arbitrary", "arbitrary"),
        name=name,
    )(x, h0, _row(norm[0]), _row(norm[1]), _row(norm[2]), w)


def _up_conv_kernel(h_ref, hp_ref, hn_ref, w_ref, cw_ref, cb_ref, o_ref, e0_ref, e1_ref, *, ntaps, sub):
    tm = h_ref.shape[0]
    pad = hp_ref.shape[0]
    half = ntaps // 2
    i = pl.program_id(0)
    first = i == 0
    last = i == pl.num_programs(0) - 1
    h, hp, hn = h_ref[...], hp_ref[...], hn_ref[...]
    rs = min(tm, CONV_ROWS)
    starts = list(range(0, o_ref.shape[1], sub))
    exts = (e0_ref, e1_ref)

    def project(c):
        ext_ref, wsub = exts[c % 2], w_ref[:, starts[c]:starts[c] + sub]
        ext_ref[pl.ds(0, pad), :] = jnp.where(first, 0.0, jnp.dot(hp, wsub, preferred_element_type=F32))
        ext_ref[pl.ds(pad, tm), :] = jnp.dot(h, wsub, preferred_element_type=F32)
        ext_ref[pl.ds(pad + tm, pad), :] = jnp.where(last, 0.0, jnp.dot(hn, wsub, preferred_element_type=F32))

    def conv(c):
        ext_ref, n0 = exts[c % 2], starts[c]
        bias = cb_ref[:, n0:n0 + sub]
        for r0 in range(0, tm, rs):
            acc = None
            for k in range(ntaps):
                v = ext_ref[pl.ds(pad + r0 + k - half, rs), :] * cw_ref[pl.ds(k, 1), n0:n0 + sub]
                acc = v if acc is None else acc + v
            o_ref[pl.ds(r0, rs), n0:n0 + sub] = _silu(acc + bias)

    project(0)
    for c in range(len(starts)):
        if c + 1 < len(starts):
            project(c + 1)
        conv(c)


def _up_conv(h, w, n_out, col_off, conv_w, conv_b, name):
    m, k = h.shape
    ntaps = conv_w.shape[0]
    tm = _tile(m, UP_TM, 2 * SUBLANE)
    tn = _tile(n_out, MM_TN, LANE)
    assert col_off % tn == 0
    pad = 2 * SUBLANE
    ratio = tm // pad
    nhb = m // pad
    sub = MM_SUB if tn % MM_SUB == 0 else LANE
    off = col_off // tn
    return pl.pallas_call(
        functools.partial(_up_conv_kernel, ntaps=ntaps, sub=sub),
        grid=(m // tm, n_out // tn),
        in_specs=[pl.BlockSpec((tm, k), lambda i, j: (i, 0)),
                  pl.BlockSpec((pad, k), lambda i, j: (jnp.maximum(i * ratio - 1, 0), 0)),
                  pl.BlockSpec((pad, k), lambda i, j: (jnp.minimum((i + 1) * ratio, nhb - 1), 0)),
                  pl.BlockSpec((k, tn), lambda i, j: (0, j + off)),
                  pl.BlockSpec((ntaps, tn), lambda i, j: (0, j)),
                  pl.BlockSpec((1, tn), lambda i, j: (0, j))],
        out_specs=pl.BlockSpec((tm, tn), lambda i, j: (i, j)),
        out_shape=jax.ShapeDtypeStruct((m, n_out), F32),
        scratch_shapes=[pltpu.VMEM((tm + 2 * pad, sub), F32), pltpu.VMEM((tm + 2 * pad, sub), F32)],
        compiler_params=_cp("arbitrary", "arbitrary"),
        name=name,
    )(h, h, h, w, conv_w, _row(conv_b))


def _rowconv_kernel(x_ref, w_ref, b_ref, g_ref, beta_ref, o_ref, pad_ref, ph_ref, y_ref, *, ntaps, seg):
    t_rows, c = x_ref.shape
    nseg = t_rows // seg
    lead = 2 * SUBLANE
    half = ntaps // 2
    amax = (lead - half + ntaps - 1) // SUBLANE
    span = seg + SUBLANE * amax
    stride = span + SUBLANE
    lw = ph_ref.shape[2]
    sub = min(seg, CONV_ROWS)
    for r in range(nseg):
        pad_ref[pl.ds(r * stride, lead), :] = jnp.zeros((lead, c), F32)
        pad_ref[pl.ds(r * stride + lead, seg), :] = x_ref[pl.ds(r * seg, seg), :]
        pad_ref[pl.ds(r * stride + lead + seg, stride - lead - seg), :] = jnp.zeros((stride - lead - seg, c), F32)

    def body(ci, carry):
        lanes = pl.ds(pl.multiple_of(ci * lw, lw), lw)
        bias = b_ref[:, lanes]
        for r in range(nseg):
            for b in range(1, SUBLANE):
                ph_ref[b] = pad_ref[pl.ds(r * stride + b, span), lanes]
            for s in range(0, seg, sub):
                acc = None
                for k in range(ntaps):
                    off = lead - half + k
                    a, b = off // SUBLANE, off % SUBLANE
                    if b == 0:
                        src = pad_ref[pl.ds(r * stride + SUBLANE * a + s, sub), lanes]
                    else:
                        src = ph_ref[b, pl.ds(SUBLANE * a + s, sub), :]
                    v = src * w_ref[pl.ds(k, 1), lanes]
                    acc = v if acc is None else acc + v
                y_ref[pl.ds(r * seg + s, sub), lanes] = acc + bias
        return carry

    lax.fori_loop(0, c // lw, body, 0)

    def norm_rows(rows):
        y = y_ref[rows, :]
        mu = jnp.mean(y, axis=-1, keepdims=True)
        yc = y - mu
        var = jnp.mean(yc * yc, axis=-1, keepdims=True)
        o_ref[rows, :] = _silu(yc * lax.rsqrt(var + NORM_EPS) * g_ref[...] + beta_ref[...]).astype(o_ref.dtype)

    _row_loop(t_rows, norm_rows)


def _rowconv(x, w, b, ln_g, ln_b, *, seg, tile_rows, name):
    rows, c = x.shape
    ntaps = w.shape[0]
    lead = 2 * SUBLANE
    assert ntaps // 2 <= lead and seg % SUBLANE == 0 and tile_rows % seg == 0
    amax = (lead - ntaps // 2 + ntaps - 1) // SUBLANE
    span = seg + SUBLANE * amax
    stride = span + SUBLANE
    lw = min(c, CONV_LANES)
    return pl.pallas_call(
        functools.partial(_rowconv_kernel, ntaps=ntaps, seg=seg),
        grid=(rows // tile_rows,),
        in_specs=[pl.BlockSpec((tile_rows, c), lambda i: (i, 0)),
                  pl.BlockSpec((ntaps, c), lambda i: (0, 0)),
                  pl.BlockSpec((1, c), lambda i: (0, 0)),
                  pl.BlockSpec((1, c), lambda i: (0, 0)),
                  pl.BlockSpec((1, c), lambda i: (0, 0))],
        out_specs=pl.BlockSpec((tile_rows, c), lambda i: (i, 0)),
        out_shape=jax.ShapeDtypeStruct((rows, c), BF16),
        scratch_shapes=[pltpu.VMEM(((tile_rows // seg) * stride, c), F32),
                        pltpu.VMEM((SUBLANE, span, lw), F32),
                        pltpu.VMEM((tile_rows, c), F32)],
        compiler_params=_cp("arbitrary"),
        name=name,
    )(x, w, _row(b), _row(ln_g), _row(ln_b))


def _colconv_kernel(x_ref, w_ref, b_ref, o_ref, buf_ref, *, ntaps):
    rows, width, lw = x_ref.shape
    half = ntaps // 2
    lead = buf_ref.shape[0] - rows - half
    zeros = jnp.zeros((width, lw), F32)
    for r in range(lead):
        buf_ref[r] = zeros
    for r in range(half):
        buf_ref[lead + rows + r] = zeros

    def copy(r, carry):
        buf_ref[lead + r] = x_ref[r]
        return carry

    lax.fori_loop(0, rows, copy, 0)

    def body(r, carry):
        acc = None
        for k in range(ntaps):
            v = buf_ref[lead - half + r + k] * w_ref[pl.ds(k, 1), :]
            acc = v if acc is None else acc + v
        o_ref[r] = acc + b_ref[...]
        return carry

    lax.fori_loop(0, rows, body, 0)


def _colconv(x3, w, b, name):
    rows, width, c = x3.shape
    ntaps = w.shape[0]
    half = ntaps // 2
    lw = min(c, LANE)
    return pl.pallas_call(
        functools.partial(_colconv_kernel, ntaps=ntaps),
        grid=(c // lw,),
        in_specs=[pl.BlockSpec((rows, width, lw), lambda j: (0, 0, j)),
                  pl.BlockSpec((ntaps, lw), lambda j: (0, j)),
                  pl.BlockSpec((1, lw), lambda j: (0, j))],
        out_specs=pl.BlockSpec((rows, width, lw), lambda j: (0, 0, j)),
        out_shape=jax.ShapeDtypeStruct((rows, width, c), F32),
        scratch_shapes=[pltpu.VMEM((rows + 2 * half, width, lw), F32)],
        compiler_params=_cp("arbitrary"),
        name=name,
    )(x3, w, _row(b))


def _down_mm_kernel(*refs, has_bias, nk):
    if has_bias:
        a_ref, w_ref, bias_ref, x_ref, gate_ref, o_ref = refs
    else:
        a_ref, w_ref, x_ref, gate_ref, o_ref = refs
    k = pl.program_id(2)
    sub = MM_SUB if o_ref.shape[1] % MM_SUB == 0 else LANE
    blocks = [slice(n0, n0 + sub) for n0 in range(0, o_ref.shape[1], sub)]

    def finish(acc, cols):
        if has_bias:
            acc = acc + bias_ref[:, cols]
        return x_ref[:, cols] + gate_ref[:, cols] * acc

    @pl.when(k == 0)
    def _():
        a = a_ref[...]
        for cols in blocks:
            part = jnp.dot(a, w_ref[:, cols], preferred_element_type=F32)
            o_ref[:, cols] = finish(part, cols) if nk == 1 else part

    @pl.when(k > 0)
    def _():
        a = a_ref[...]
        for cols in blocks:
            acc = o_ref[:, cols] + jnp.dot(a, w_ref[:, cols], preferred_element_type=F32)
            o_ref[:, cols] = jnp.where(k == nk - 1, finish(acc, cols), acc)


def _down_mm(a, w, bias, x, gate, name):
    m, kdim = a.shape
    d = w.shape[1]
    tm = _tile(m, UP_TM, SUBLANE)
    tn = _tile(d, MM_TN, LANE)
    tk = _tile(kdim, DOWN_TK, LANE)
    vec = pl.BlockSpec((1, tn), lambda i, j, k: (0, j))
    in_specs = [pl.BlockSpec((tm, tk), lambda i, j, k: (i, k)),
                pl.BlockSpec((tk, tn), lambda i, j, k: (k, j))]
    args = [a, w]
    if bias is not None:
        in_specs.append(vec)
        args.append(_row(bias))
    in_specs += [pl.BlockSpec((tm, tn), lambda i, j, k: (i, j)), vec]
    args += [x, _row(gate)]
    return pl.pallas_call(
        functools.partial(_down_mm_kernel, has_bias=bias is not None, nk=kdim // tk),
        grid=(m // tm, d // tn, kdim // tk),
        in_specs=in_specs,
        out_specs=pl.BlockSpec((tm, tn), lambda i, j, k: (i, j)),
        out_shape=jax.ShapeDtypeStruct((m, d), F32),
        compiler_params=_cp("arbitrary", "arbitrary", "arbitrary"),
        name=name,
    )(*args)


def _ffn_down_kernel(gc_ref, gp_ref, gn_ref, up_ref, dw_ref, dwb_ref, w_ref, x_ref, gate_ref, ng_ref, sc_ref,
                     sh_ref, *rest, on_grid, final, n_i, nk):
    if final:
        o_ref, buf_ref, a0_ref, a1_ref = rest
        h_ref = None
    else:
        o_ref, h_ref, buf_ref, a0_ref, a1_ref = rest
    tm, tk = gc_ref.shape
    hb = gp_ref.shape[0]
    d = o_ref.shape[1]
    s = pl.program_id(0)
    ip = jnp.minimum(s, n_i * nk - 1) // nk
    m = jnp.maximum(s - 1, 0)
    km = lax.rem(m, nk)

    @pl.when(s == 0)
    def _():
        a1_ref[...] = jnp.zeros_like(a1_ref)

    @pl.when(km == 0)
    def _():
        def zero(rows):
            o_ref[rows, :] = jnp.zeros((min(tm, ROW_CHUNK), d), F32)

        _row_loop(tm, zero)

    sub = min(tm, CONV_ROWS)
    lw = min(tk, CONV_LANES)
    dys = (-1, 0, 1) if on_grid else (0,)
    tn = min(d, MM_SUB)

    def step(dst_ref, src_ref):
        buf_ref[pl.ds(0, hb), :] = jnp.where(ip > 0, gp_ref[...].astype(F32), 0.0)
        buf_ref[pl.ds(hb, tm), :] = gc_ref[...].astype(F32)
        buf_ref[pl.ds(hb + tm, hb), :] = jnp.where(ip < n_i - 1, gn_ref[...].astype(F32), 0.0)
        col = lax.broadcasted_iota(jnp.int32, (sub, lw), 0)

        def build(lc, r):
            lanes = pl.ds(lc * lw, lw)
            base = hb + r * sub

            def tapsum(dx):
                acc = None
                for dy in dys:
                    v = (buf_ref[pl.ds(base + GRID_W * dy + dx, sub), lanes]
                         * dw_ref[pl.ds((dy + 1) * 3 + dx + 1, 1), lanes])
                    acc = v if acc is None else acc + v
                return acc

            left, right = tapsum(-1), tapsum(1)
            if on_grid:
                left = jnp.where(col >= 1, left, 0.0)
                right = jnp.where(col <= GRID_W - 2, right, 0.0)
            conv = tapsum(0) + left + right + dwb_ref[:, lanes]
            dst_ref[pl.ds(r * sub, sub), lanes] = (
                jax.nn.gelu(conv) * up_ref[pl.ds(r * sub, sub), lanes].astype(F32)).astype(BF16)

        blocks = [(lc, r) for lc in range(tk // lw) for r in range(tm // sub)]
        n_mm = d // tn
        per = -(-len(blocks) // n_mm)
        a = src_ref[...]
        for c in range(n_mm):
            n0 = c * tn
            o_ref[:, n0:n0 + tn] += jnp.dot(a, w_ref[:, n0:n0 + tn], preferred_element_type=F32)
            for lc, r in blocks[c * per:(c + 1) * per]:
                build(lc, r)

    parity = lax.rem(s, 2)

    @pl.when(parity == 0)
    def _():
        step(a0_ref, a1_ref)

    @pl.when(parity == 1)
    def _():
        step(a1_ref, a0_ref)

    @pl.when((km == nk - 1) & (s > 0))
    def _():
        def finish(rows):
            xn = x_ref[rows, :] + gate_ref[...] * o_ref[rows, :]
            if final:
                ms = jnp.mean(xn * xn, axis=-1, keepdims=True)
                o_ref[rows, :] = (xn * lax.rsqrt(ms + NORM_EPS)) * ng_ref[...]
            else:
                o_ref[rows, :] = xn
                h_ref[rows, :] = _prenorm_math(xn, ng_ref[...], sc_ref[...], sh_ref[...]).astype(BF16)

        _row_loop(tm, finish)


def _down_ffn(a, dw, dw_b, w, x, gate, norm, on_grid, final, name):
    ng, sc, sh = norm
    m = a.shape[0]
    f, d = w.shape
    tm = _tile(m, FFN_TM, 2 * GRID_W)
    tk = _tile(f, FFN_TK, LANE)
    hb = 2 * GRID_W
    ratio = tm // hb
    nhb = m // hb
    koff = f // tk
    n_i, nk = m // tm, f // tk
    last = n_i * nk - 1

    def pro(s):
        p = jnp.minimum(s, last)
        return p // nk, lax.rem(p, nk)

    def mat(s):
        q = jnp.maximum(s - 1, 0)
        return q // nk, lax.rem(q, nk)

    vec = pl.BlockSpec((1, d), lambda s: (0, 0))
    rowblk = pl.BlockSpec((tm, d), lambda s: (mat(s)[0], 0))
    in_specs = [pl.BlockSpec((tm, tk), lambda s: pro(s)),
                pl.BlockSpec((hb, tk), lambda s: (jnp.maximum(pro(s)[0] * ratio - 1, 0), pro(s)[1])),
                pl.BlockSpec((hb, tk), lambda s: (jnp.minimum((pro(s)[0] + 1) * ratio, nhb - 1), pro(s)[1])),
                pl.BlockSpec((tm, tk), lambda s: (pro(s)[0], pro(s)[1] + koff)),
                pl.BlockSpec((9, tk), lambda s: (0, pro(s)[1])),
                pl.BlockSpec((1, tk), lambda s: (0, pro(s)[1])),
                pl.BlockSpec((tk, d), lambda s: (mat(s)[1], 0)),
                rowblk, vec, vec, vec, vec]
    out_specs = [rowblk]
    out_shape = [jax.ShapeDtypeStruct((m, d), F32)]
    if not final:
        out_specs.append(rowblk)
        out_shape.append(jax.ShapeDtypeStruct((m, d), BF16))
    res = pl.pallas_call(
        functools.partial(_ffn_down_kernel, on_grid=on_grid, final=final, n_i=n_i, nk=nk),
        grid=(n_i * nk + 1,),
        in_specs=in_specs,
        out_specs=out_specs,
        out_shape=out_shape,
        scratch_shapes=[pltpu.VMEM((tm + 2 * hb, tk), F32), pltpu.VMEM((tm, tk), BF16),
                        pltpu.VMEM((tm, tk), BF16)],
        compiler_params=_cp("arbitrary"),
        name=name,
    )(a, a, a, a, dw.reshape(9, f), _row(dw_b), w, x, _row(gate), _row(ng), _row(sc), _row(sh))
    return res[0] if final else res


def _sgu_kernel(u_ref, v_ref, g_ref, b_ref, ws_ref, bs_ref, o_ref):
    v = v_ref[...].astype(F32)
    mu = jnp.mean(v, axis=-1, keepdims=True)
    vc = v - mu
    var = jnp.mean(vc * vc, axis=-1, keepdims=True)
    vn = (vc * lax.rsqrt(var + NORM_EPS) * g_ref[...] + b_ref[...]).astype(BF16)
    groups = ws_ref.shape[0]
    gw = v.shape[1] // groups
    for g in range(groups):
        mixed = jnp.dot(ws_ref[g], vn[:, g * gw:(g + 1) * gw], preferred_element_type=F32) + bs_ref[g]
        o_ref[:, g * gw:(g + 1) * gw] = (u_ref[:, g * gw:(g + 1) * gw].astype(F32) * mixed).astype(o_ref.dtype)


def _sgu_mix(a, ln_g, ln_b, ws, bs):
    m = a.shape[0]
    e = a.shape[1] // 2
    q = SGU_CHUNK
    groups = ws.shape[0]
    return pl.pallas_call(
        _sgu_kernel,
        grid=(m // q,),
        in_specs=[pl.BlockSpec((q, e), lambda c: (c, 0)),
                  pl.BlockSpec((q, e), lambda c: (c, 1)),
                  pl.BlockSpec((1, e), lambda c: (0, 0)),
                  pl.BlockSpec((1, e), lambda c: (0, 0)),
                  pl.BlockSpec((groups, q, q), lambda c: (0, 0, 0)),
                  pl.BlockSpec((groups, q, 1), lambda c: (0, 0, 0))],
        out_specs=pl.BlockSpec((q, e), lambda c: (c, 0)),
        out_shape=jax.ShapeDtypeStruct((m, e), BF16),
        compiler_params=_cp("arbitrary"),
        name="sgu_mix",
    )(a, a, _row(ln_g), _row(ln_b), ws.astype(BF16), bs.reshape(groups, q, 1))


def _split3(a):
    a1 = a.astype(BF16)
    r1 = a - a1.astype(F32)
    a2 = r1.astype(BF16)
    a3 = (r1 - a2.astype(F32)).astype(BF16)
    return [a1, a2, a3]


def _ssd_kernel(*refs, reverse, mode):
    x_ref, b_ref, c_ref, dt_ref, dtt_ref, al_ref, alt_ref, s0_ref = refs[:8]
    if mode == "gated":
        yprev_ref, d_ref, z_ref, ng_ref, y_ref, sfin_ref, st_ref, yacc_ref = refs[8:]
    elif mode == "y":
        y_ref, sfin_ref, st_ref = refs[8:]
    else:
        sfin_ref, st_ref = refs[8:]
    c = pl.program_id(1)

    @pl.when(c == 0)
    def _():
        st_ref[...] = s0_ref[0]

    q = SSD_CHUNK
    hpg = dt_ref.shape[3]
    pw = 2 * SSD_HEADDIM
    row = lax.broadcasted_iota(jnp.int32, (q, q), 0)
    col = lax.broadcasted_iota(jnp.int32, (q, q), 1)
    mask = (row <= col) if reverse else (row >= col)
    mask_t = (row >= col) if reverse else (row <= col)
    mask3 = jnp.concatenate([mask.astype(BF16)] * 3, axis=1)
    mask3_t = jnp.concatenate([mask_t.astype(BF16)] * 3, axis=0)
    lo = lax.broadcasted_iota(jnp.int32, (1, pw), 1) < SSD_HEADDIM
    neg_a = -jnp.exp(al_ref[0, 0])
    neg_a_t = -jnp.exp(alt_ref[0, 0])

    def halves(a):
        return jnp.concatenate([jnp.where(lo, a, 0.0), jnp.where(lo, 0.0, a)], axis=0).astype(BF16)

    def chunk(r0):
        rows = pl.ds(r0, q)
        dtt = dtt_ref[0, 0][:, r0:r0 + q]
        dtat = dtt * neg_a_t
        cumt = jnp.dot(jnp.concatenate(_split3(dtat), axis=1), mask3_t,
                       preferred_element_type=F32)
        lastt = jnp.sum(dtat, axis=1, keepdims=True)
        vt = dtt * jnp.exp(lastt - cumt)
        dta = dt_ref[0, 0, rows, :] * neg_a
        elast = jnp.exp(jnp.sum(dta, axis=0, keepdims=True))
        bmat = b_ref[rows, :]
        bt = bmat.T
        if mode != "state":
            cum = jnp.dot(mask3, jnp.concatenate(_split3(dta), axis=0), preferred_element_type=F32)
            cmat = c_ref[rows, :]
            cb = lax.dot_general(cmat.astype(BF16), bmat.astype(BF16), (((1,), (1,)), ((), ())),
                                 preferred_element_type=F32)

        for j in range(hpg // 2):
            h0, h1 = 2 * j, 2 * j + 1
            cols = slice(j * pw, (j + 1) * pw)
            xs = x_ref[rows, cols]
            xcat = halves(xs)
            st = st_ref[j]
            btv = jnp.concatenate([bt * vt[h0:h0 + 1, :], bt * vt[h1:h1 + 1, :]], axis=1).astype(BF16)
            st_ref[j] = (st * jnp.where(lo, elast[:, h0:h0 + 1], elast[:, h1:h1 + 1])
                         + jnp.dot(btv, xcat, preferred_element_type=F32))
            if mode == "state":
                continue

            def lhs_parts(h):
                colb = jnp.broadcast_to(cum[:, h:h + 1], (q, q))
                e = jnp.where(mask, jnp.exp(colb - cumt[h:h + 1, :]), 0.0)
                return (cb * e * dtt[h:h + 1, :]).astype(BF16), (cmat * jnp.exp(colb)).astype(BF16)

            m0, c0 = lhs_parts(h0)
            m1, c1 = lhs_parts(h1)
            lhs = jnp.concatenate([m0, m1, c0, c1], axis=1)
            rhs = jnp.concatenate([xcat, halves(st)], axis=0)
            y = jnp.dot(lhs, rhs, preferred_element_type=F32)
            if mode == "gated":
                dlane = jnp.where(lo, d_ref[0][:, h0:h0 + 1], d_ref[0][:, h1:h1 + 1])
                yacc_ref[rows, cols] = y + yprev_ref[rows, cols] + xs * dlane
            else:
                y_ref[rows, cols] = y

        if mode == "gated":
            yz = yacc_ref[rows, :] * _silu(z_ref[rows, :].astype(F32))
            ms = jnp.mean(yz * yz, axis=-1, keepdims=True)
            y_ref[rows, :] = (yz * lax.rsqrt(ms + NORM_EPS) * ng_ref[...]).astype(y_ref.dtype)

    starts = list(range(0, x_ref.shape[0], q))
    for r0 in (reversed(starts) if reverse else starts):
        chunk(r0)

    @pl.when(c == pl.num_programs(1) - 1)
    def _():
        sfin_ref[0] = st_ref[...]


def _ssd_scan(xbc, dt_a, dt_b, al_a, al_b, s0, direction, mode, extra=()):
    length = xbc.shape[0]
    groups, hpg = dt_a.shape[1], dt_a.shape[3]
    q = SSD_CHUNK * min(SSD_BLOCK, length // SSD_CHUNK)
    n = SSD_STATE
    pw = 2 * SSD_HEADDIM
    gw = hpg * SSD_HEADDIM
    d_inner = groups * gw
    assert length % q == 0
    nc = length // q
    reverse = direction == 1
    boff = d_inner // n

    def cc(c):
        return nc - 1 - c if reverse else c

    xblk = pl.BlockSpec((q, gw), lambda g, c: (cc(c), g))
    stblk = pl.BlockSpec((1, hpg // 2, n, pw), lambda g, c: (g, 0, 0, 0))
    in_specs = [xblk,
                pl.BlockSpec((q, n), lambda g, c: (cc(c), boff + g)),
                pl.BlockSpec((q, n), lambda g, c: (cc(c), boff + groups + g)),
                pl.BlockSpec((1, 1, q, hpg), lambda g, c: (direction, g, cc(c), 0)),
                pl.BlockSpec((1, 1, hpg, q), lambda g, c: (direction, g, 0, cc(c))),
                pl.BlockSpec((1, 1, 1, hpg), lambda g, c: (direction, g, 0, 0)),
                pl.BlockSpec((1, 1, hpg, 1), lambda g, c: (direction, g, 0, 0)),
                stblk]
    args = [xbc, xbc, xbc, dt_a, dt_b, al_a, al_b, s0]
    out_specs, out_shape = [], []
    scratch = [pltpu.VMEM((hpg // 2, n, pw), F32)]
    if mode == "gated":
        yprev, dskip, z, norm_g = extra
        in_specs += [xblk, pl.BlockSpec((1, 1, hpg), lambda g, c: (g, 0, 0)), xblk,
                     pl.BlockSpec((1, gw), lambda g, c: (0, g))]
        args += [yprev, dskip, z, _row(norm_g)]
        scratch.append(pltpu.VMEM((q, gw), F32))
    if mode != "state":
        out_specs.append(xblk)
        out_shape.append(jax.ShapeDtypeStruct((length, d_inner), BF16 if mode == "gated" else F32))
    out_specs.append(stblk)
    out_shape.append(jax.ShapeDtypeStruct(s0.shape, F32))
    return pl.pallas_call(
        functools.partial(_ssd_kernel, reverse=reverse, mode=mode),
        grid=(groups, nc),
        in_specs=in_specs,
        out_specs=out_specs,
        out_shape=out_shape,
        scratch_shapes=scratch,
        compiler_params=_cp("arbitrary", "arbitrary"),
        name="ssd_scan_%s_%s" % ("bwd" if reverse else "fwd", mode),
    )(*args)


def _ssd_inputs(h, w_bf, conv_w, conv_b, dt_bias, d_inner, conv_dim, heads2, name):
    length = h.shape[0]
    xbc = _up_conv(h, w_bf, conv_dim, d_inner, conv_w, conv_b, name + "_xbc")
    dt = _up(h, w_bf[:, d_inner + conv_dim:], heads2, [0], dt_bias.reshape(-1), _epi_softplus, F32,
             name + "_dt", tn_pref=heads2)
    hpg = heads2 // 2 // SSD_GROUPS
    dt4 = dt.reshape(length, 2, SSD_GROUPS, hpg)
    return xbc, jnp.transpose(dt4, (1, 2, 0, 3)), jnp.transpose(dt4, (1, 2, 3, 0))


def kernel(x, c, ctx, c_ctx, norm_mix_g, norm_ffn_g, mod_wa, mod_wb, mod_b, conv_w1, conv_b1, conv_dw, conv_dw_b, conv_ln_g, conv_ln_b, conv_w2, conv_b2, ssd_w_in, ssd_conv_w, ssd_conv_b, ssd_a_log, ssd_dt_bias, ssd_d, ssd_norm_g, ssd_w_out, sgu_w1, sgu_b1, sgu_ln_g, sgu_ln_b, sgu_ws, sgu_bs, sgu_w2, sgu_b2, ffn_w_in, ffn_dw, ffn_dw_b, ffn_w_out, final_g):
    bsz, seq, d = x.shape
    assert bsz == 1
    depth = norm_mix_g.shape[0]
    xl = x.reshape(seq, d)
    xc = ctx.reshape(ctx.shape[1], d)
    grid_rows = seq // GRID_W
    f_hidden = ffn_w_out.shape[1]

    v8 = jnp.zeros((SUBLANE, d), F32).at[0].set(c[0]).at[1].set(c_ctx)
    mod = _adaln(v8, mod_wa, mod_wb, mod_b)

    def norms(layer, row):
        sh1, sc1, g1, sh2, sc2, g2 = [mod[layer, row, t * d:(t + 1) * d] for t in range(6)]
        return (norm_mix_g[layer], sc1, sh1), (norm_ffn_g[layer], sc2, sh2), g1, g2

    def next_norm(layer, row):
        if layer + 1 < depth:
            return norms(layer + 1, row)[0]
        zeros = jnp.zeros((d,), F32)
        return final_g, zeros, zeros

    ssd_layers = [i for i in range(depth) if i % DEPTH_MIXERS == 1]
    last_ctx = ssd_layers[-1] if ssd_layers else -1

    hl = _prenorm(xl, norms(0, 0)[0])
    hc = _prenorm(xc, norms(0, 1)[0]) if last_ctx >= 0 else None
    for i in range(depth):
        kind, k = i % DEPTH_MIXERS, i // DEPTH_MIXERS
        ctx_full = i < last_ctx
        _, norm2, g1, g2 = norms(i, 0)
        if i <= last_ctx:
            _, cnorm2, cg1, cg2 = norms(i, 1)
        tag = "l%d" % i

        if kind == 0:
            w1 = conv_w1[k].astype(BF16)
            w2 = conv_w2[k].astype(BF16)
            gl = _up(hl, w1, d, [0, d], conv_b1[k], _epi_glu, F32, tag + "_conv_up")
            if k % 2 == 0:
                al = _rowconv(gl, conv_dw[k], conv_dw_b[k], conv_ln_g[k], conv_ln_b[k], seg=GRID_W,
                              tile_rows=2 * GRID_W, name=tag + "_rowconv")
            else:
                gl = _colconv(gl.reshape(grid_rows, GRID_W, d), conv_dw[k], conv_dw_b[k],
                              tag + "_colconv").reshape(seq, d)
                al = _ln_silu(gl, conv_ln_g[k], conv_ln_b[k])
            xl = _down_mm(al, w2, conv_b2[k], xl, g1, tag + "_conv_down")
            if ctx_full:
                gc = _up(hc, w1, d, [0, d], conv_b1[k], _epi_glu, F32, tag + "_conv_up_ctx")
                ac = _rowconv(gc, conv_dw[k], conv_dw_b[k], conv_ln_g[k], conv_ln_b[k], seg=gc.shape[0],
                              tile_rows=gc.shape[0], name=tag + "_seqconv_ctx")
                xc = _down_mm(ac, w2, conv_b2[k], xc, cg1, tag + "_conv_down_ctx")
        elif kind == 1:
            assert not ctx_full
            w_in = ssd_w_in[k].astype(BF16)
            w_out = ssd_w_out[k].astype(BF16)
            d_inner = w_out.shape[0]
            heads2 = 2 * ssd_a_log.shape[2]
            conv_dim = w_in.shape[1] - d_inner - heads2
            hpg = heads2 // 2 // SSD_GROUPS
            al_a = ssd_a_log[k].reshape(2, SSD_GROUPS, 1, hpg)
            al_b = ssd_a_log[k].reshape(2, SSD_GROUPS, hpg, 1)
            dskip = ssd_d[k].reshape(SSD_GROUPS, 1, hpg)
            zero_state = jnp.zeros((SSD_GROUPS, hpg // 2, SSD_STATE, 2 * SSD_HEADDIM), F32)
            xbc_c, dta_c, dtb_c = _ssd_inputs(hc, w_in, ssd_conv_w[k], ssd_conv_b[k], ssd_dt_bias[k],
                                              d_inner, conv_dim, heads2, tag + "_ssd_ctx")
            (state_f,) = _ssd_scan(xbc_c, dta_c, dtb_c, al_a, al_b, zero_state, 0, "state")
            (state_b,) = _ssd_scan(xbc_c, dta_c, dtb_c, al_a, al_b, zero_state, 1, "state")
            xbc_l, dta_l, dtb_l = _ssd_inputs(hl, w_in, ssd_conv_w[k], ssd_conv_b[k], ssd_dt_bias[k],
                                              d_inner, conv_dim, heads2, tag + "_ssd")
            z = _up(hl, w_in, d_inner, [0], None, _epi_id, BF16, tag + "_ssd_z")
            y_f, _ = _ssd_scan(xbc_l, dta_l, dtb_l, al_a, al_b, state_f, 0, "y")
            p, _ = _ssd_scan(xbc_l, dta_l, dtb_l, al_a, al_b, state_b, 1, "gated",
                             (y_f, dskip, z, ssd_norm_g[k]))
            xl = _down_mm(p, w_out, None, xl, g1, tag + "_ssd_down")
        else:
            w1 = sgu_w1[k].astype(BF16)
            w2 = sgu_w2[k].astype(BF16)

            def sgu(h, name):
                a = _up(h, w1, w1.shape[1], [0], sgu_b1[k], _epi_gelu, BF16, name + "_up")
                return _sgu_mix(a, sgu_ln_g[k], sgu_ln_b[k], sgu_ws[k], sgu_bs[k])

            xl = _down_mm(sgu(hl, tag + "_sgu"), w2, sgu_b2[k], xl, g1, tag + "_sgu_down")
            if ctx_full:
                xc = _down_mm(sgu(hc, tag + "_sgu_ctx"), w2, sgu_b2[k], xc, cg1, tag + "_sgu_down_ctx")

        fw_in = ffn_w_in[i].astype(BF16)
        fw_out = ffn_w_out[i].astype(BF16)
        final = i == depth - 1
        a = _up_norm(xl, norm2, fw_in, 2 * f_hidden, BF16, tag + "_ffn_up")
        res = _down_ffn(a, ffn_dw[i], ffn_dw_b[i], fw_out, xl, g2, next_norm(i, 0), True, final, tag + "_ffn_down")
        if final:
            xl = res
        else:
            xl, hl = res
        if ctx_full:
            ac = _up_norm(xc, cnorm2, fw_in, 2 * f_hidden, BF16, tag + "_ffn_up_ctx")
            xc, hc = _down_ffn(ac, ffn_dw[i], ffn_dw_b[i], fw_out, xc, cg2, next_norm(i, 1), False, False,
                               tag + "_ffn_down_ctx")
    return xl.reshape(bsz, seq, d)
```
